```python
import math
import jax, jax.numpy as jnp
from jax import lax
import numpy as np

D_MODEL = 1024
BATCH = 8
SEQ = 8192
DEPTH = 1
DEC_BATCH = 16
DEC_SEQ = 16
PAST_LEN = 2048

CHUNK = 64
Q_BLOCK = 128
EPS = 1e-6
MLA_HEADS = 4
NOPE_DIM = 128
ROPE_DIM = 64
V_DIM = 128
Q_LORA = 256
KV_LORA = 256
ROPE_THETA = 10000.0
MLA_WIDTH = MLA_HEADS * V_DIM
S5_WIDTH = D_MODEL - MLA_WIDTH
S5_GROUP_CH = 16
S5_GROUPS = S5_WIDTH // S5_GROUP_CH
S5_STATE = 64
IN_WIDTH = Q_LORA + KV_LORA + ROPE_DIM + S5_WIDTH
N_EXPERTS = 256
TOP_K = 8
N_EXPERT_GROUPS = 8
TOPK_GROUPS = 4
EXPERT_FF = 256
SHARED_FF = 256
ROUTED_SCALE = 2.5
MOE_BLOCK = 128

kernel_name = "hybrid_mla_s5_moe_adaln_stream_step"

F32 = jnp.float32


def rmsnorm(x, g):
    xf = x.astype(F32)
    y = xf * lax.rsqrt(jnp.mean(xf * xf, axis=-1, keepdims=True) + EPS)
    return (y * g.astype(F32)).astype(x.dtype)


def modulate(x, g, shift, scale):
    return rmsnorm(x, g) * (1.0 + scale[:, None, :]) + shift[:, None, :]


def rope(x, pos):
    half = ROPE_DIM // 2
    inv = ROPE_THETA ** (-jnp.arange(half, dtype=F32) / half)
    ang = pos.astype(F32)[:, None] * inv[None, :]
    cos = jnp.cos(ang)[None, :, None, :]
    sin = jnp.sin(ang)[None, :, None, :]
    xf = x.astype(F32)
    x1, x2 = xf[..., :half], xf[..., half:]
    return jnp.concatenate([x1 * cos - x2 * sin, x1 * sin + x2 * cos], axis=-1).astype(x.dtype)


def mla_attend(q_lat, q_rope, ckv, kr, qpos, kpos):
    scale = (NOPE_DIM + ROPE_DIM) ** -0.5
    s = (jnp.einsum('bqhc,bkc->bhqk', q_lat, ckv, preferred_element_type=F32)
         + jnp.einsum('bqhr,bkr->bhqk', q_rope, kr, preferred_element_type=F32)) * scale
    mask = (kpos // CHUNK)[None, :] <= (qpos // CHUNK)[:, None]
    s = jnp.where(mask[None, None], s, -jnp.inf)
    p = jax.nn.softmax(s, axis=-1)
    return jnp.einsum('bhqk,bkc->bqhc', p.astype(ckv.dtype), ckv)


def mla_prompt(q_lat, q_rope, ckv, kr):
    B, L = ckv.shape[0], ckv.shape[1]
    kpos = jnp.arange(L)

    def block(i):
        s0 = i * Q_BLOCK
        ql = lax.dynamic_slice_in_dim(q_lat, s0, Q_BLOCK, axis=1)
        qr = lax.dynamic_slice_in_dim(q_rope, s0, Q_BLOCK, axis=1)
        return mla_attend(ql, qr, ckv, kr, s0 + jnp.arange(Q_BLOCK), kpos)

    o = lax.map(block, jnp.arange(L // Q_BLOCK))
    return jnp.moveaxis(o, 0, 1).reshape(B, L, MLA_HEADS, KV_LORA)


def s5_params(lp):
    lam = lax.complex(lp['s5_A_re'].astype(F32), lp['s5_A_im'].astype(F32))
    dt = jnp.exp(lp['s5_log_dt'].astype(F32))[:, None]
    a_bar = jnp.exp(lam * dt)
    b = lax.complex(lp['s5_B_re'].astype(F32), lp['s5_B_im'].astype(F32))
    b_bar = ((a_bar - 1.0) / lam)[..., None] * b
    c_mat = lax.complex(lp['s5_C_re'].astype(F32), lp['s5_C_im'].astype(F32))
    d_vec = lp['s5_D'].astype(F32)
    return a_bar, b_bar, c_mat, d_vec


def s5_chunk(h0, u, a_bar, b_bar, c_mat, d_vec):
    bu = jnp.einsum('gnc,blgc->blgn', b_bar, u.astype(jnp.complex64))
    bu = bu.at[:, 0].add(a_bar * h0)
    a = jnp.broadcast_to(a_bar, bu.shape)

    def combine(l, r):
        return (l[0] * r[0], r[0] * l[1] + r[1])

    _, hs = lax.associative_scan(combine, (a, bu), axis=1)
    y = jnp.real(jnp.einsum('gcn,blgn->blgc', c_mat, hs)) + d_vec * u
    return hs[:, -1], y


def s5_prompt(u, a_bar, b_bar, c_mat, d_vec):
    B, L = u.shape[0], u.shape[1]
    uc = jnp.moveaxis(u.reshape(B, L // CHUNK, CHUNK, S5_GROUPS, S5_GROUP_CH), 1, 0)
    h0 = jnp.zeros((B, S5_GROUPS, S5_STATE), jnp.complex64)

    def step(h, uk):
        return s5_chunk(h, uk, a_bar, b_bar, c_mat, d_vec)

    h_last, ys = lax.scan(step, h0, uc)
    return h_last, jnp.moveaxis(ys, 0, 1).reshape(B, L, S5_GROUPS, S5_GROUP_CH)


def moe(h, w_router, b_router, w_eg, w_eu, w_ed, w_sg, w_su, w_sd):
    T = h.shape[0]
    scores = jax.nn.sigmoid(jnp.dot(h, w_router, preferred_element_type=F32))
    sel = scores + b_router.astype(F32)
    grp_score = lax.top_k(sel.reshape(T, N_EXPERT_GROUPS, -1), 2)[0].sum(-1)
    _, gidx = lax.top_k(grp_score, TOPK_GROUPS)
    gmask = jnp.sum(jax.nn.one_hot(gidx, N_EXPERT_GROUPS, dtype=F32), axis=1) > 0
    emask = jnp.repeat(gmask, N_EXPERTS // N_EXPERT_GROUPS, axis=1)
    _, eidx = lax.top_k(jnp.where(emask, sel, -jnp.inf), TOP_K)
    gate = jnp.take_along_axis(scores, eidx, axis=1)
    gate = gate / jnp.sum(gate, axis=-1, keepdims=True) * ROUTED_SCALE

    A = T * TOP_K
    e_flat = eidx.reshape(-1).astype(jnp.int32)
    tok = jnp.arange(A, dtype=jnp.int32) // TOP_K
    order = jnp.argsort(e_flat)
    e_sorted = e_flat[order]
    counts = jnp.bincount(e_flat, length=N_EXPERTS)
    starts = jnp.cumsum(counts) - counts
    nblk = (counts + MOE_BLOCK - 1) // MOE_BLOCK
    blk_end = jnp.cumsum(nblk)
    pstart = (blk_end - nblk) * MOE_BLOCK
    dest = (pstart[e_sorted] + jnp.arange(A, dtype=jnp.int32) - starts[e_sorted]).astype(jnp.int32)
    NB = A // MOE_BLOCK + N_EXPERTS
    xs = jnp.zeros((NB * MOE_BLOCK, h.shape[1]), h.dtype).at[dest].set(h[tok[order]])
    blk_expert = jnp.minimum(jnp.searchsorted(blk_end, jnp.arange(NB, dtype=jnp.int32), side='right'),
                             N_EXPERTS - 1)

    def expert_block(args):
        xb, e = args
        return (jax.nn.silu(xb @ w_eg[e]) * (xb @ w_eu[e])) @ w_ed[e]

    ys = lax.map(expert_block, (xs.reshape(NB, MOE_BLOCK, -1), blk_expert)).reshape(NB * MOE_BLOCK, -1)
    dest_tok = jnp.zeros((A,), jnp.int32).at[order].set(dest)
    routed = jnp.sum(ys[dest_tok].reshape(T, TOP_K, -1) * gate[..., None].astype(h.dtype), axis=1)
    shared = (jax.nn.silu(h @ w_sg) * (h @ w_su)) @ w_sd
    return routed + shared


def layer(x, c, pos, lp, past):
    B, L, _ = x.shape
    mod = jax.nn.silu(c) @ lp['w_ada'] + lp['b_ada']
    sh_m, sc_m, gt_m, sh_f, sc_f, gt_f = jnp.split(mod, 6, axis=-1)

    h = modulate(x, lp['g_mix'], sh_m, sc_m)
    z = h @ lp['w_in']
    o1, o2, o3 = Q_LORA, Q_LORA + KV_LORA, Q_LORA + KV_LORA + ROPE_DIM
    c_q = rmsnorm(z[..., :o1], lp['g_q'])
    ckv = rmsnorm(z[..., o1:o2], lp['g_kv'])
    kr = rope(z[..., o2:o3][:, :, None, :], pos)[:, :, 0, :]
    u = z[..., o3:].reshape(B, L, S5_GROUPS, S5_GROUP_CH).astype(F32)
    q = (c_q @ lp['w_uq']).reshape(B, L, MLA_HEADS, NOPE_DIM + ROPE_DIM)
    q_rope = rope(q[..., NOPE_DIM:], pos)
    q_lat = jnp.einsum('blhn,chn->blhc', q[..., :NOPE_DIM], lp['w_uk'])
    a_bar, b_bar, c_mat, d_vec = s5_params(lp)

    if past is None:
        o_lat = mla_prompt(q_lat, q_rope, ckv, kr)
        h_last, y_s5 = s5_prompt(u, a_bar, b_bar, c_mat, d_vec)
    else:
        ckv_past, kr_past, s_re, s_im = past
        ckv_all = jnp.concatenate([ckv_past.astype(ckv.dtype), ckv], axis=1)
        kr_all = jnp.concatenate([kr_past.astype(kr.dtype), kr], axis=1)
        o_lat = mla_attend(q_lat, q_rope, ckv_all, kr_all, pos, jnp.arange(ckv_all.shape[1]))
        h0 = lax.complex(s_re.astype(F32), s_im.astype(F32))
        h_last, y_s5 = s5_chunk(h0, u, a_bar, b_bar, c_mat, d_vec)

    o_attn = jnp.einsum('blhc,chv->blhv', o_lat, lp['w_uv']).reshape(B, L, MLA_WIDTH)
    zg = jax.nn.gelu(y_s5.reshape(B, L, S5_WIDTH))
    o_s5 = (zg * jax.nn.sigmoid(zg @ lp['s5_w_glu'].astype(F32) + lp['s5_b_glu'].astype(F32))).astype(x.dtype)
    mix = jnp.concatenate([rmsnorm(o_attn, lp['g_out_attn']), rmsnorm(o_s5, lp['g_out_s5'])], axis=-1) @ lp['w_out']
    x = x + gt_m[:, None, :] * mix

    h = modulate(x, lp['g_ffn'], sh_f, sc_f)

    def moe_fn(t):
        return moe(t, lp['w_router'], lp['b_router'], lp['w_exp_gate'], lp['w_exp_up'], lp['w_exp_down'],
                   lp['w_sh_gate'], lp['w_sh_up'], lp['w_sh_down'])

    if past is None:
        f = lax.map(moe_fn, h)
    else:
        f = moe_fn(h.reshape(B * L, D_MODEL)).reshape(B, L, D_MODEL)
    x = x + gt_f[:, None, :] * f
    return x, (ckv, kr, jnp.real(h_last).astype(x.dtype), jnp.imag(h_last).astype(x.dtype))


def setup_inputs(seed: int = 0) -> dict:
    key = jax.random.key(seed)
    ks = iter(jax.random.split(key, 64))

    def nrm(shape, s):
        return jax.random.normal(next(ks), shape, F32) * s

    def gain(shape):
        return 1.0 + nrm(shape, 0.01)

    D = D_MODEL
    G, N, CH = S5_GROUPS, S5_STATE, S5_GROUP_CH
    n_idx = jnp.arange(N, dtype=F32)
    return {
        'x_prompt': nrm((BATCH, SEQ, D), 1.0),
        'x_sample': nrm((DEC_BATCH, DEC_SEQ, D), 1.0),
        'c_prompt': nrm((BATCH, D), 1.0),
        'c_sample': nrm((DEC_BATCH, D), 1.0),
        'cache_ckv': nrm((DEPTH, DEC_BATCH, PAST_LEN, KV_LORA), 1.0),
        'cache_krope': nrm((DEPTH, DEC_BATCH, PAST_LEN, ROPE_DIM), 1.0),
        'state_s5_re': nrm((DEPTH, DEC_BATCH, G, N), 0.1),
        'state_s5_im': nrm((DEPTH, DEC_BATCH, G, N), 0.1),
        'w_ada': nrm((DEPTH, D, 6 * D), 0.5 * D ** -0.5),
        'b_ada': nrm((DEPTH, 6 * D), 0.01),
        'g_mix': gain((DEPTH, D)),
        'g_ffn': gain((DEPTH, D)),
        'w_in': nrm((DEPTH, D, IN_WIDTH), D ** -0.5),
        'g_q': gain((DEPTH, Q_LORA)),
        'w_uq': nrm((DEPTH, Q_LORA, MLA_HEADS * (NOPE_DIM + ROPE_DIM)), Q_LORA ** -0.5),
        'g_kv': gain((DEPTH, KV_LORA)),
        'w_uk': nrm((DEPTH, KV_LORA, MLA_HEADS, NOPE_DIM), KV_LORA ** -0.5),
        'w_uv': nrm((DEPTH, KV_LORA, MLA_HEADS, V_DIM), KV_LORA ** -0.5),
        's5_A_re': -0.5 + nrm((DEPTH, G, N), 0.01),
        's5_A_im': math.pi * n_idx[None, None, :] + nrm((DEPTH, G, N), 0.01),
        's5_B_re': nrm((DEPTH, G, N, CH), (2.0 * CH) ** -0.5),
        's5_B_im': nrm((DEPTH, G, N, CH), (2.0 * CH) ** -0.5),
        's5_C_re': nrm((DEPTH, G, CH, N), (2.0 * N) ** -0.5 * 4.0),
        's5_C_im': nrm((DEPTH, G, CH, N), (2.0 * N) ** -0.5 * 4.0),
        's5_D': nrm((DEPTH, G, CH), 1.0),
        's5_log_dt': jax.random.uniform(next(ks), (DEPTH, G), F32, math.log(1e-3), math.log(1e-1)),
        's5_w_glu': nrm((DEPTH, S5_WIDTH, S5_WIDTH), S5_WIDTH ** -0.5),
        's5_b_glu': nrm((DEPTH, S5_WIDTH), 0.01),
        'g_out_attn': gain((DEPTH, MLA_WIDTH)),
        'g_out_s5': gain((DEPTH, S5_WIDTH)),
        'w_out': nrm((DEPTH, D, D), D ** -0.5),
        'w_router': nrm((DEPTH, D, N_EXPERTS), D ** -0.5),
        'b_router': nrm((DEPTH, N_EXPERTS), 0.01),
        'w_exp_gate': nrm((DEPTH, N_EXPERTS, D, EXPERT_FF), D ** -0.5),
        'w_exp_up': nrm((DEPTH, N_EXPERTS, D, EXPERT_FF), D ** -0.5),
        'w_exp_down': nrm((DEPTH, N_EXPERTS, EXPERT_FF, D), EXPERT_FF ** -0.5),
        'w_sh_gate': nrm((DEPTH, D, SHARED_FF), D ** -0.5),
        'w_sh_up': nrm((DEPTH, D, SHARED_FF), D ** -0.5),
        'w_sh_down': nrm((DEPTH, SHARED_FF, D), SHARED_FF ** -0.5),
        'g_final': gain((D,)),
    }


def reference(x_prompt, x_sample, c_prompt, c_sample, cache_ckv, cache_krope, state_s5_re, state_s5_im,
              w_ada, b_ada, g_mix, g_ffn, w_in, g_q, w_uq, g_kv, w_uk, w_uv,
              s5_A_re, s5_A_im, s5_B_re, s5_B_im, s5_C_re, s5_C_im, s5_D, s5_log_dt, s5_w_glu, s5_b_glu,
              g_out_attn, g_out_s5, w_out, w_router, b_router, w_exp_gate, w_exp_up, w_exp_down,
              w_sh_gate, w_sh_up, w_sh_down, g_final):
    pos_p = jnp.arange(x_prompt.shape[1])
    pos_s = cache_ckv.shape[2] + jnp.arange(x_sample.shape[1])
    xp, xs = x_prompt, x_sample
    ckv_p, kr_p, sre_p, sim_p = [], [], [], []
    ckv_s, kr_s, sre_s, sim_s = [], [], [], []
    for l in range(DEPTH):
        lp = {
            'w_ada': w_ada[l], 'b_ada': b_ada[l], 'g_mix': g_mix[l], 'g_ffn': g_ffn[l],
            'w_in': w_in[l], 'g_q': g_q[l], 'w_uq': w_uq[l], 'g_kv': g_kv[l], 'w_uk': w_uk[l], 'w_uv': w_uv[l],
            's5_A_re': s5_A_re[l], 's5_A_im': s5_A_im[l], 's5_B_re': s5_B_re[l], 's5_B_im': s5_B_im[l],
            's5_C_re': s5_C_re[l], 's5_C_im': s5_C_im[l], 's5_D': s5_D[l], 's5_log_dt': s5_log_dt[l],
            's5_w_glu': s5_w_glu[l], 's5_b_glu': s5_b_glu[l],
            'g_out_attn': g_out_attn[l], 'g_out_s5': g_out_s5[l], 'w_out': w_out[l],
            'w_router': w_router[l], 'b_router': b_router[l],
            'w_exp_gate': w_exp_gate[l], 'w_exp_up': w_exp_up[l], 'w_exp_down': w_exp_down[l],
            'w_sh_gate': w_sh_gate[l], 'w_sh_up': w_sh_up[l], 'w_sh_down': w_sh_down[l],
        }
        xp, (a, b, cr, ci) = layer(xp, c_prompt, pos_p, lp, None)
        ckv_p.append(a); kr_p.append(b); sre_p.append(cr); sim_p.append(ci)
        past = (cache_ckv[l], cache_krope[l], state_s5_re[l], state_s5_im[l])
        xs, (a, b, cr, ci) = layer(xs, c_sample, pos_s, lp, past)
        ckv_s.append(a); kr_s.append(b); sre_s.append(cr); sim_s.append(ci)
    y_prompt = rmsnorm(xp, g_final)
    y_sample = rmsnorm(xs, g_final)
    return (y_prompt, y_sample,
            jnp.stack(ckv_p), jnp.stack(kr_p), jnp.stack(sre_p), jnp.stack(sim_p),
            jnp.stack(ckv_s), jnp.stack(kr_s), jnp.stack(sre_s), jnp.stack(sim_s))
```

```python
import functools
import math

import numpy as np
import jax
import jax.numpy as jnp
from jax import lax
from jax.experimental import pallas as pl
from jax.experimental.pallas import tpu as pltpu

F32 = jnp.float32
BF16 = jnp.bfloat16
I32 = jnp.int32

EPS = 1e-6
CHUNK = 64
MLA_HEADS = 4
NOPE_DIM = 128
ROPE_DIM = 64
V_DIM = 128
Q_LORA = 256
KV_LORA = 256
QK_DIM = KV_LORA + ROPE_DIM
ROPE_THETA = 10000.0
S5_GROUP_CH = 16
S5_STATE = 64
N_EXPERTS = 256
TOP_K = 8
N_EXPERT_GROUPS = 8
TOPK_GROUPS = 4
ROUTED_SCALE = 2.5

S5_SUB = 16
MOE_ROWS = 256
LANES = 128
VMEM_LIMIT = 56 * 1024 * 1024

NEG_INF = float("-inf")
BIG_I32 = 1 << 30


def _dot(a, b):
    return jnp.dot(a, b, preferred_element_type=F32)


def _dot_nt(a, b):
    return lax.dot_general(a, b, (((1,), (1,)), ((), ())), preferred_element_type=F32)


def _split(a):
    hi = a.astype(BF16)
    lo = (a - hi.astype(F32)).astype(BF16)
    return hi, lo


def _rms(x, g):
    return x * lax.rsqrt(jnp.mean(x * x, axis=-1, keepdims=True) + EPS) * g


def _silu(x):
    return x * jax.nn.sigmoid(x)


def _params(sem):
    return pltpu.CompilerParams(dimension_semantics=sem, vmem_limit_bytes=VMEM_LIMIT)


def _const_spec(shape):
    nd = len(shape)
    return pl.BlockSpec(shape, lambda *_: (0,) * nd)


def _ada_kernel(c_ref, whi_ref, wlo_ref, b_ref, o_ref):
    c = c_ref[...]
    s_hi, s_lo = _split(_silu(c))
    w_hi = whi_ref[...]
    o_ref[...] = _dot(s_hi, w_hi) + _dot(s_hi, wlo_ref[...]) + _dot(s_lo, w_hi) + b_ref[...]


def _ada(c, w_ada, b_ada):
    rows, d = c.shape
    n = w_ada.shape[1]
    tn = 512
    w_hi, w_lo = _split(w_ada)
    return pl.pallas_call(
        _ada_kernel,
        grid=(n // tn,),
        in_specs=[_const_spec((rows, d)),
                  pl.BlockSpec((d, tn), lambda j: (0, j)),
                  pl.BlockSpec((d, tn), lambda j: (0, j)),
                  pl.BlockSpec((1, tn), lambda j: (0, j))],
        out_specs=pl.BlockSpec((rows, tn), lambda j: (0, j)),
        out_shape=jax.ShapeDtypeStruct((rows, n), F32),
        compiler_params=_params(("arbitrary",)),
        name="ada",
    )(c, w_hi, w_lo, b_ada.reshape(1, n))


def _pre_kernel(x_ref, sh_ref, sc_ref, g_ref, win_ref, gq_ref, gkv_ref, wqn_ref, wqr_ref, wqt_ref,
                wuk_ref, cos_ref, sin_ref, ckv_ref, kr_ref, u_ref, kcat_ref, q_ref, *, scale):
    x = x_ref[0]
    h = _rms(x, g_ref[...]) * (1.0 + sc_ref[0]) + sh_ref[0]
    z = _dot(h.astype(BF16), win_ref[...])
    cq = _rms(z[:, :Q_LORA], gq_ref[...])
    ckv = _rms(z[:, Q_LORA:Q_LORA + KV_LORA], gkv_ref[...])
    o_s5 = Q_LORA + KV_LORA
    s5w = u_ref.shape[-1]
    u_ref[0] = z[:, o_s5:o_s5 + s5w]
    o_r = o_s5 + s5w
    cos = cos_ref[...]
    sin = sin_ref[...]
    kr = z[:, o_r:o_r + ROPE_DIM] * cos[:, :ROPE_DIM] + z[:, o_r + ROPE_DIM:o_r + 2 * ROPE_DIM] * sin[:, :ROPE_DIM]
    ckv_ref[0] = ckv
    kr_ref[0] = kr
    kcat_ref[0, :, :KV_LORA] = ckv.astype(BF16)
    kcat_ref[0, :, KV_LORA:] = kr.astype(BF16)
    cqb = cq.astype(BF16)
    qn = _dot(cqb, wqn_ref[...])
    qr = (_dot(cqb, wqr_ref[...]) * cos + _dot(cqb, wqt_ref[...]) * sin) * scale
    for hd in range(MLA_HEADS):
        ql = _dot(qn[:, hd * NOPE_DIM:(hd + 1) * NOPE_DIM].astype(BF16), wuk_ref[hd]) * scale
        q_ref[0, hd, :, :KV_LORA] = ql.astype(BF16)
        q_ref[0, hd, :, KV_LORA:] = qr[:, hd * ROPE_DIM:(hd + 1) * ROPE_DIM].astype(BF16)


def _pre(x, shift, scale_mod, cos_t, sin_t, wts, tm):
    B, L, D = x.shape
    nt = L // tm
    per_row = shift.shape[1] != 1
    mod_spec = (pl.BlockSpec((1, tm, D), lambda b, i: (b, i, 0)) if per_row
                else pl.BlockSpec((1, 1, D), lambda b, i: (b, 0, 0)))
    s5w = wts["s5w"]
    hr = MLA_HEADS * ROPE_DIM
    kern = functools.partial(_pre_kernel, scale=(NOPE_DIM + ROPE_DIM) ** -0.5)
    consts = [wts["g_mix"], wts["w_in"], wts["g_q"], wts["g_kv"], wts["wq_nope"], wts["wq_rope"],
              wts["wq_rot"], wts["w_ukT"]]
    in_specs = [pl.BlockSpec((1, tm, D), lambda b, i: (b, i, 0)), mod_spec, mod_spec]
    in_specs += [_const_spec(c.shape) for c in consts]
    in_specs += [pl.BlockSpec((tm, hr), lambda b, i: (i, 0)), pl.BlockSpec((tm, hr), lambda b, i: (i, 0))]
    out_shape = (jax.ShapeDtypeStruct((B, L, KV_LORA), F32),
                 jax.ShapeDtypeStruct((B, L, ROPE_DIM), F32),
                 jax.ShapeDtypeStruct((B, L, s5w), F32),
                 jax.ShapeDtypeStruct((B, L, QK_DIM), BF16),
                 jax.ShapeDtypeStruct((B, MLA_HEADS, L, QK_DIM), BF16))
    out_specs = (pl.BlockSpec((1, tm, KV_LORA), lambda b, i: (b, i, 0)),
                 pl.BlockSpec((1, tm, ROPE_DIM), lambda b, i: (b, i, 0)),
                 pl.BlockSpec((1, tm, s5w), lambda b, i: (b, i, 0)),
                 pl.BlockSpec((1, tm, QK_DIM), lambda b, i: (b, i, 0)),
                 pl.BlockSpec((1, MLA_HEADS, tm, QK_DIM), lambda b, i: (b, 0, i, 0)))
    return pl.pallas_call(
        kern, grid=(B, nt), in_specs=in_specs, out_specs=out_specs, out_shape=out_shape,
        compiler_params=_params(("arbitrary", "arbitrary")), name="pre",
    )(x, shift, scale_mod, *consts, cos_t, sin_t)


def _attn_kernel(q_ref, k_ref, o_ref, m_scr, l_scr, acc_scr, *, tq, tk):
    i = pl.program_id(1)
    rows = MLA_HEADS * tq
    q = q_ref[0].reshape(rows, QK_DIM)
    m_scr[...] = jnp.full(m_scr.shape, NEG_INF, F32)
    l_scr[...] = jnp.zeros(l_scr.shape, F32)
    acc_scr[...] = jnp.zeros(acc_scr.shape, F32)

    def step(j0, masked):
        k = k_ref[0, pl.ds(j0, tk), :]
        s = _dot_nt(q, k)
        if masked:
            qpos = i * tq + lax.rem(lax.broadcasted_iota(I32, (rows, tk), 0), tq)
            kpos = j0 + lax.broadcasted_iota(I32, (rows, tk), 1)
            s = jnp.where(kpos // CHUNK <= qpos // CHUNK, s, NEG_INF)
        m_prev = m_scr[...]
        m_next = jnp.maximum(m_prev, jnp.max(s, axis=1, keepdims=True))
        alpha = jnp.exp(m_prev - m_next)
        p = jnp.exp(s - pltpu.repeat(m_next, tk // LANES, axis=1))
        l_scr[...] = alpha * l_scr[...] + jnp.sum(p, axis=1, keepdims=True)
        m_scr[...] = m_next
        pv = _dot(p.astype(BF16), k[:, :KV_LORA])
        acc_scr[...] = acc_scr[...] * pltpu.repeat(alpha, KV_LORA // LANES, axis=1) + pv

    n_full = (i * tq) // tk

    def body(j, carry):
        step(pl.multiple_of(j * tk, tk), False)
        return carry

    lax.fori_loop(0, n_full, body, 0)
    step(pl.multiple_of(n_full * tk, tk), True)
    inv = 1.0 / l_scr[...]
    o = acc_scr[...] * pltpu.repeat(inv, KV_LORA // LANES, axis=1)
    o_ref[0] = o.astype(BF16).reshape(MLA_HEADS, tq, KV_LORA)


def _attn_prompt(q, kcat, tq, tk):
    B, H, L, _ = q.shape
    rows = H * tq
    kern = functools.partial(_attn_kernel, tq=tq, tk=tk)
    return pl.pallas_call(
        kern, grid=(B, L // tq),
        in_specs=[pl.BlockSpec((1, H, tq, QK_DIM), lambda b, i: (b, 0, i, 0)),
                  pl.BlockSpec((1, L, QK_DIM), lambda b, i: (b, 0, 0))],
        out_specs=pl.BlockSpec((1, H, tq, KV_LORA), lambda b, i: (b, 0, i, 0)),
        out_shape=jax.ShapeDtypeStruct((B, H, L, KV_LORA), BF16),
        scratch_shapes=[pltpu.VMEM((rows, LANES), F32), pltpu.VMEM((rows, LANES), F32),
                        pltpu.VMEM((rows, KV_LORA), F32)],
        compiler_params=_params(("arbitrary", "arbitrary")), name="attn_prompt",
    )(q, kcat)


def _attn_sample_kernel(q_ref, kn_ref, pc_ref, pr_ref, o_ref, *, past, lq):
    rows = MLA_HEADS * lq
    q = q_ref[0].reshape(rows, QK_DIM)
    pc = pc_ref[0].astype(BF16)
    pr = pr_ref[0].astype(BF16)
    kn = kn_ref[0]
    s_p = _dot_nt(q[:, :KV_LORA], pc) + _dot_nt(q[:, KV_LORA:], pr)
    s_n = _dot_nt(q, kn)
    qpos_p = past + lax.rem(lax.broadcasted_iota(I32, (rows, past), 0), lq)
    kpos_p = lax.broadcasted_iota(I32, (rows, past), 1)
    s_p = jnp.where(kpos_p // CHUNK <= qpos_p // CHUNK, s_p, NEG_INF)
    qpos_n = past + lax.rem(lax.broadcasted_iota(I32, (rows, lq), 0), lq)
    kpos_n = past + lax.broadcasted_iota(I32, (rows, lq), 1)
    s_n = jnp.where(kpos_n // CHUNK <= qpos_n // CHUNK, s_n, NEG_INF)
    m = jnp.maximum(jnp.max(s_p, axis=1, keepdims=True), jnp.max(s_n, axis=1, keepdims=True))
    p_p = jnp.exp(s_p - m)
    p_n = jnp.exp(s_n - m)
    l = jnp.sum(p_p, axis=1, keepdims=True) + jnp.sum(p_n, axis=1, keepdims=True)
    o = _dot(p_p.astype(BF16), pc) + _dot(p_n.astype(BF16), kn[:, :KV_LORA])
    o_ref[0] = (o / l).astype(BF16).reshape(MLA_HEADS, lq, KV_LORA)


def _attn_sample(q, kcat, past_ckv, past_kr):
    B, lq, _ = kcat.shape
    past = past_ckv.shape[1]
    H = MLA_HEADS
    kern = functools.partial(_attn_sample_kernel, past=past, lq=lq)
    return pl.pallas_call(
        kern, grid=(B,),
        in_specs=[pl.BlockSpec((1, H, lq, QK_DIM), lambda b: (0, 0, b, 0)),
                  pl.BlockSpec((1, lq, QK_DIM), lambda b: (b, 0, 0)),
                  pl.BlockSpec((1, past, KV_LORA), lambda b: (b, 0, 0)),
                  pl.BlockSpec((1, past, ROPE_DIM), lambda b: (b, 0, 0))],
        out_specs=pl.BlockSpec((1, H, lq, KV_LORA), lambda b: (0, 0, b, 0)),
        out_shape=jax.ShapeDtypeStruct((1, H, B * lq, KV_LORA), BF16),
        compiler_params=_params(("arbitrary",)), name="attn_sample",
    )(q, kcat, past_ckv, past_kr)


def _s5_kernel(u_ref, h0_ref, bre_ref, bim_ref, brel_ref, biml_ref, lt_ref, pinr_ref, pini_ref,
               pwr_ref, pwi_ref, a_ref, cre_ref, cim_ref, d_ref, wglu_ref, bglu_ref, gout_ref,
               o_ref, hl_ref, st_scr, cum_scr, hs_scr, *, tm, precise):
    i = pl.program_id(1)
    ns = st_scr.shape[1]
    half = ns // 2
    wh = u_ref.shape[-1] // 2

    @pl.when(i == 0)
    def _():
        st_scr[...] = h0_ref[0]

    u = u_ref[0]
    lt = lt_ref[...]
    for hf in range(2):
        uh = u[:, hf * wh:(hf + 1) * wh]
        if precise:
            u_hi, u_lo = _split(uh)
            bu_re = _dot(u_hi, bre_ref[hf]) + _dot(u_lo, bre_ref[hf]) + _dot(u_hi, brel_ref[hf])
            bu_im = _dot(u_hi, bim_ref[hf]) + _dot(u_lo, bim_ref[hf]) + _dot(u_hi, biml_ref[hf])
        else:
            u_hi = uh.astype(BF16)
            bu_re = _dot(u_hi, bre_ref[hf])
            bu_im = _dot(u_hi, bim_ref[hf])
        sl = slice(hf * half, (hf + 1) * half)
        pr = pinr_ref[:, sl]
        pi = pini_ref[:, sl]
        v_re = pr * bu_re - pi * bu_im
        v_im = pr * bu_im + pi * bu_re
        for part, v in ((0, v_re), (1, v_im)):
            if precise:
                v_hi, v_lo = _split(v)
                c = _dot(lt, v_hi) + _dot(lt, v_lo)
            else:
                c = _dot(lt, v.astype(BF16))
            cum_scr[:, part * ns + hf * half:part * ns + (hf + 1) * half] = c

    a_re = a_ref[0:1, :]
    a_im = a_ref[1:2, :]
    pw_re = pwr_ref[...]
    pw_im = pwi_ref[...]

    def chunk(c, carry):
        s_re, s_im = carry
        r0 = pl.multiple_of(c * S5_SUB, S5_SUB)
        t_re = cum_scr[pl.ds(r0, S5_SUB), 0:ns] + (a_re * s_re - a_im * s_im)
        t_im = cum_scr[pl.ds(r0, S5_SUB), ns:2 * ns] + (a_re * s_im + a_im * s_re)
        h_re = pw_re * t_re - pw_im * t_im
        h_im = pw_re * t_im + pw_im * t_re
        hs_scr[pl.ds(r0, S5_SUB), 0:ns] = h_re.astype(BF16)
        hs_scr[pl.ds(r0, S5_SUB), ns:2 * ns] = h_im.astype(BF16)
        return h_re[S5_SUB - 1:S5_SUB, :], h_im[S5_SUB - 1:S5_SUB, :]

    s_re, s_im = lax.fori_loop(0, tm // S5_SUB, chunk, (st_scr[0:1, :], st_scr[1:2, :]))
    st_scr[0:1, :] = s_re
    st_scr[1:2, :] = s_im
    hl_ref[0, 0:1, :] = s_re
    hl_ref[0, 1:2, :] = s_im

    ys = []
    for hf in range(2):
        hre = hs_scr[:, hf * half:(hf + 1) * half]
        him = hs_scr[:, ns + hf * half:ns + (hf + 1) * half]
        ys.append(_dot(hre, cre_ref[hf]) + _dot(him, cim_ref[hf]))
    y = jnp.concatenate(ys, axis=1) + d_ref[...] * u
    zg = jax.nn.gelu(y)
    gl = _dot(zg.astype(BF16), wglu_ref[...]) + bglu_ref[...]
    o = zg * jax.nn.sigmoid(gl)
    o_ref[0] = _rms(o, gout_ref[...]).astype(BF16)


def _s5(u, h0, tabs, wts, tm, precise):
    B, L, W = u.shape
    ns = h0.shape[-1]
    consts = [tabs["b_re"], tabs["b_im"], tabs["b_re_lo"], tabs["b_im_lo"], tabs["lt"], tabs["pin_re"],
              tabs["pin_im"], tabs["pw_re"], tabs["pw_im"], tabs["a"], tabs["c_re"], tabs["c_im"],
              wts["s5_d"], wts["w_glu"], wts["b_glu"], wts["g_out_s5"]]
    kern = functools.partial(_s5_kernel, tm=tm, precise=precise)
    return pl.pallas_call(
        kern, grid=(B, L // tm),
        in_specs=[pl.BlockSpec((1, tm, W), lambda b, i: (b, i, 0)),
                  pl.BlockSpec((1, 2, ns), lambda b, i: (b, 0, 0))] + [_const_spec(c.shape) for c in consts],
        out_specs=(pl.BlockSpec((1, tm, W), lambda b, i: (b, i, 0)),
                   pl.BlockSpec((1, 2, ns), lambda b, i: (b, 0, 0))),
        out_shape=(jax.ShapeDtypeStruct((B, L, W), BF16), jax.ShapeDtypeStruct((B, 2, ns), F32)),
        scratch_shapes=[pltpu.VMEM((2, ns), F32), pltpu.VMEM((tm, 2 * ns), F32), pltpu.VMEM((tm, 2 * ns), BF16)],
        compiler_params=_params(("arbitrary", "arbitrary")), name="s5",
    )(u, h0, *consts)


def _s5_tables(a_re_p, a_im_p, b_re_p, b_im_p, c_re_p, c_im_p, log_dt, tm):
    G, N = a_re_p.shape
    CH = b_re_p.shape[-1]
    dt = jnp.exp(log_dt.astype(F32))[:, None]
    lr = a_re_p.astype(F32) * dt
    li = a_im_p.astype(F32) * dt
    er = jnp.exp(lr)
    ab_re, ab_im = er * jnp.cos(li), er * jnp.sin(li)
    lam2 = a_re_p.astype(F32) ** 2 + a_im_p.astype(F32) ** 2
    nr, ni = ab_re - 1.0, ab_im
    f_re = (nr * a_re_p + ni * a_im_p) / lam2
    f_im = (ni * a_re_p - nr * a_im_p) / lam2
    bb_re = f_re[..., None] * b_re_p - f_im[..., None] * b_im_p
    bb_im = f_re[..., None] * b_im_p + f_im[..., None] * b_re_p
    gh = G // 2
    eye = jnp.eye(gh, dtype=F32)

    def blk_b(bb):
        t = bb.reshape(2, gh, N, CH)
        m = jnp.einsum("hgnc,gk->hgckn", t, eye)
        return m.reshape(2, gh * CH, gh * N)

    def blk_c(cc):
        t = cc.reshape(2, gh, CH, N)
        m = jnp.einsum("hgcn,gk->hgnkc", t, eye)
        return m.reshape(2, gh * N, gh * CH)

    b_re_m, b_im_m = blk_b(bb_re), blk_b(bb_im)
    b_re_hi, b_re_lo = _split(b_re_m)
    b_im_hi, b_im_lo = _split(b_im_m)
    s = jnp.arange(S5_SUB, dtype=F32)[:, None, None]

    def powers(sign):
        e = jnp.exp(sign * lr[None] * s)
        return ((e * jnp.cos(sign * li[None] * s)).reshape(S5_SUB, G * N),
                (e * jnp.sin(sign * li[None] * s)).reshape(S5_SUB, G * N))

    pin_re, pin_im = powers(-1.0)
    pw_re, pw_im = powers(1.0)
    reps = tm // S5_SUB
    r = np.arange(tm)
    lt = ((r[:, None] // S5_SUB == r[None, :] // S5_SUB) & (r[None, :] <= r[:, None])).astype(np.float32)
    return {
        "b_re": b_re_hi, "b_im": b_im_hi, "b_re_lo": b_re_lo, "b_im_lo": b_im_lo,
        "lt": jnp.asarray(lt, BF16),
        "pin_re": jnp.tile(pin_re, (reps, 1)), "pin_im": jnp.tile(pin_im, (reps, 1)),
        "pw_re": pw_re, "pw_im": pw_im,
        "a": jnp.stack([ab_re.reshape(G * N), ab_im.reshape(G * N)]),
        "c_re": blk_c(c_re_p.astype(F32)).astype(BF16), "c_im": (-blk_c(c_im_p.astype(F32))).astype(BF16),
    }


def _route(scores, sel):
    E, tm = scores.shape
    ge = E // N_EXPERT_GROUPS
    io_g = lax.broadcasted_iota(I32, (ge, tm), 0)
    gs_rows = []
    for g in range(N_EXPERT_GROUPS):
        sg = sel[g * ge:(g + 1) * ge, :]
        m1 = jnp.max(sg, axis=0, keepdims=True)
        i1 = jnp.min(jnp.where(sg == m1, io_g, BIG_I32), axis=0, keepdims=True)
        m2 = jnp.max(jnp.where(io_g == i1, NEG_INF, sg), axis=0, keepdims=True)
        gs_rows.append(m1 + m2)
    gs = jnp.concatenate(gs_rows, axis=0)
    gio = lax.broadcasted_iota(I32, gs.shape, 0)
    gsel = jnp.zeros(gs.shape, F32)
    for _ in range(TOPK_GROUPS):
        mx = jnp.max(gs, axis=0, keepdims=True)
        ix = jnp.min(jnp.where(gs == mx, gio, BIG_I32), axis=0, keepdims=True)
        hit = gio == ix
        gsel = jnp.where(hit, 1.0, gsel)
        gs = jnp.where(hit, NEG_INF, gs)
    emask = jnp.concatenate([jnp.broadcast_to(gsel[g:g + 1, :], (ge, tm)) for g in range(N_EXPERT_GROUPS)], axis=0)
    cand = jnp.where(emask > 0.0, sel, NEG_INF)
    eio = lax.broadcasted_iota(I32, (E, tm), 0)
    idxs, gates = [], []
    for _ in range(TOP_K):
        mx = jnp.max(cand, axis=0, keepdims=True)
        ix = jnp.min(jnp.where(cand == mx, eio, BIG_I32), axis=0, keepdims=True)
        hit = eio == ix
        gates.append(jnp.sum(jnp.where(hit, scores, 0.0), axis=0, keepdims=True))
        idxs.append(ix)
        cand = jnp.where(hit, NEG_INF, cand)
    eidx = jnp.concatenate(idxs, axis=0)
    gate = jnp.concatenate(gates, axis=0)
    gate = gate / jnp.sum(gate, axis=0, keepdims=True) * ROUTED_SCALE
    return eidx, gate


def _post_kernel(x_ref, ol_ref, os_ref, gtm_ref, shf_ref, scf_ref, gtf_ref, wuv_ref, goa_ref, wout_ref,
                 gffn_ref, wrh_ref, wrl_ref, br_ref, wsg_ref, wsu_ref, wsd_ref,
                 h2_ref, base_ref, eidx_ref, gate_ref):
    x = x_ref[0]
    oa = jnp.concatenate([_dot(ol_ref[0, hd], wuv_ref[hd]) for hd in range(MLA_HEADS)], axis=1)
    oan = _rms(oa, goa_ref[...]).astype(BF16)
    wa = oan.shape[1]
    mix = _dot(oan, wout_ref[:wa, :]) + _dot(os_ref[0], wout_ref[wa:, :])
    x1 = x + gtm_ref[0] * mix
    h2 = _rms(x1, gffn_ref[...]) * (1.0 + scf_ref[0]) + shf_ref[0]
    h2_hi, h2_lo = _split(h2)
    h2_ref[0] = h2_hi
    sh = _dot((_silu(_dot(h2_hi, wsg_ref[...])) * _dot(h2_hi, wsu_ref[...])).astype(BF16), wsd_ref[...])
    base_ref[0] = x1 + gtf_ref[0] * sh
    wr_hi = wrh_ref[...]
    logits = _dot_nt(wr_hi, h2_hi) + _dot_nt(wrl_ref[...], h2_hi) + _dot_nt(wr_hi, h2_lo)
    scores = jax.nn.sigmoid(logits)
    eidx, gate = _route(scores, scores + br_ref[...])
    eidx_ref[...] = eidx
    gate_ref[...] = gate


def _post(x, o_lat, o_s5, mods, wts, tm):
    B, L, D = x.shape
    nt = L // tm
    per_row = mods[0].shape[1] != 1
    mod_spec = (pl.BlockSpec((1, tm, D), lambda b, i: (b, i, 0)) if per_row
                else pl.BlockSpec((1, 1, D), lambda b, i: (b, 0, 0)))
    W = o_s5.shape[-1]
    consts = [wts["w_uv"], wts["g_out_attn"], wts["w_out"], wts["g_ffn"], wts["wr_hi"], wts["wr_lo"],
              wts["b_router"], wts["w_sg"], wts["w_su"], wts["w_sd"]]
    in_specs = [pl.BlockSpec((1, tm, D), lambda b, i: (b, i, 0)),
                pl.BlockSpec((1, MLA_HEADS, tm, KV_LORA), lambda b, i: (b, 0, i, 0)),
                pl.BlockSpec((1, tm, W), lambda b, i: (b, i, 0)),
                mod_spec, mod_spec, mod_spec, mod_spec] + [_const_spec(c.shape) for c in consts]
    out_shape = (jax.ShapeDtypeStruct((B, L, D), BF16), jax.ShapeDtypeStruct((B, L, D), F32),
                 jax.ShapeDtypeStruct((TOP_K, B * L), I32), jax.ShapeDtypeStruct((TOP_K, B * L), F32))
    out_specs = (pl.BlockSpec((1, tm, D), lambda b, i: (b, i, 0)),
                 pl.BlockSpec((1, tm, D), lambda b, i: (b, i, 0)),
                 pl.BlockSpec((TOP_K, tm), lambda b, i: (0, b * nt + i)),
                 pl.BlockSpec((TOP_K, tm), lambda b, i: (0, b * nt + i)))
    return pl.pallas_call(
        _post_kernel, grid=(B, nt), in_specs=in_specs, out_specs=out_specs, out_shape=out_shape,
        compiler_params=_params(("arbitrary", "arbitrary")), name="post",
    )(x, o_lat, o_s5, *mods, *consts)


def _rank_kernel(eidx_ref, tri_ref, rank_ref, cnt_ref, carry_scr):
    i = pl.program_id(0)

    @pl.when(i == 0)
    def _():
        carry_scr[...] = jnp.zeros(carry_scr.shape, F32)

    eidx = eidx_ref[...]
    tt = eidx.shape[1]
    eio = lax.broadcasted_iota(I32, (N_EXPERTS, tt), 0)
    hits = [eio == eidx[k:k + 1, :] for k in range(TOP_K)]
    onehot = jnp.zeros((N_EXPERTS, tt), F32)
    for hit in hits:
        onehot = jnp.where(hit, 1.0, onehot)
    before = _dot(onehot.astype(BF16), tri_ref[...]) + carry_scr[...]
    ranks = [jnp.sum(jnp.where(hit, before, 0.0), axis=0, keepdims=True) for hit in hits]
    rank_ref[...] = jnp.concatenate(ranks, axis=0).astype(I32)
    carry = carry_scr[...] + jnp.sum(onehot, axis=1, keepdims=True)
    carry_scr[...] = carry
    cnt_ref[...] = carry


def _rank(eidx, tt):
    K, T = eidx.shape
    r = np.arange(tt)
    tri = jnp.asarray((r[:, None] < r[None, :]).astype(np.float32), BF16)
    return pl.pallas_call(
        _rank_kernel, grid=(T // tt,),
        in_specs=[pl.BlockSpec((K, tt), lambda i: (0, i)), _const_spec((tt, tt))],
        out_specs=(pl.BlockSpec((K, tt), lambda i: (0, i)), _const_spec((N_EXPERTS, 1))),
        out_shape=(jax.ShapeDtypeStruct((K, T), I32), jax.ShapeDtypeStruct((N_EXPERTS, 1), F32)),
        scratch_shapes=[pltpu.VMEM((N_EXPERTS, 1), F32)],
        compiler_params=_params(("arbitrary",)), name="rank",
    )(eidx, tri)


def _expert_kernel(blk_e_ref, nused_ref, x_ref, wg_ref, wu_ref, wd_ref, o_ref):
    b = pl.program_id(0)

    @pl.when(b < nused_ref[0])
    def _():
        x = x_ref[...]
        g = _dot(x, wg_ref[0])
        u = _dot(x, wu_ref[0])
        o_ref[...] = _dot((_silu(g) * u).astype(BF16), wd_ref[0]).astype(BF16)


def _experts(xs, blk_e, nused, wg, wu, wd):
    rows, D = xs.shape
    nb = rows // MOE_ROWS
    F = wg.shape[-1]

    def xmap(b, blk_e_ref, nused_ref):
        return (jnp.minimum(b, nused_ref[0] - 1), 0)

    def wmap(b, blk_e_ref, nused_ref):
        return (blk_e_ref[b], 0, 0)

    grid_spec = pltpu.PrefetchScalarGridSpec(
        num_scalar_prefetch=2, grid=(nb,),
        in_specs=[pl.BlockSpec((MOE_ROWS, D), xmap),
                  pl.BlockSpec((1, D, F), wmap), pl.BlockSpec((1, D, F), wmap), pl.BlockSpec((1, F, D), wmap)],
        out_specs=pl.BlockSpec((MOE_ROWS, D), xmap))
    return pl.pallas_call(
        _expert_kernel, grid_spec=grid_spec, out_shape=jax.ShapeDtypeStruct((rows, D), BF16),
        compiler_params=_params(("arbitrary",)), name="experts",
    )(blk_e, nused, xs, wg, wu, wd)


def _final_kernel(base_ref, ys_ref, gate_ref, gtf_ref, gfin_ref, o_ref):
    D = base_ref.shape[-1]
    gate = gate_ref[...]
    routed = jnp.zeros(base_ref.shape[1:], F32)
    for k in range(TOP_K):
        routed = routed + gate[:, k:k + 1] * ys_ref[:, k * D:(k + 1) * D].astype(F32)
    y = base_ref[0] + gtf_ref[0] * routed
    o_ref[0] = _rms(y, gfin_ref[...])


def _final(base, ysg, gate_t, gt_f, g_final, tm, row0):
    B, L, D = base.shape
    nt = L // tm
    off = row0 // tm
    per_row = gt_f.shape[1] != 1
    mod_spec = (pl.BlockSpec((1, tm, D), lambda b, i: (b, i, 0)) if per_row
                else pl.BlockSpec((1, 1, D), lambda b, i: (b, 0, 0)))
    return pl.pallas_call(
        _final_kernel, grid=(B, nt),
        in_specs=[pl.BlockSpec((1, tm, D), lambda b, i: (b, i, 0)),
                  pl.BlockSpec((tm, TOP_K * D), lambda b, i: (off + b * nt + i, 0)),
                  pl.BlockSpec((tm, TOP_K), lambda b, i: (off + b * nt + i, 0)),
                  mod_spec, _const_spec((1, D))],
        out_specs=pl.BlockSpec((1, tm, D), lambda b, i: (b, i, 0)),
        out_shape=jax.ShapeDtypeStruct((B, L, D), F32),
        compiler_params=_params(("arbitrary", "arbitrary")), name="final",
    )(base, ysg, gate_t, gt_f, g_final)


def _rope_tables(pos):
    half = ROPE_DIM // 2
    inv = ROPE_THETA ** (-jnp.arange(half, dtype=F32) / half)
    ang = pos.astype(F32)[:, None] * inv[None, :]
    cos = jnp.tile(jnp.cos(ang), (1, 2 * MLA_HEADS))
    sin = jnp.tile(jnp.sin(ang), (1, 2 * MLA_HEADS))
    return cos, sin


def _rot_cols(w):
    half = ROPE_DIM // 2
    return jnp.concatenate([-w[..., half:], w[..., :half]], axis=-1)


def _layer_weights(w_in, g_mix, g_q, g_kv, w_uq, w_uk, w_uv, s5_D, s5_w_glu, s5_b_glu, g_out_attn, g_out_s5,
                   w_out, g_ffn, w_router, b_router, w_sh_gate, w_sh_up, w_sh_down):
    D = w_in.shape[0]
    o1, o2, o3 = Q_LORA, Q_LORA + KV_LORA, Q_LORA + KV_LORA + ROPE_DIM
    s5w = w_in.shape[1] - o3
    w_rope = w_in[:, o2:o3]
    w_in_ext = jnp.concatenate([w_in[:, :o2], w_in[:, o3:], w_rope, _rot_cols(w_rope)], axis=1).astype(BF16)
    wq = w_uq.reshape(Q_LORA, MLA_HEADS, NOPE_DIM + ROPE_DIM)
    wq_rope = wq[:, :, NOPE_DIM:]
    wr_hi, wr_lo = _split(w_router.T)
    return {
        "s5w": s5w,
        "g_mix": g_mix.reshape(1, D), "w_in": w_in_ext,
        "g_q": g_q.reshape(1, Q_LORA), "g_kv": g_kv.reshape(1, KV_LORA),
        "wq_nope": wq[:, :, :NOPE_DIM].reshape(Q_LORA, MLA_HEADS * NOPE_DIM).astype(BF16),
        "wq_rope": wq_rope.reshape(Q_LORA, MLA_HEADS * ROPE_DIM).astype(BF16),
        "wq_rot": _rot_cols(wq_rope).reshape(Q_LORA, MLA_HEADS * ROPE_DIM).astype(BF16),
        "w_ukT": jnp.transpose(w_uk, (1, 2, 0)).astype(BF16),
        "w_uv": jnp.transpose(w_uv, (1, 0, 2)).astype(BF16),
        "s5_d": s5_D.reshape(1, s5w), "w_glu": s5_w_glu.astype(BF16), "b_glu": s5_b_glu.reshape(1, s5w),
        "g_out_attn": g_out_attn.reshape(1, -1), "g_out_s5": g_out_s5.reshape(1, s5w),
        "w_out": w_out.astype(BF16), "g_ffn": g_ffn.reshape(1, D),
        "wr_hi": wr_hi, "wr_lo": wr_lo, "b_router": b_router.reshape(N_EXPERTS, 1),
        "w_sg": w_sh_gate.astype(BF16), "w_su": w_sh_up.astype(BF16), "w_sd": w_sh_down.astype(BF16),
    }


def _state_in(s_re, s_im):
    B = s_re.shape[0]
    return jnp.stack([s_re.reshape(B, -1), s_im.reshape(B, -1)], axis=1).astype(F32)


def kernel(x_prompt, x_sample, c_prompt, c_sample, cache_ckv, cache_krope, state_s5_re, state_s5_im, w_ada, b_ada, g_mix, g_ffn, w_in, g_q, w_uq, g_kv, w_uk, w_uv, s5_A_re, s5_A_im, s5_B_re, s5_B_im, s5_C_re, s5_C_im, s5_D, s5_log_dt, s5_w_glu, s5_b_glu, g_out_attn, g_out_s5, w_out, w_router, b_router, w_exp_gate, w_exp_up, w_exp_down, w_sh_gate, w_sh_up, w_sh_down, g_final):
    Bp, Lp, D = x_prompt.shape
    Bs, Ls, _ = x_sample.shape
    depth = w_ada.shape[0]
    assert depth == 1, "single-layer step"
    past = cache_ckv.shape[2]
    G, N = s5_A_re.shape[1:]
    Ts = Bs * Ls
    Tp = Bp * Lp
    l = 0

    wts = _layer_weights(w_in[l], g_mix[l], g_q[l], g_kv[l], w_uq[l], w_uk[l], w_uv[l], s5_D[l], s5_w_glu[l],
                         s5_b_glu[l], g_out_attn[l], g_out_s5[l], w_out[l], g_ffn[l], w_router[l], b_router[l],
                         w_sh_gate[l], w_sh_up[l], w_sh_down[l])

    mod = _ada(jnp.concatenate([c_prompt, c_sample], axis=0), w_ada[l], b_ada[l])
    mod_p = [m.reshape(Bp, 1, D) for m in jnp.split(mod[:Bp], 6, axis=-1)]
    mod_s = [jnp.broadcast_to(m[:, None, :], (Bs, Ls, D)).reshape(1, Ts, D)
             for m in jnp.split(mod[Bp:], 6, axis=-1)]

    tm_p = 256
    cos_p, sin_p = _rope_tables(jnp.arange(Lp))
    cos_s, sin_s = _rope_tables(jnp.tile(past + jnp.arange(Ls), Bs))

    ckv_p, kr_p, u_p, kcat_p, q_p = _pre(x_prompt, mod_p[0], mod_p[1], cos_p, sin_p, wts, tm_p)
    olat_p = _attn_prompt(q_p, kcat_p, tq=256, tk=512)
    tabs_p = _s5_tables(s5_A_re[l], s5_A_im[l], s5_B_re[l], s5_B_im[l], s5_C_re[l], s5_C_im[l], s5_log_dt[l], tm_p)
    os5_p, hl_p = _s5(u_p, jnp.zeros((Bp, 2, G * N), F32), tabs_p, wts, tm_p, precise=False)
    h2_p, base_p, eidx_p, gate_p = _post(x_prompt, olat_p, os5_p, [mod_p[2], mod_p[3], mod_p[4], mod_p[5]], wts, tm_p)

    xs_rows = x_sample.reshape(1, Ts, D)
    ckv_s, kr_s, u_s, kcat_s, q_s = _pre(xs_rows, mod_s[0], mod_s[1], cos_s, sin_s, wts, Ts)
    olat_s = _attn_sample(q_s, kcat_s.reshape(Bs, Ls, QK_DIM), cache_ckv[l], cache_krope[l])
    tabs_s = _s5_tables(s5_A_re[l], s5_A_im[l], s5_B_re[l], s5_B_im[l], s5_C_re[l], s5_C_im[l], s5_log_dt[l], Ls)
    os5_s, hl_s = _s5(u_s.reshape(Bs, Ls, -1), _state_in(state_s5_re[l], state_s5_im[l]), tabs_s, wts, Ls,
                      precise=True)
    h2_s, base_s, eidx_s, gate_s = _post(xs_rows, olat_s, os5_s.reshape(1, Ts, -1),
                                         [mod_s[2], mod_s[3], mod_s[4], mod_s[5]], wts, Ts)

    T_all = Tp + Ts
    eidx = jnp.concatenate([eidx_p, eidx_s], axis=1)
    gate = jnp.concatenate([gate_p, gate_s], axis=1)
    rank, counts = _rank(eidx, 256)
    counts = counts.reshape(N_EXPERTS).astype(I32)
    nblk = (counts + MOE_ROWS - 1) // MOE_ROWS
    blk_end = jnp.cumsum(nblk)
    pstart = (blk_end - nblk) * MOE_ROWS
    pos = pstart[eidx] + rank
    nb_max = (T_all * TOP_K) // MOE_ROWS + N_EXPERTS
    nused = blk_end[-1]
    blk_e = jnp.minimum(jnp.searchsorted(blk_end, jnp.arange(nb_max, dtype=I32), side="right"),
                        N_EXPERTS - 1).astype(I32)
    blk_e = jnp.where(jnp.arange(nb_max) < nused, blk_e, blk_e[jnp.maximum(nused - 1, 0)])
    src = jnp.zeros((nb_max * MOE_ROWS,), I32).at[pos.reshape(-1)].set(
        jnp.tile(jnp.arange(T_all, dtype=I32), TOP_K))
    h2_all = jnp.concatenate([h2_p.reshape(Tp, D), h2_s.reshape(Ts, D)], axis=0)
    xs_sorted = h2_all[src]
    ys = _experts(xs_sorted, blk_e, nused.reshape(1).astype(I32), w_exp_gate[l].astype(BF16),
                  w_exp_up[l].astype(BF16), w_exp_down[l].astype(BF16))
    ysg = ys[pos.T].reshape(T_all, TOP_K * D)
    gate_t = gate.T

    gfin = g_final.reshape(1, D)
    y_p = _final(base_p, ysg, gate_t, mod_p[5], gfin, tm_p, 0)
    y_s = _final(base_s, ysg, gate_t, mod_s[5], gfin, Ts, Tp).reshape(Bs, Ls, D)

    def state_out(hl, B):
        return hl[:, 0].reshape(1, B, G, N), hl[:, 1].reshape(1, B, G, N)

    sre_p, sim_p = state_out(hl_p, Bp)
    sre_s, sim_s = state_out(hl_s, Bs)
    return (y_p, y_s, ckv_p[None], kr_p[None], sre_p, sim_p,
            ckv_s.reshape(1, Bs, Ls, KV_LORA), kr_s.reshape(1, Bs, Ls, ROPE_DIM), sre_s, sim_s)
```

```python
import functools
import math

import numpy as np
import jax
import jax.numpy as jnp
from jax import lax
from jax.experimental import pallas as pl
from jax.experimental.pallas import tpu as pltpu
from jax.experimental.pallas import tpu_sc as plsc

F32 = jnp.float32
BF16 = jnp.bfloat16
I32 = jnp.int32
U32 = jnp.uint32

EPS = 1e-6
CHUNK = 64
MLA_HEADS = 4
NOPE_DIM = 128
ROPE_DIM = 64
V_DIM = 128
Q_LORA = 256
KV_LORA = 256
QK_DIM = KV_LORA + ROPE_DIM
ROPE_THETA = 10000.0
S5_GROUP_CH = 16
S5_STATE = 64
N_EXPERTS = 256
TOP_K = 8
N_EXPERT_GROUPS = 8
TOPK_GROUPS = 4
ROUTED_SCALE = 2.5

S5_SUB = 16
MOE_ROWS = 256
LANES = 128
VMEM_LIMIT = 56 * 1024 * 1024
SC_CORES = 2
SC_SUBCORES = 16
SC_WORKERS = SC_CORES * SC_SUBCORES
SC_WINDOW = 128

NEG_INF = float("-inf")
BIG_I32 = 1 << 30


def _dot(a, b):
    return jnp.dot(a, b, preferred_element_type=F32)


def _dot_nt(a, b):
    return lax.dot_general(a, b, (((1,), (1,)), ((), ())), preferred_element_type=F32)


def _split(a):
    hi = a.astype(BF16)
    lo = (a - hi.astype(F32)).astype(BF16)
    return hi, lo


def _lane_tile(x, n):
    return jnp.concatenate([x] * n, axis=1)


def _rms(x, g):
    return x * lax.rsqrt(jnp.mean(x * x, axis=-1, keepdims=True) + EPS) * g


def _silu(x):
    return x * jax.nn.sigmoid(x)


def _pack_rows(x):
    half = x.shape[1] // 2
    hi = lax.bitcast_convert_type(x[:, :half].astype(BF16).astype(F32), U32)
    lo = lax.bitcast_convert_type(x[:, half:].astype(BF16).astype(F32), U32)
    w = hi | (lo >> 16)
    return [w[:, j * LANES:(j + 1) * LANES] for j in range(half // LANES)]


def _unpack_rows(planes):
    his = [lax.bitcast_convert_type(p & jnp.uint32(0xFFFF0000), F32).astype(BF16) for p in planes]
    los = [lax.bitcast_convert_type(p << 16, F32).astype(BF16) for p in planes]
    return jnp.concatenate(his + los, axis=1)


def _params(sem):
    return pltpu.CompilerParams(dimension_semantics=sem, vmem_limit_bytes=VMEM_LIMIT)


def _const_spec(shape):
    nd = len(shape)
    return pl.BlockSpec(shape, lambda *_: (0,) * nd)


def _ada_kernel(c_ref, whi_ref, wlo_ref, b_ref, o_ref):
    c = c_ref[...]
    s_hi, s_lo = _split(_silu(c))
    w_hi = whi_ref[...]
    o_ref[...] = _dot(s_hi, w_hi) + _dot(s_hi, wlo_ref[...]) + _dot(s_lo, w_hi) + b_ref[...]


def _ada(c, w_ada, b_ada):
    rows, d = c.shape
    n = w_ada.shape[1]
    tn = 512
    w_hi, w_lo = _split(w_ada)
    return pl.pallas_call(
        _ada_kernel,
        grid=(n // tn,),
        in_specs=[_const_spec((rows, d)),
                  pl.BlockSpec((d, tn), lambda j: (0, j)),
                  pl.BlockSpec((d, tn), lambda j: (0, j)),
                  pl.BlockSpec((1, tn), lambda j: (0, j))],
        out_specs=pl.BlockSpec((rows, tn), lambda j: (0, j)),
        out_shape=jax.ShapeDtypeStruct((rows, n), F32),
        compiler_params=_params(("arbitrary",)),
        name="ada",
    )(c, w_hi, w_lo, b_ada.reshape(1, n))


def _pre_kernel(x_ref, sh_ref, sc_ref, g_ref, win_ref, gq_ref, gkv_ref, wqn_ref, wqr_ref, wqt_ref,
                wuk_ref, cos_ref, sin_ref, ckv_ref, kr_ref, u_ref, kcat_ref, q_ref, *, scale):
    x = x_ref[0]
    h = _rms(x, g_ref[...]) * (1.0 + sc_ref[0]) + sh_ref[0]
    z = _dot(h.astype(BF16), win_ref[...])
    cq = _rms(z[:, :Q_LORA], gq_ref[...])
    ckv = _rms(z[:, Q_LORA:Q_LORA + KV_LORA], gkv_ref[...])
    o_s5 = Q_LORA + KV_LORA
    s5w = u_ref.shape[-1]
    u_ref[0] = z[:, o_s5:o_s5 + s5w]
    o_r = o_s5 + s5w
    cos = cos_ref[...]
    sin = sin_ref[...]
    kr = z[:, o_r:o_r + ROPE_DIM] * cos[:, :ROPE_DIM] + z[:, o_r + ROPE_DIM:o_r + 2 * ROPE_DIM] * sin[:, :ROPE_DIM]
    ckv_ref[0] = ckv
    kr_ref[0] = kr
    kcat_ref[0, :, :KV_LORA] = ckv.astype(BF16)
    kcat_ref[0, :, KV_LORA:] = kr.astype(BF16)
    cqb = cq.astype(BF16)
    qn = _dot(cqb, wqn_ref[...])
    qr = (_dot(cqb, wqr_ref[...]) * cos + _dot(cqb, wqt_ref[...]) * sin) * scale
    for hd in range(MLA_HEADS):
        ql = _dot(qn[:, hd * NOPE_DIM:(hd + 1) * NOPE_DIM].astype(BF16), wuk_ref[hd]) * scale
        q_ref[0, hd, :, :KV_LORA] = ql.astype(BF16)
        q_ref[0, hd, :, KV_LORA:] = qr[:, hd * ROPE_DIM:(hd + 1) * ROPE_DIM].astype(BF16)


def _pre(x, shift, scale_mod, cos_t, sin_t, wts, tm):
    B, L, D = x.shape
    nt = L // tm
    per_row = shift.shape[1] != 1
    mod_spec = (pl.BlockSpec((1, tm, D), lambda b, i: (b, i, 0)) if per_row
                else pl.BlockSpec((1, 1, D), lambda b, i: (b, 0, 0)))
    s5w = wts["s5w"]
    hr = MLA_HEADS * ROPE_DIM
    kern = functools.partial(_pre_kernel, scale=(NOPE_DIM + ROPE_DIM) ** -0.5)
    consts = [wts["g_mix"], wts["w_in"], wts["g_q"], wts["g_kv"], wts["wq_nope"], wts["wq_rope"],
              wts["wq_rot"], wts["w_ukT"]]
    in_specs = [pl.BlockSpec((1, tm, D), lambda b, i: (b, i, 0)), mod_spec, mod_spec]
    in_specs += [_const_spec(c.shape) for c in consts]
    in_specs += [pl.BlockSpec((tm, hr), lambda b, i: (i, 0)), pl.BlockSpec((tm, hr), lambda b, i: (i, 0))]
    out_shape = (jax.ShapeDtypeStruct((B, L, KV_LORA), F32),
                 jax.ShapeDtypeStruct((B, L, ROPE_DIM), F32),
                 jax.ShapeDtypeStruct((B, L, s5w), F32),
                 jax.ShapeDtypeStruct((B, L, QK_DIM), BF16),
                 jax.ShapeDtypeStruct((B, MLA_HEADS, L, QK_DIM), BF16))
    out_specs = (pl.BlockSpec((1, tm, KV_LORA), lambda b, i: (b, i, 0)),
                 pl.BlockSpec((1, tm, ROPE_DIM), lambda b, i: (b, i, 0)),
                 pl.BlockSpec((1, tm, s5w), lambda b, i: (b, i, 0)),
                 pl.BlockSpec((1, tm, QK_DIM), lambda b, i: (b, i, 0)),
                 pl.BlockSpec((1, MLA_HEADS, tm, QK_DIM), lambda b, i: (b, 0, i, 0)))
    return pl.pallas_call(
        kern, grid=(B, nt), in_specs=in_specs, out_specs=out_specs, out_shape=out_shape,
        compiler_params=_params(("arbitrary", "arbitrary")), name="pre",
    )(x, shift, scale_mod, *consts, cos_t, sin_t)


def _attn_kernel(q_ref, k_ref, o_ref, m_scr, l_scr, acc_scr, *, tq, tk):
    i = pl.program_id(1)
    rows = MLA_HEADS * tq
    q = q_ref[0].reshape(rows, QK_DIM)
    m_scr[...] = jnp.full(m_scr.shape, NEG_INF, F32)
    l_scr[...] = jnp.zeros(l_scr.shape, F32)
    acc_scr[...] = jnp.zeros(acc_scr.shape, F32)

    def step(j0, masked):
        k = k_ref[0, pl.ds(j0, tk), :]
        s = _dot_nt(q, k)
        if masked:
            qpos = i * tq + lax.rem(lax.broadcasted_iota(I32, (rows, tk), 0), tq)
            kpos = j0 + lax.broadcasted_iota(I32, (rows, tk), 1)
            s = jnp.where(kpos // CHUNK <= qpos // CHUNK, s, NEG_INF)
        m_prev = m_scr[...]
        m_next = jnp.maximum(m_prev, jnp.max(s, axis=1, keepdims=True))
        alpha = jnp.exp(m_prev - m_next)
        p = jnp.exp(s - _lane_tile(m_next, tk // LANES))
        l_scr[...] = alpha * l_scr[...] + jnp.sum(p, axis=1, keepdims=True)
        m_scr[...] = m_next
        pv = _dot(p.astype(BF16), k[:, :KV_LORA])
        acc_scr[...] = acc_scr[...] * _lane_tile(alpha, KV_LORA // LANES) + pv

    n_full = (i * tq) // tk

    def body(j, carry):
        step(pl.multiple_of(j * tk, tk), False)
        return carry

    lax.fori_loop(0, n_full, body, 0)
    step(pl.multiple_of(n_full * tk, tk), True)
    inv = 1.0 / l_scr[...]
    o = acc_scr[...] * _lane_tile(inv, KV_LORA // LANES)
    o_ref[0] = o.astype(BF16).reshape(MLA_HEADS, tq, KV_LORA)


def _attn_prompt(q, kcat, tq, tk):
    B, H, L, _ = q.shape
    rows = H * tq
    kern = functools.partial(_attn_kernel, tq=tq, tk=tk)
    return pl.pallas_call(
        kern, grid=(B, L // tq),
        in_specs=[pl.BlockSpec((1, H, tq, QK_DIM), lambda b, i: (b, 0, i, 0)),
                  pl.BlockSpec((1, L, QK_DIM), lambda b, i: (b, 0, 0))],
        out_specs=pl.BlockSpec((1, H, tq, KV_LORA), lambda b, i: (b, 0, i, 0)),
        out_shape=jax.ShapeDtypeStruct((B, H, L, KV_LORA), BF16),
        scratch_shapes=[pltpu.VMEM((rows, LANES), F32), pltpu.VMEM((rows, LANES), F32),
                        pltpu.VMEM((rows, KV_LORA), F32)],
        compiler_params=_params(("arbitrary", "arbitrary")), name="attn_prompt",
    )(q, kcat)


def _attn_sample_kernel(q_ref, kn_ref, pc_ref, pr_ref, o_ref, *, past, lq):
    rows = MLA_HEADS * lq
    q = q_ref[0].reshape(rows, QK_DIM)
    pc = pc_ref[0].astype(BF16)
    pr = pr_ref[0].astype(BF16)
    kn = kn_ref[0]
    s_p = _dot_nt(q[:, :KV_LORA], pc) + _dot_nt(q[:, KV_LORA:], pr)
    s_n = _dot_nt(q, kn)
    qpos_p = past + lax.rem(lax.broadcasted_iota(I32, (rows, past), 0), lq)
    kpos_p = lax.broadcasted_iota(I32, (rows, past), 1)
    s_p = jnp.where(kpos_p // CHUNK <= qpos_p // CHUNK, s_p, NEG_INF)
    qpos_n = past + lax.rem(lax.broadcasted_iota(I32, (rows, lq), 0), lq)
    kpos_n = past + lax.broadcasted_iota(I32, (rows, lq), 1)
    s_n = jnp.where(kpos_n // CHUNK <= qpos_n // CHUNK, s_n, NEG_INF)
    m = jnp.maximum(jnp.max(s_p, axis=1, keepdims=True), jnp.max(s_n, axis=1, keepdims=True))
    p_p = jnp.exp(s_p - m)
    p_n = jnp.exp(s_n - m)
    l = jnp.sum(p_p, axis=1, keepdims=True) + jnp.sum(p_n, axis=1, keepdims=True)
    o = _dot(p_p.astype(BF16), pc) + _dot(p_n.astype(BF16), kn[:, :KV_LORA])
    o_ref[0] = (o / l).astype(BF16).reshape(MLA_HEADS, lq, KV_LORA)


def _attn_sample(q, kcat, past_ckv, past_kr):
    B, lq, _ = kcat.shape
    past = past_ckv.shape[1]
    H = MLA_HEADS
    kern = functools.partial(_attn_sample_kernel, past=past, lq=lq)
    return pl.pallas_call(
        kern, grid=(B,),
        in_specs=[pl.BlockSpec((1, H, lq, QK_DIM), lambda b: (0, 0, b, 0)),
                  pl.BlockSpec((1, lq, QK_DIM), lambda b: (b, 0, 0)),
                  pl.BlockSpec((1, past, KV_LORA), lambda b: (b, 0, 0)),
                  pl.BlockSpec((1, past, ROPE_DIM), lambda b: (b, 0, 0))],
        out_specs=pl.BlockSpec((1, H, lq, KV_LORA), lambda b: (0, 0, b, 0)),
        out_shape=jax.ShapeDtypeStruct((1, H, B * lq, KV_LORA), BF16),
        compiler_params=_params(("arbitrary",)), name="attn_sample",
    )(q, kcat, past_ckv, past_kr)


def _s5_kernel(u_ref, h0_ref, bre_ref, bim_ref, brel_ref, biml_ref, lt_ref, pinr_ref, pini_ref,
               pwr_ref, pwi_ref, a_ref, cre_ref, cim_ref, d_ref, wglu_ref, bglu_ref, gout_ref,
               o_ref, hl_ref, st_scr, cum_scr, hs_scr, *, tm, precise):
    i = pl.program_id(1)
    ns = st_scr.shape[1]
    half = ns // 2
    wh = u_ref.shape[-1] // 2

    @pl.when(i == 0)
    def _():
        st_scr[...] = h0_ref[0]

    u = u_ref[0]
    lt = lt_ref[...]
    for hf in range(2):
        uh = u[:, hf * wh:(hf + 1) * wh]
        if precise:
            u_hi, u_lo = _split(uh)
            bu_re = _dot(u_hi, bre_ref[hf]) + _dot(u_lo, bre_ref[hf]) + _dot(u_hi, brel_ref[hf])
            bu_im = _dot(u_hi, bim_ref[hf]) + _dot(u_lo, bim_ref[hf]) + _dot(u_hi, biml_ref[hf])
        else:
            u_hi = uh.astype(BF16)
            bu_re = _dot(u_hi, bre_ref[hf])
            bu_im = _dot(u_hi, bim_ref[hf])
        sl = slice(hf * half, (hf + 1) * half)
        pr = pinr_ref[:, sl]
        pi = pini_ref[:, sl]
        v_re = pr * bu_re - pi * bu_im
        v_im = pr * bu_im + pi * bu_re
        for part, v in ((0, v_re), (1, v_im)):
            if precise:
                v_hi, v_lo = _split(v)
                c = _dot(lt, v_hi) + _dot(lt, v_lo)
            else:
                c = _dot(lt, v.astype(BF16))
            cum_scr[:, part * ns + hf * half:part * ns + (hf + 1) * half] = c

    a_re = a_ref[0:1, :]
    a_im = a_ref[1:2, :]
    pw_re = pwr_ref[...]
    pw_im = pwi_ref[...]

    def chunk(c, carry):
        s_re, s_im = carry
        r0 = pl.multiple_of(c * S5_SUB, S5_SUB)
        t_re = cum_scr[pl.ds(r0, S5_SUB), 0:ns] + (a_re * s_re - a_im * s_im)
        t_im = cum_scr[pl.ds(r0, S5_SUB), ns:2 * ns] + (a_re * s_im + a_im * s_re)
        h_re = pw_re * t_re - pw_im * t_im
        h_im = pw_re * t_im + pw_im * t_re
        hs_scr[pl.ds(r0, S5_SUB), 0:ns] = h_re.astype(BF16)
        hs_scr[pl.ds(r0, S5_SUB), ns:2 * ns] = h_im.astype(BF16)
        return h_re[S5_SUB - 1:S5_SUB, :], h_im[S5_SUB - 1:S5_SUB, :]

    s_re, s_im = lax.fori_loop(0, tm // S5_SUB, chunk, (st_scr[0:1, :], st_scr[1:2, :]))
    st_scr[0:1, :] = s_re
    st_scr[1:2, :] = s_im
    hl_ref[0, 0:1, :] = s_re
    hl_ref[0, 1:2, :] = s_im

    ys = []
    for hf in range(2):
        hre = hs_scr[:, hf * half:(hf + 1) * half]
        him = hs_scr[:, ns + hf * half:ns + (hf + 1) * half]
        ys.append(_dot(hre, cre_ref[hf]) + _dot(him, cim_ref[hf]))
    y = jnp.concatenate(ys, axis=1) + d_ref[...] * u
    zg = jax.nn.gelu(y)
    gl = _dot(zg.astype(BF16), wglu_ref[...]) + bglu_ref[...]
    o = zg * jax.nn.sigmoid(gl)
    o_ref[0] = _rms(o, gout_ref[...]).astype(BF16)


def _s5(u, h0, tabs, wts, tm, precise):
    B, L, W = u.shape
    ns = h0.shape[-1]
    consts = [tabs["b_re"], tabs["b_im"], tabs["b_re_lo"], tabs["b_im_lo"], tabs["lt"], tabs["pin_re"],
              tabs["pin_im"], tabs["pw_re"], tabs["pw_im"], tabs["a"], tabs["c_re"], tabs["c_im"],
              wts["s5_d"], wts["w_glu"], wts["b_glu"], wts["g_out_s5"]]
    kern = functools.partial(_s5_kernel, tm=tm, precise=precise)
    return pl.pallas_call(
        kern, grid=(B, L // tm),
        in_specs=[pl.BlockSpec((1, tm, W), lambda b, i: (b, i, 0)),
                  pl.BlockSpec((1, 2, ns), lambda b, i: (b, 0, 0))] + [_const_spec(c.shape) for c in consts],
        out_specs=(pl.BlockSpec((1, tm, W), lambda b, i: (b, i, 0)),
                   pl.BlockSpec((1, 2, ns), lambda b, i: (b, 0, 0))),
        out_shape=(jax.ShapeDtypeStruct((B, L, W), BF16), jax.ShapeDtypeStruct((B, 2, ns), F32)),
        scratch_shapes=[pltpu.VMEM((2, ns), F32), pltpu.VMEM((tm, 2 * ns), F32), pltpu.VMEM((tm, 2 * ns), BF16)],
        compiler_params=_params(("arbitrary", "arbitrary")), name="s5",
    )(u, h0, *consts)


def _s5_tables(a_re_p, a_im_p, b_re_p, b_im_p, c_re_p, c_im_p, log_dt, tm):
    G, N = a_re_p.shape
    CH = b_re_p.shape[-1]
    dt = jnp.exp(log_dt.astype(F32))[:, None]
    lr = a_re_p.astype(F32) * dt
    li = a_im_p.astype(F32) * dt
    er = jnp.exp(lr)
    ab_re, ab_im = er * jnp.cos(li), er * jnp.sin(li)
    lam2 = a_re_p.astype(F32) ** 2 + a_im_p.astype(F32) ** 2
    nr, ni = ab_re - 1.0, ab_im
    f_re = (nr * a_re_p + ni * a_im_p) / lam2
    f_im = (ni * a_re_p - nr * a_im_p) / lam2
    bb_re = f_re[..., None] * b_re_p - f_im[..., None] * b_im_p
    bb_im = f_re[..., None] * b_im_p + f_im[..., None] * b_re_p
    gh = G // 2
    eye = jnp.eye(gh, dtype=F32)

    def blk_b(bb):
        t = bb.reshape(2, gh, N, CH)
        m = jnp.einsum("hgnc,gk->hgckn", t, eye)
        return m.reshape(2, gh * CH, gh * N)

    def blk_c(cc):
        t = cc.reshape(2, gh, CH, N)
        m = jnp.einsum("hgcn,gk->hgnkc", t, eye)
        return m.reshape(2, gh * N, gh * CH)

    b_re_m, b_im_m = blk_b(bb_re), blk_b(bb_im)
    b_re_hi, b_re_lo = _split(b_re_m)
    b_im_hi, b_im_lo = _split(b_im_m)
    s = jnp.arange(S5_SUB, dtype=F32)[:, None, None]

    def powers(sign):
        e = jnp.exp(sign * lr[None] * s)
        return ((e * jnp.cos(sign * li[None] * s)).reshape(S5_SUB, G * N),
                (e * jnp.sin(sign * li[None] * s)).reshape(S5_SUB, G * N))

    pin_re, pin_im = powers(-1.0)
    pw_re, pw_im = powers(1.0)
    reps = tm // S5_SUB
    r = np.arange(tm)
    lt = ((r[:, None] // S5_SUB == r[None, :] // S5_SUB) & (r[None, :] <= r[:, None])).astype(np.float32)
    return {
        "b_re": b_re_hi, "b_im": b_im_hi, "b_re_lo": b_re_lo, "b_im_lo": b_im_lo,
        "lt": jnp.asarray(lt, BF16),
        "pin_re": jnp.tile(pin_re, (reps, 1)), "pin_im": jnp.tile(pin_im, (reps, 1)),
        "pw_re": pw_re, "pw_im": pw_im,
        "a": jnp.stack([ab_re.reshape(G * N), ab_im.reshape(G * N)]),
        "c_re": blk_c(c_re_p.astype(F32)).astype(BF16), "c_im": (-blk_c(c_im_p.astype(F32))).astype(BF16),
    }


def _route(scores, sel):
    E, tm = scores.shape
    ge = E // N_EXPERT_GROUPS
    io_g = lax.broadcasted_iota(I32, (ge, tm), 0)
    gs_rows = []
    for g in range(N_EXPERT_GROUPS):
        sg = sel[g * ge:(g + 1) * ge, :]
        m1 = jnp.max(sg, axis=0, keepdims=True)
        i1 = jnp.min(jnp.where(sg == m1, io_g, BIG_I32), axis=0, keepdims=True)
        m2 = jnp.max(jnp.where(io_g == i1, NEG_INF, sg), axis=0, keepdims=True)
        gs_rows.append(m1 + m2)
    gs = jnp.concatenate(gs_rows, axis=0)
    gio = lax.broadcasted_iota(I32, gs.shape, 0)
    gsel = jnp.zeros(gs.shape, F32)
    for _ in range(TOPK_GROUPS):
        mx = jnp.max(gs, axis=0, keepdims=True)
        ix = jnp.min(jnp.where(gs == mx, gio, BIG_I32), axis=0, keepdims=True)
        hit = gio == ix
        gsel = jnp.where(hit, 1.0, gsel)
        gs = jnp.where(hit, NEG_INF, gs)
    emask = jnp.concatenate([jnp.broadcast_to(gsel[g:g + 1, :], (ge, tm)) for g in range(N_EXPERT_GROUPS)], axis=0)
    cand = jnp.where(emask > 0.0, sel, NEG_INF)
    eio = lax.broadcasted_iota(I32, (E, tm), 0)
    idxs, gates = [], []
    for _ in range(TOP_K):
        mx = jnp.max(cand, axis=0, keepdims=True)
        ix = jnp.min(jnp.where(cand == mx, eio, BIG_I32), axis=0, keepdims=True)
        hit = eio == ix
        gates.append(jnp.sum(jnp.where(hit, scores, 0.0), axis=0, keepdims=True))
        idxs.append(ix)
        cand = jnp.where(hit, NEG_INF, cand)
    eidx = jnp.concatenate(idxs, axis=0)
    gate = jnp.concatenate(gates, axis=0)
    gate = gate / jnp.sum(gate, axis=0, keepdims=True) * ROUTED_SCALE
    return eidx, gate


def _post_kernel(x_ref, ol_ref, os_ref, gtm_ref, shf_ref, scf_ref, gtf_ref, wuv_ref, goa_ref, wout_ref,
                 gffn_ref, wrh_ref, wrl_ref, br_ref, wsg_ref, wsu_ref, wsd_ref,
                 h2_ref, base_ref, eidx_ref, gate_ref):
    x = x_ref[0]
    oa = jnp.concatenate([_dot(ol_ref[0, hd], wuv_ref[hd]) for hd in range(MLA_HEADS)], axis=1)
    oan = _rms(oa, goa_ref[...]).astype(BF16)
    wa = oan.shape[1]
    mix = _dot(oan, wout_ref[:wa, :]) + _dot(os_ref[0], wout_ref[wa:, :])
    x1 = x + gtm_ref[0] * mix
    h2 = _rms(x1, gffn_ref[...]) * (1.0 + scf_ref[0]) + shf_ref[0]
    h2_hi, h2_lo = _split(h2)
    for j, plane in enumerate(_pack_rows(h2)):
        h2_ref[j] = plane
    sh =_dot((_silu(_dot(h2_hi, wsg_ref[...])) * _dot(h2_hi, wsu_ref[...])).astype(BF16), wsd_ref[...])
    base_ref[0] = x1 + gtf_ref[0] * sh
    wr_hi = wrh_ref[...]
    logits = _dot_nt(wr_hi, h2_hi) + _dot_nt(wrl_ref[...], h2_hi) + _dot_nt(wr_hi, h2_lo)
    scores = jax.nn.sigmoid(logits)
    eidx, gate = _route(scores, scores + br_ref[...])
    eidx_ref[...] = eidx
    gate_ref[...] = gate


def _post(x, o_lat, o_s5, mods, wts, tm):
    B, L, D = x.shape
    nt = L // tm
    per_row = mods[0].shape[1] != 1
    mod_spec = (pl.BlockSpec((1, tm, D), lambda b, i: (b, i, 0)) if per_row
                else pl.BlockSpec((1, 1, D), lambda b, i: (b, 0, 0)))
    W = o_s5.shape[-1]
    consts = [wts["w_uv"], wts["g_out_attn"], wts["w_out"], wts["g_ffn"], wts["wr_hi"], wts["wr_lo"],
              wts["b_router"], wts["w_sg"], wts["w_su"], wts["w_sd"]]
    in_specs = [pl.BlockSpec((1, tm, D), lambda b, i: (b, i, 0)),
                pl.BlockSpec((1, MLA_HEADS, tm, KV_LORA), lambda b, i: (b, 0, i, 0)),
                pl.BlockSpec((1, tm, W), lambda b, i: (b, i, 0)),
                mod_spec, mod_spec, mod_spec, mod_spec] + [_const_spec(c.shape) for c in consts]
    planes = D // (2 * LANES)
    out_shape = (jax.ShapeDtypeStruct((planes, B * L, LANES), U32), jax.ShapeDtypeStruct((B, L, D), F32),
                 jax.ShapeDtypeStruct((TOP_K, B * L), I32), jax.ShapeDtypeStruct((TOP_K, B * L), F32))
    out_specs = (pl.BlockSpec((planes, tm, LANES), lambda b, i: (0, b * nt + i, 0)),
                 pl.BlockSpec((1, tm, D), lambda b, i: (b, i, 0)),
                 pl.BlockSpec((TOP_K, tm), lambda b, i: (0, b * nt + i)),
                 pl.BlockSpec((TOP_K, tm), lambda b, i: (0, b * nt + i)))
    return pl.pallas_call(
        _post_kernel, grid=(B, nt), in_specs=in_specs, out_specs=out_specs, out_shape=out_shape,
        compiler_params=_params(("arbitrary", "arbitrary")), name="post",
    )(x, o_lat, o_s5, *mods, *consts)


def _rank_kernel(eidx_ref, tri_ref, rank_ref, cnt_ref, carry_scr):
    i = pl.program_id(0)

    @pl.when(i == 0)
    def _():
        carry_scr[...] = jnp.zeros(carry_scr.shape, F32)

    eidx = eidx_ref[...]
    tt = eidx.shape[1]
    eio = lax.broadcasted_iota(I32, (N_EXPERTS, tt), 0)
    hits = [eio == eidx[k:k + 1, :] for k in range(TOP_K)]
    onehot = jnp.zeros((N_EXPERTS, tt), F32)
    for hit in hits:
        onehot = jnp.where(hit, 1.0, onehot)
    before = _dot(onehot.astype(BF16), tri_ref[...]) + carry_scr[...]
    ranks = [jnp.sum(jnp.where(hit, before, 0.0), axis=0, keepdims=True) for hit in hits]
    rank_ref[...] = jnp.concatenate(ranks, axis=0).astype(I32)
    carry = carry_scr[...] + jnp.sum(onehot, axis=1, keepdims=True)
    carry_scr[...] = carry
    cnt_ref[...] = carry


def _rank(eidx, tt):
    K, T = eidx.shape
    r = np.arange(tt)
    tri = jnp.asarray((r[:, None] < r[None, :]).astype(np.float32), BF16)
    return pl.pallas_call(
        _rank_kernel, grid=(T // tt,),
        in_specs=[pl.BlockSpec((K, tt), lambda i: (0, i)), _const_spec((tt, tt))],
        out_specs=(pl.BlockSpec((K, tt), lambda i: (0, i)), _const_spec((N_EXPERTS, 1))),
        out_shape=(jax.ShapeDtypeStruct((K, T), I32), jax.ShapeDtypeStruct((N_EXPERTS, 1), F32)),
        scratch_shapes=[pltpu.VMEM((N_EXPERTS, 1), F32)],
        compiler_params=_params(("arbitrary",)), name="rank",
    )(eidx, tri)


def _expert_kernel(blk_e_ref, nvalid_ref, nused_ref, x_ref, wg_ref, wu_ref, wd_ref, o_ref):
    b = pl.program_id(0)

    @pl.when(b < nused_ref[0])
    def _():
        planes = x_ref.shape[0]
        x = _unpack_rows([x_ref[j] for j in range(planes)])
        live = lax.broadcasted_iota(I32, x.shape, 0) < nvalid_ref[b]
        x = jnp.where(live, x, jnp.zeros_like(x))
        g = _dot(x, wg_ref[0].astype(BF16))
        u = _dot(x, wu_ref[0].astype(BF16))
        y = _dot((_silu(g) * u).astype(BF16), wd_ref[0].astype(BF16))
        for j, plane in enumerate(_pack_rows(y)):
            o_ref[j] = plane


def _experts(xs, blk_e, nvalid, nused, wg, wu, wd):
    planes, rows, _ = xs.shape
    nb = rows // MOE_ROWS
    _, D, F = wg.shape

    def xmap(b, blk_e_ref, nvalid_ref, nused_ref):
        return (0, jnp.minimum(b, nused_ref[0] - 1), 0)

    def wmap(b, blk_e_ref, nvalid_ref, nused_ref):
        return (blk_e_ref[b], 0, 0)

    grid_spec = pltpu.PrefetchScalarGridSpec(
        num_scalar_prefetch=3, grid=(nb,),
        in_specs=[pl.BlockSpec((planes, MOE_ROWS, LANES), xmap),
                  pl.BlockSpec((1, D, F), wmap), pl.BlockSpec((1, D, F), wmap), pl.BlockSpec((1, F, D), wmap)],
        out_specs=pl.BlockSpec((planes, MOE_ROWS, LANES), xmap))
    return pl.pallas_call(
        _expert_kernel, grid_spec=grid_spec, out_shape=jax.ShapeDtypeStruct(xs.shape, U32),
        compiler_params=_params(("arbitrary",)), name="experts",
    )(blk_e, nvalid, nused, xs, wg, wu, wd)


def _final_kernel(base_ref, ys_ref, gate_ref, gtf_ref, gfin_ref, o_ref):
    planes = ys_ref.shape[0]
    gate = gate_ref[...]
    routed = jnp.zeros(base_ref.shape[1:], F32)
    for k in range(TOP_K):
        rows = _unpack_rows([ys_ref[j, k] for j in range(planes)])
        routed = routed + gate[:, k:k + 1] * rows.astype(F32)
    y = base_ref[0] + gtf_ref[0] * routed
    o_ref[0] = _rms(y, gfin_ref[...])


def _final(base, ysg, gate_t, gt_f, g_final, tm, row0):
    B, L, D = base.shape
    planes = ysg.shape[0]
    nt = L // tm
    off = row0 // tm
    per_row = gt_f.shape[1] != 1
    mod_spec = (pl.BlockSpec((1, tm, D), lambda b, i: (b, i, 0)) if per_row
                else pl.BlockSpec((1, 1, D), lambda b, i: (b, 0, 0)))
    return pl.pallas_call(
        _final_kernel, grid=(B, nt),
        in_specs=[pl.BlockSpec((1, tm, D), lambda b, i: (b, i, 0)),
                  pl.BlockSpec((planes, TOP_K, tm, LANES), lambda b, i: (0, 0, off + b * nt + i, 0)),
                  pl.BlockSpec((tm, TOP_K), lambda b, i: (off + b * nt + i, 0)),
                  mod_spec, _const_spec((1, D))],
        out_specs=pl.BlockSpec((1, tm, D), lambda b, i: (b, i, 0)),
        out_shape=jax.ShapeDtypeStruct((B, L, D), F32),
        compiler_params=_params(("arbitrary", "arbitrary")), name="final",
    )(base, ysg, gate_t, gt_f, g_final)


def _sc_mesh():
    return plsc.VectorSubcoreMesh(core_axis_name="c", subcore_axis_name="s")


def _sc_worker():
    return lax.axis_index("s") * SC_CORES + lax.axis_index("c")


def _sc_dispatch(src, idx, out_rows, planes, n_tok):
    n_win = n_tok // SC_WINDOW
    n_iter = -(-n_win // SC_WORKERS)

    @functools.partial(
        pl.kernel, mesh=_sc_mesh(), out_type=jax.ShapeDtypeStruct((out_rows, LANES), src.dtype),
        scratch_types=[pltpu.VMEM((SC_WINDOW,), I32), pltpu.VMEM((SC_WINDOW, LANES), src.dtype),
                       pltpu.SemaphoreType.DMA],
        name="sc_dispatch")
    def k(src_hbm, idx_hbm, out_hbm, idx_v, rows_v, sem):
        wid = _sc_worker()

        @pl.loop(0, n_iter)
        def _(i):
            win = i * SC_WORKERS + wid

            @pl.when(win < n_win)
            def _():
                t0 = win * SC_WINDOW
                for j in range(planes):
                    pltpu.sync_copy(src_hbm.at[pl.ds(pl.multiple_of(j * n_tok + t0, SC_WINDOW), SC_WINDOW)], rows_v)
                    for kk in range(TOP_K):
                        off = pl.multiple_of((j * TOP_K + kk) * n_tok + t0, SC_WINDOW)
                        pltpu.sync_copy(idx_hbm.at[pl.ds(off, SC_WINDOW)], idx_v)
                        pltpu.async_copy(rows_v, out_hbm.at[idx_v], sem).wait()

    return k(src, idx)


def _sc_gather(table, idx):
    n_idx = idx.shape[0]
    n_iter = n_idx // (SC_WINDOW * SC_WORKERS)

    @functools.partial(
        pl.kernel, mesh=_sc_mesh(), out_type=jax.ShapeDtypeStruct((n_idx, LANES), table.dtype),
        scratch_types=[pltpu.VMEM((SC_WINDOW,), I32), pltpu.VMEM((SC_WINDOW, LANES), table.dtype),
                       pltpu.SemaphoreType.DMA],
        name="sc_gather")
    def k(table_hbm, idx_hbm, out_hbm, idx_v, rows_v, sem):
        wid = _sc_worker()

        @pl.loop(0, n_iter)
        def _(i):
            base = pl.multiple_of((i * SC_WORKERS + wid) * SC_WINDOW, SC_WINDOW)
            pltpu.sync_copy(idx_hbm.at[pl.ds(base, SC_WINDOW)], idx_v)
            pltpu.async_copy(table_hbm.at[idx_v], rows_v, sem).wait()
            pltpu.sync_copy(rows_v, out_hbm.at[pl.ds(base, SC_WINDOW)])

    return k(table, idx)


def _rope_tables(pos):
    half = ROPE_DIM // 2
    inv = ROPE_THETA ** (-jnp.arange(half, dtype=F32) / half)
    ang = pos.astype(F32)[:, None] * inv[None, :]
    cos = jnp.tile(jnp.cos(ang), (1, 2 * MLA_HEADS))
    sin = jnp.tile(jnp.sin(ang), (1, 2 * MLA_HEADS))
    return cos, sin


def _rot_cols(w):
    half = ROPE_DIM // 2
    return jnp.concatenate([-w[..., half:], w[..., :half]], axis=-1)


def _layer_weights(w_in, g_mix, g_q, g_kv, w_uq, w_uk, w_uv, s5_D, s5_w_glu, s5_b_glu, g_out_attn, g_out_s5,
                   w_out, g_ffn, w_router, b_router, w_sh_gate, w_sh_up, w_sh_down):
    D = w_in.shape[0]
    o1, o2, o3 = Q_LORA, Q_LORA + KV_LORA, Q_LORA + KV_LORA + ROPE_DIM
    s5w = w_in.shape[1] - o3
    w_rope = w_in[:, o2:o3]
    w_in_ext = jnp.concatenate([w_in[:, :o2], w_in[:, o3:], w_rope, _rot_cols(w_rope)], axis=1).astype(BF16)
    wq = w_uq.reshape(Q_LORA, MLA_HEADS, NOPE_DIM + ROPE_DIM)
    wq_rope = wq[:, :, NOPE_DIM:]
    wr_hi, wr_lo = _split(w_router.T)
    return {
        "s5w": s5w,
        "g_mix": g_mix.reshape(1, D), "w_in": w_in_ext,
        "g_q": g_q.reshape(1, Q_LORA), "g_kv": g_kv.reshape(1, KV_LORA),
        "wq_nope": wq[:, :, :NOPE_DIM].reshape(Q_LORA, MLA_HEADS * NOPE_DIM).astype(BF16),
        "wq_rope": wq_rope.reshape(Q_LORA, MLA_HEADS * ROPE_DIM).astype(BF16),
        "wq_rot": _rot_cols(wq_rope).reshape(Q_LORA, MLA_HEADS * ROPE_DIM).astype(BF16),
        "w_ukT": jnp.transpose(w_uk, (1, 2, 0)).astype(BF16),
        "w_uv": jnp.transpose(w_uv, (1, 0, 2)).astype(BF16),
        "s5_d": s5_D.reshape(1, s5w), "w_glu": s5_w_glu.astype(BF16), "b_glu": s5_b_glu.reshape(1, s5w),
        "g_out_attn": g_out_attn.reshape(1, -1), "g_out_s5": g_out_s5.reshape(1, s5w),
        "w_out": w_out.astype(BF16), "g_ffn": g_ffn.reshape(1, D),
        "wr_hi": wr_hi, "wr_lo": wr_lo, "b_router": b_router.reshape(N_EXPERTS, 1),
        "w_sg": w_sh_gate.astype(BF16), "w_su": w_sh_up.astype(BF16), "w_sd": w_sh_down.astype(BF16),
    }


def _state_in(s_re, s_im):
    B = s_re.shape[0]
    return jnp.stack([s_re.reshape(B, -1), s_im.reshape(B, -1)], axis=1).astype(F32)


def kernel(x_prompt, x_sample, c_prompt, c_sample, cache_ckv, cache_krope, state_s5_re, state_s5_im, w_ada, b_ada, g_mix, g_ffn, w_in, g_q, w_uq, g_kv, w_uk, w_uv, s5_A_re, s5_A_im, s5_B_re, s5_B_im, s5_C_re, s5_C_im, s5_D, s5_log_dt, s5_w_glu, s5_b_glu, g_out_attn, g_out_s5, w_out, w_router, b_router, w_exp_gate, w_exp_up, w_exp_down, w_sh_gate, w_sh_up, w_sh_down, g_final):
    Bp, Lp, D = x_prompt.shape
    Bs, Ls, _ = x_sample.shape
    depth = w_ada.shape[0]
    assert depth == 1, "single-layer step"
    past = cache_ckv.shape[2]
    G, N = s5_A_re.shape[1:]
    Ts = Bs * Ls
    Tp = Bp * Lp
    l = 0

    wts = _layer_weights(w_in[l], g_mix[l], g_q[l], g_kv[l], w_uq[l], w_uk[l], w_uv[l], s5_D[l], s5_w_glu[l],
                         s5_b_glu[l], g_out_attn[l], g_out_s5[l], w_out[l], g_ffn[l], w_router[l], b_router[l],
                         w_sh_gate[l], w_sh_up[l], w_sh_down[l])

    mod = _ada(jnp.concatenate([c_prompt, c_sample], axis=0), w_ada[l], b_ada[l])
    mod_p = [m.reshape(Bp, 1, D) for m in jnp.split(mod[:Bp], 6, axis=-1)]
    mod_s = [jnp.broadcast_to(m[:, None, :], (Bs, Ls, D)).reshape(1, Ts, D)
             for m in jnp.split(mod[Bp:], 6, axis=-1)]

    tm_p = 256
    cos_p, sin_p = _rope_tables(jnp.arange(Lp))
    cos_s, sin_s = _rope_tables(jnp.tile(past + jnp.arange(Ls), Bs))

    ckv_p, kr_p, u_p, kcat_p, q_p = _pre(x_prompt, mod_p[0], mod_p[1], cos_p, sin_p, wts, tm_p)
    olat_p = _attn_prompt(q_p, kcat_p, tq=256, tk=512)
    tabs_p = _s5_tables(s5_A_re[l], s5_A_im[l], s5_B_re[l], s5_B_im[l], s5_C_re[l], s5_C_im[l], s5_log_dt[l], tm_p)
    os5_p, hl_p = _s5(u_p, jnp.zeros((Bp, 2, G * N), F32), tabs_p, wts, tm_p, precise=False)
    h2_p, base_p, eidx_p, gate_p = _post(x_prompt, olat_p, os5_p, [mod_p[2], mod_p[3], mod_p[4], mod_p[5]], wts, tm_p)

    xs_rows = x_sample.reshape(1, Ts, D)
    ckv_s, kr_s, u_s, kcat_s, q_s = _pre(xs_rows, mod_s[0], mod_s[1], cos_s, sin_s, wts, Ts)
    olat_s = _attn_sample(q_s, kcat_s.reshape(Bs, Ls, QK_DIM), cache_ckv[l], cache_krope[l])
    tabs_s = _s5_tables(s5_A_re[l], s5_A_im[l], s5_B_re[l], s5_B_im[l], s5_C_re[l], s5_C_im[l], s5_log_dt[l], Ls)
    os5_s, hl_s = _s5(u_s.reshape(Bs, Ls, -1), _state_in(state_s5_re[l], state_s5_im[l]), tabs_s, wts, Ls,
                      precise=True)
    h2_s, base_s, eidx_s, gate_s = _post(xs_rows, olat_s, os5_s.reshape(1, Ts, -1),
                                         [mod_s[2], mod_s[3], mod_s[4], mod_s[5]], wts, Ts)

    T_all = Tp + Ts
    eidx = jnp.concatenate([eidx_p, eidx_s], axis=1)
    gate = jnp.concatenate([gate_p, gate_s], axis=1)
    rank, counts = _rank(eidx, 256)
    counts = counts.reshape(N_EXPERTS).astype(I32)
    nblk = (counts + MOE_ROWS - 1) // MOE_ROWS
    blk_end = jnp.cumsum(nblk)
    pstart = (blk_end - nblk) * MOE_ROWS
    pos = pstart[eidx] + rank
    nb_max = (T_all * TOP_K) // MOE_ROWS + N_EXPERTS
    nused = blk_end[-1]
    blk_e = jnp.minimum(jnp.searchsorted(blk_end, jnp.arange(nb_max, dtype=I32), side="right"),
                        N_EXPERTS - 1).astype(I32)
    bidx = jnp.arange(nb_max, dtype=I32)
    blk_e = jnp.where(bidx < nused, blk_e, blk_e[jnp.maximum(nused - 1, 0)])
    nvalid = jnp.clip(counts[blk_e] - (bidx - (blk_end - nblk)[blk_e]) * MOE_ROWS, 0, MOE_ROWS).astype(I32)
    rows_pad = nb_max * MOE_ROWS
    planes = h2_p.shape[0]
    idx = (jnp.arange(planes, dtype=I32)[:, None, None] * rows_pad + pos[None]).reshape(-1)
    h2_all = jnp.concatenate([h2_p, h2_s], axis=1).reshape(planes * T_all, LANES)
    xs_sorted = _sc_dispatch(h2_all, idx, planes * rows_pad, planes, T_all).reshape(planes, rows_pad, LANES)
    ys = _experts(xs_sorted, blk_e, nvalid, nused.reshape(1).astype(I32), w_exp_gate[l], w_exp_up[l], w_exp_down[l])
    ysg = _sc_gather(ys.reshape(planes * rows_pad, LANES), idx).reshape(planes, TOP_K, T_all, LANES)
    gate_t = gate.T

    gfin = g_final.reshape(1, D)
    y_p = _final(base_p, ysg, gate_t, mod_p[5], gfin, tm_p, 0)
    y_s = _final(base_s, ysg, gate_t, mod_s[5], gfin, Ts, Tp).reshape(Bs, Ls, D)

    def state_out(hl, B):
        return hl[:, 0].reshape(1, B, G, N), hl[:, 1].reshape(1, B, G, N)

    sre_p, sim_p = state_out(hl_p, Bp)
    sre_s, sim_s = state_out(hl_s, Bs)
    return (y_p, y_s, ckv_p[None], kr_p[None], sre_p, sim_p,
            ckv_s.reshape(1, Bs, Ls, KV_LORA), kr_s.reshape(1, Bs, Ls, ROPE_DIM), sre_s, sim_s)
```

```python
import functools
import math

import numpy as np
import jax
import jax.numpy as jnp
from jax import lax
from jax.experimental import pallas as pl
from jax.experimental.pallas import tpu as pltpu
from jax.experimental.pallas import tpu_sc as plsc

F32 = jnp.float32
BF16 = jnp.bfloat16
I32 = jnp.int32
U32 = jnp.uint32

EPS = 1e-6
CHUNK = 64
MLA_HEADS = 4
NOPE_DIM = 128
ROPE_DIM = 64
V_DIM = 128
Q_LORA = 256
KV_LORA = 256
QK_DIM = KV_LORA + ROPE_DIM
ROPE_THETA = 10000.0
S5_GROUP_CH = 16
S5_STATE = 64
N_EXPERTS = 256
TOP_K = 8
N_EXPERT_GROUPS = 8
TOPK_GROUPS = 4
ROUTED_SCALE = 2.5

S5_SUB = 16
MOE_ROWS = 256
LANES = 128
VMEM_LIMIT = 56 * 1024 * 1024
SC_CORES = 2
SC_SUBCORES = 16
SC_WORKERS = SC_CORES * SC_SUBCORES
SC_WINDOW = 128

NEG_INF = float("-inf")
BIG_I32 = 1 << 30


def _dot(a, b):
    return jnp.dot(a, b, preferred_element_type=F32)


def _dot_nt(a, b):
    return lax.dot_general(a, b, (((1,), (1,)), ((), ())), preferred_element_type=F32)


def _split(a):
    hi = a.astype(BF16)
    lo = (a - hi.astype(F32)).astype(BF16)
    return hi, lo


def _lane_tile(x, n):
    return jnp.concatenate([x] * n, axis=1)


def _rms(x, g):
    return x * lax.rsqrt(jnp.mean(x * x, axis=-1, keepdims=True) + EPS) * g


def _silu(x):
    return x * jax.nn.sigmoid(x)


def _pack_rows(x):
    half = x.shape[1] // 2
    hi = lax.bitcast_convert_type(x[:, :half].astype(BF16).astype(F32), U32)
    lo = lax.bitcast_convert_type(x[:, half:].astype(BF16).astype(F32), U32)
    w = hi | (lo >> 16)
    return [w[:, j * LANES:(j + 1) * LANES] for j in range(half // LANES)]


def _unpack_rows(planes):
    his = [lax.bitcast_convert_type(p & jnp.uint32(0xFFFF0000), F32).astype(BF16) for p in planes]
    los = [lax.bitcast_convert_type(p << 16, F32).astype(BF16) for p in planes]
    return jnp.concatenate(his + los, axis=1)


def _params(sem):
    return pltpu.CompilerParams(dimension_semantics=sem, vmem_limit_bytes=VMEM_LIMIT)


def _const_spec(shape):
    nd = len(shape)
    return pl.BlockSpec(shape, lambda *_: (0,) * nd)


def _ada_kernel(c_ref, whi_ref, wlo_ref, b_ref, o_ref):
    c = c_ref[...]
    s_hi, s_lo = _split(_silu(c))
    w_hi = whi_ref[...]
    o_ref[...] = _dot(s_hi, w_hi) + _dot(s_hi, wlo_ref[...]) + _dot(s_lo, w_hi) + b_ref[...]


def _ada(c, w_ada, b_ada):
    rows, d = c.shape
    n = w_ada.shape[1]
    tn = 512
    w_hi, w_lo = _split(w_ada)
    return pl.pallas_call(
        _ada_kernel,
        grid=(n // tn,),
        in_specs=[_const_spec((rows, d)),
                  pl.BlockSpec((d, tn), lambda j: (0, j)),
                  pl.BlockSpec((d, tn), lambda j: (0, j)),
                  pl.BlockSpec((1, tn), lambda j: (0, j))],
        out_specs=pl.BlockSpec((rows, tn), lambda j: (0, j)),
        out_shape=jax.ShapeDtypeStruct((rows, n), F32),
        compiler_params=_params(("arbitrary",)),
        name="ada",
    )(c, w_hi, w_lo, b_ada.reshape(1, n))


def _pre_kernel(x_ref, sh_ref, sc_ref, g_ref, win_ref, gq_ref, gkv_ref, wqn_ref, wqr_ref, wqt_ref,
                wuk_ref, cos_ref, sin_ref, ckv_ref, kr_ref, u_ref, kcat_ref, q_ref, *, scale):
    x = x_ref[0]
    h = _rms(x, g_ref[...]) * (1.0 + sc_ref[0]) + sh_ref[0]
    z = _dot(h.astype(BF16), win_ref[...])
    cq = _rms(z[:, :Q_LORA], gq_ref[...])
    ckv = _rms(z[:, Q_LORA:Q_LORA + KV_LORA], gkv_ref[...])
    o_s5 = Q_LORA + KV_LORA
    s5w = u_ref.shape[-1]
    u_ref[0] = z[:, o_s5:o_s5 + s5w]
    o_r = o_s5 + s5w
    cos = cos_ref[...]
    sin = sin_ref[...]
    kr = z[:, o_r:o_r + ROPE_DIM] * cos[:, :ROPE_DIM] + z[:, o_r + ROPE_DIM:o_r + 2 * ROPE_DIM] * sin[:, :ROPE_DIM]
    ckv_ref[0] = ckv
    kr_ref[0] = kr
    kcat_ref[0, :, :KV_LORA] = ckv.astype(BF16)
    kcat_ref[0, :, KV_LORA:] = kr.astype(BF16)
    cqb = cq.astype(BF16)
    qn = _dot(cqb, wqn_ref[...])
    qr = (_dot(cqb, wqr_ref[...]) * cos + _dot(cqb, wqt_ref[...]) * sin) * scale
    for hd in range(MLA_HEADS):
        ql = _dot(qn[:, hd * NOPE_DIM:(hd + 1) * NOPE_DIM].astype(BF16), wuk_ref[hd]) * scale
        q_ref[0, hd, :, :KV_LORA] = ql.astype(BF16)
        q_ref[0, hd, :, KV_LORA:] = qr[:, hd * ROPE_DIM:(hd + 1) * ROPE_DIM].astype(BF16)


def _pre(x, shift, scale_mod, cos_t, sin_t, wts, tm):
    B, L, D = x.shape
    nt = L // tm
    per_row = shift.shape[1] != 1
    mod_spec = (pl.BlockSpec((1, tm, D), lambda b, i: (b, i, 0)) if per_row
                else pl.BlockSpec((1, 1, D), lambda b, i: (b, 0, 0)))
    s5w = wts["s5w"]
    hr = MLA_HEADS * ROPE_DIM
    kern = functools.partial(_pre_kernel, scale=(NOPE_DIM + ROPE_DIM) ** -0.5 * math.log2(math.e))
    consts = [wts["g_mix"], wts["w_in"], wts["g_q"], wts["g_kv"], wts["wq_nope"], wts["wq_rope"],
              wts["wq_rot"], wts["w_ukT"]]
    in_specs = [pl.BlockSpec((1, tm, D), lambda b, i: (b, i, 0)), mod_spec, mod_spec]
    in_specs += [_const_spec(c.shape) for c in consts]
    in_specs += [pl.BlockSpec((tm, hr), lambda b, i: (i, 0)), pl.BlockSpec((tm, hr), lambda b, i: (i, 0))]
    out_shape = (jax.ShapeDtypeStruct((B, L, KV_LORA), F32),
                 jax.ShapeDtypeStruct((B, L, ROPE_DIM), F32),
                 jax.ShapeDtypeStruct((B, L, s5w), F32),
                 jax.ShapeDtypeStruct((B, L, QK_DIM), BF16),
                 jax.ShapeDtypeStruct((B, MLA_HEADS, L, QK_DIM), BF16))
    out_specs = (pl.BlockSpec((1, tm, KV_LORA), lambda b, i: (b, i, 0)),
                 pl.BlockSpec((1, tm, ROPE_DIM), lambda b, i: (b, i, 0)),
                 pl.BlockSpec((1, tm, s5w), lambda b, i: (b, i, 0)),
                 pl.BlockSpec((1, tm, QK_DIM), lambda b, i: (b, i, 0)),
                 pl.BlockSpec((1, MLA_HEADS, tm, QK_DIM), lambda b, i: (b, 0, i, 0)))
    return pl.pallas_call(
        kern, grid=(B, nt), in_specs=in_specs, out_specs=out_specs, out_shape=out_shape,
        compiler_params=_params(("arbitrary", "arbitrary")), name="pre",
    )(x, shift, scale_mod, *consts, cos_t, sin_t)


def _attn_kernel(q_ref, k_ref, o_ref, m_scr, l_scr, acc_scr, *, tq, tk):
    i = pl.program_id(1)
    rows = MLA_HEADS * tq
    q = q_ref[0].reshape(rows, QK_DIM)
    m_scr[...] = jnp.full(m_scr.shape, NEG_INF, F32)
    l_scr[...] = jnp.zeros(l_scr.shape, F32)
    acc_scr[...] = jnp.zeros(acc_scr.shape, F32)

    def step(j0, masked):
        k = k_ref[0, pl.ds(j0, tk), :]
        s = _dot_nt(q, k)
        if masked:
            qpos = i * tq + lax.rem(lax.broadcasted_iota(I32, (rows, tk), 0), tq)
            kpos = j0 + lax.broadcasted_iota(I32, (rows, tk), 1)
            s = jnp.where(kpos // CHUNK <= qpos // CHUNK, s, NEG_INF)
        m_prev = m_scr[...]
        m_next = jnp.maximum(m_prev, jnp.max(s, axis=1, keepdims=True))
        alpha = jnp.exp2(m_prev - m_next)
        p = jnp.exp2(s - _lane_tile(m_next, tk // LANES))
        l_scr[...] = alpha * l_scr[...] + jnp.sum(p, axis=1, keepdims=True)
        m_scr[...] = m_next
        pv = _dot(p.astype(BF16), k[:, :KV_LORA])
        acc_scr[...] = acc_scr[...] * _lane_tile(alpha, KV_LORA // LANES) + pv

    n_full = (i * tq) // tk

    def body(j, carry):
        step(pl.multiple_of(j * tk, tk), False)
        return carry

    lax.fori_loop(0, n_full, body, 0)
    step(pl.multiple_of(n_full * tk, tk), True)
    inv = 1.0 / l_scr[...]
    o = acc_scr[...] * _lane_tile(inv, KV_LORA // LANES)
    o_ref[0] = o.astype(BF16).reshape(MLA_HEADS, tq, KV_LORA)


def _attn_prompt(q, kcat, tq, tk):
    B, H, L, _ = q.shape
    rows = H * tq
    kern = functools.partial(_attn_kernel, tq=tq, tk=tk)
    return pl.pallas_call(
        kern, grid=(B, L // tq),
        in_specs=[pl.BlockSpec((1, H, tq, QK_DIM), lambda b, i: (b, 0, i, 0)),
                  pl.BlockSpec((1, L, QK_DIM), lambda b, i: (b, 0, 0))],
        out_specs=pl.BlockSpec((1, H, tq, KV_LORA), lambda b, i: (b, 0, i, 0)),
        out_shape=jax.ShapeDtypeStruct((B, H, L, KV_LORA), BF16),
        scratch_shapes=[pltpu.VMEM((rows, LANES), F32), pltpu.VMEM((rows, LANES), F32),
                        pltpu.VMEM((rows, KV_LORA), F32)],
        compiler_params=_params(("arbitrary", "arbitrary")), name="attn_prompt",
    )(q, kcat)


def _attn_sample_kernel(q_ref, kn_ref, pc_ref, pr_ref, o_ref, *, past, lq):
    rows = MLA_HEADS * lq
    q = q_ref[0].reshape(rows, QK_DIM)
    pc = pc_ref[0].astype(BF16)
    pr = pr_ref[0].astype(BF16)
    kn = kn_ref[0]
    s_p = _dot_nt(q[:, :KV_LORA], pc) + _dot_nt(q[:, KV_LORA:], pr)
    s_n = _dot_nt(q, kn)
    qpos_p = past + lax.rem(lax.broadcasted_iota(I32, (rows, past), 0), lq)
    kpos_p = lax.broadcasted_iota(I32, (rows, past), 1)
    s_p = jnp.where(kpos_p // CHUNK <= qpos_p // CHUNK, s_p, NEG_INF)
    qpos_n = past + lax.rem(lax.broadcasted_iota(I32, (rows, lq), 0), lq)
    kpos_n = past + lax.broadcasted_iota(I32, (rows, lq), 1)
    s_n = jnp.where(kpos_n // CHUNK <= qpos_n // CHUNK, s_n, NEG_INF)
    m = jnp.maximum(jnp.max(s_p, axis=1, keepdims=True), jnp.max(s_n, axis=1, keepdims=True))
    p_p = jnp.exp2(s_p - m)
    p_n = jnp.exp2(s_n - m)
    l = jnp.sum(p_p, axis=1, keepdims=True) + jnp.sum(p_n, axis=1, keepdims=True)
    o = _dot(p_p.astype(BF16), pc) + _dot(p_n.astype(BF16), kn[:, :KV_LORA])
    o_ref[0] = (o / l).astype(BF16).reshape(MLA_HEADS, lq, KV_LORA)


def _attn_sample(q, kcat, past_ckv, past_kr):
    B, lq, _ = kcat.shape
    past = past_ckv.shape[1]
    H = MLA_HEADS
    kern = functools.partial(_attn_sample_kernel, past=past, lq=lq)
    return pl.pallas_call(
        kern, grid=(B,),
        in_specs=[pl.BlockSpec((1, H, lq, QK_DIM), lambda b: (0, 0, b, 0)),
                  pl.BlockSpec((1, lq, QK_DIM), lambda b: (b, 0, 0)),
                  pl.BlockSpec((1, past, KV_LORA), lambda b: (b, 0, 0)),
                  pl.BlockSpec((1, past, ROPE_DIM), lambda b: (b, 0, 0))],
        out_specs=pl.BlockSpec((1, H, lq, KV_LORA), lambda b: (0, 0, b, 0)),
        out_shape=jax.ShapeDtypeStruct((1, H, B * lq, KV_LORA), BF16),
        compiler_params=_params(("arbitrary",)), name="attn_sample",
    )(q, kcat, past_ckv, past_kr)


def _s5_kernel(u_ref, h0_ref, bre_ref, bim_ref, brel_ref, biml_ref, lt_ref, pinr_ref, pini_ref,
               pwr_ref, pwi_ref, a_ref, cre_ref, cim_ref, d_ref, wglu_ref, bglu_ref, gout_ref,
               o_ref, hl_ref, st_scr, cum_scr, hs_scr, *, tm, precise):
    i = pl.program_id(1)
    ns = st_scr.shape[1]
    half = ns // 2
    wh = u_ref.shape[-1] // 2

    @pl.when(i == 0)
    def _():
        st_scr[...] = h0_ref[0]

    u = u_ref[0]
    lt = lt_ref[...]
    for hf in range(2):
        uh = u[:, hf * wh:(hf + 1) * wh]
        if precise:
            u_hi, u_lo = _split(uh)
            bu_re = _dot(u_hi, bre_ref[hf]) + _dot(u_lo, bre_ref[hf]) + _dot(u_hi, brel_ref[hf])
            bu_im = _dot(u_hi, bim_ref[hf]) + _dot(u_lo, bim_ref[hf]) + _dot(u_hi, biml_ref[hf])
        else:
            u_hi = uh.astype(BF16)
            bu_re = _dot(u_hi, bre_ref[hf])
            bu_im = _dot(u_hi, bim_ref[hf])
        sl = slice(hf * half, (hf + 1) * half)
        pr = pinr_ref[:, sl]
        pi = pini_ref[:, sl]
        v_re = pr * bu_re - pi * bu_im
        v_im = pr * bu_im + pi * bu_re
        for part, v in ((0, v_re), (1, v_im)):
            if precise:
                v_hi, v_lo = _split(v)
                c = _dot(lt, v_hi) + _dot(lt, v_lo)
            else:
                c = _dot(lt, v.astype(BF16))
            cum_scr[:, part * ns + hf * half:part * ns + (hf + 1) * half] = c

    a_re = a_ref[0:1, :]
    a_im = a_ref[1:2, :]
    pw_re = pwr_ref[...]
    pw_im = pwi_ref[...]

    def chunk(c, carry):
        s_re, s_im = carry
        r0 = pl.multiple_of(c * S5_SUB, S5_SUB)
        t_re = cum_scr[pl.ds(r0, S5_SUB), 0:ns] + (a_re * s_re - a_im * s_im)
        t_im = cum_scr[pl.ds(r0, S5_SUB), ns:2 * ns] + (a_re * s_im + a_im * s_re)
        h_re = pw_re * t_re - pw_im * t_im
        h_im = pw_re * t_im + pw_im * t_re
        hs_scr[pl.ds(r0, S5_SUB), 0:ns] = h_re.astype(BF16)
        hs_scr[pl.ds(r0, S5_SUB), ns:2 * ns] = h_im.astype(BF16)
        return h_re[S5_SUB - 1:S5_SUB, :], h_im[S5_SUB - 1:S5_SUB, :]

    s_re, s_im = lax.fori_loop(0, tm // S5_SUB, chunk, (st_scr[0:1, :], st_scr[1:2, :]))
    st_scr[0:1, :] = s_re
    st_scr[1:2, :] = s_im
    hl_ref[0, 0:1, :] = s_re
    hl_ref[0, 1:2, :] = s_im

    ys = []
    for hf in range(2):
        hre = hs_scr[:, hf * half:(hf + 1) * half]
        him = hs_scr[:, ns + hf * half:ns + (hf + 1) * half]
        ys.append(_dot(hre, cre_ref[hf]) + _dot(him, cim_ref[hf]))
    y = jnp.concatenate(ys, axis=1) + d_ref[...] * u
    zg = jax.nn.gelu(y)
    gl = _dot(zg.astype(BF16), wglu_ref[...]) + bglu_ref[...]
    o = zg * jax.nn.sigmoid(gl)
    o_ref[0] = _rms(o, gout_ref[...]).astype(BF16)


def _s5(u, h0, tabs, wts, tm, precise):
    B, L, W = u.shape
    ns = h0.shape[-1]
    consts = [tabs["b_re"], tabs["b_im"], tabs["b_re_lo"], tabs["b_im_lo"], tabs["lt"], tabs["pin_re"],
              tabs["pin_im"], tabs["pw_re"], tabs["pw_im"], tabs["a"], tabs["c_re"], tabs["c_im"],
              wts["s5_d"], wts["w_glu"], wts["b_glu"], wts["g_out_s5"]]
    kern = functools.partial(_s5_kernel, tm=tm, precise=precise)
    return pl.pallas_call(
        kern, grid=(B, L // tm),
        in_specs=[pl.BlockSpec((1, tm, W), lambda b, i: (b, i, 0)),
                  pl.BlockSpec((1, 2, ns), lambda b, i: (b, 0, 0))] + [_const_spec(c.shape) for c in consts],
        out_specs=(pl.BlockSpec((1, tm, W), lambda b, i: (b, i, 0)),
                   pl.BlockSpec((1, 2, ns), lambda b, i: (b, 0, 0))),
        out_shape=(jax.ShapeDtypeStruct((B, L, W), BF16), jax.ShapeDtypeStruct((B, 2, ns), F32)),
        scratch_shapes=[pltpu.VMEM((2, ns), F32), pltpu.VMEM((tm, 2 * ns), F32), pltpu.VMEM((tm, 2 * ns), BF16)],
        compiler_params=_params(("arbitrary", "arbitrary")), name="s5",
    )(u, h0, *consts)


def _s5_tables(a_re_p, a_im_p, b_re_p, b_im_p, c_re_p, c_im_p, log_dt, tm):
    G, N = a_re_p.shape
    CH = b_re_p.shape[-1]
    dt = jnp.exp(log_dt.astype(F32))[:, None]
    lr = a_re_p.astype(F32) * dt
    li = a_im_p.astype(F32) * dt
    er = jnp.exp(lr)
    ab_re, ab_im = er * jnp.cos(li), er * jnp.sin(li)
    lam2 = a_re_p.astype(F32) ** 2 + a_im_p.astype(F32) ** 2
    nr, ni = ab_re - 1.0, ab_im
    f_re = (nr * a_re_p + ni * a_im_p) / lam2
    f_im = (ni * a_re_p - nr * a_im_p) / lam2
    bb_re = f_re[..., None] * b_re_p - f_im[..., None] * b_im_p
    bb_im = f_re[..., None] * b_im_p + f_im[..., None] * b_re_p
    gh = G // 2
    eye = jnp.eye(gh, dtype=F32)

    def blk_b(bb):
        t = bb.reshape(2, gh, N, CH)
        m = jnp.einsum("hgnc,gk->hgckn", t, eye)
        return m.reshape(2, gh * CH, gh * N)

    def blk_c(cc):
        t = cc.reshape(2, gh, CH, N)
        m = jnp.einsum("hgcn,gk->hgnkc", t, eye)
        return m.reshape(2, gh * N, gh * CH)

    b_re_m, b_im_m = blk_b(bb_re), blk_b(bb_im)
    b_re_hi, b_re_lo = _split(b_re_m)
    b_im_hi, b_im_lo = _split(b_im_m)
    s = jnp.arange(S5_SUB, dtype=F32)[:, None, None]

    def powers(sign):
        e = jnp.exp(sign * lr[None] * s)
        return ((e * jnp.cos(sign * li[None] * s)).reshape(S5_SUB, G * N),
                (e * jnp.sin(sign * li[None] * s)).reshape(S5_SUB, G * N))

    pin_re, pin_im = powers(-1.0)
    pw_re, pw_im = powers(1.0)
    reps = tm // S5_SUB
    r = np.arange(tm)
    lt = ((r[:, None] // S5_SUB == r[None, :] // S5_SUB) & (r[None, :] <= r[:, None])).astype(np.float32)
    return {
        "b_re": b_re_hi, "b_im": b_im_hi, "b_re_lo": b_re_lo, "b_im_lo": b_im_lo,
        "lt": jnp.asarray(lt, BF16),
        "pin_re": jnp.tile(pin_re, (reps, 1)), "pin_im": jnp.tile(pin_im, (reps, 1)),
        "pw_re": pw_re, "pw_im": pw_im,
        "a": jnp.stack([ab_re.reshape(G * N), ab_im.reshape(G * N)]),
        "c_re": blk_c(c_re_p.astype(F32)).astype(BF16), "c_im": (-blk_c(c_im_p.astype(F32))).astype(BF16),
    }


def _route(scores, sel):
    E, tm = scores.shape
    ge = E // N_EXPERT_GROUPS
    io_g = lax.broadcasted_iota(I32, (ge, tm), 0)
    gs_rows = []
    for g in range(N_EXPERT_GROUPS):
        sg = sel[g * ge:(g + 1) * ge, :]
        m1 = jnp.max(sg, axis=0, keepdims=True)
        i1 = jnp.min(jnp.where(sg == m1, io_g, BIG_I32), axis=0, keepdims=True)
        m2 = jnp.max(jnp.where(io_g == i1, NEG_INF, sg), axis=0, keepdims=True)
        gs_rows.append(m1 + m2)
    gs = jnp.concatenate(gs_rows, axis=0)
    gio = lax.broadcasted_iota(I32, gs.shape, 0)
    gsel = jnp.zeros(gs.shape, F32)
    for _ in range(TOPK_GROUPS):
        mx = jnp.max(gs, axis=0, keepdims=True)
        ix = jnp.min(jnp.where(gs == mx, gio, BIG_I32), axis=0, keepdims=True)
        hit = gio == ix
        gsel = jnp.where(hit, 1.0, gsel)
        gs = jnp.where(hit, NEG_INF, gs)
    emask = jnp.concatenate([jnp.broadcast_to(gsel[g:g + 1, :], (ge, tm)) for g in range(N_EXPERT_GROUPS)], axis=0)
    cand = jnp.where(emask > 0.0, sel, NEG_INF)
    eio = lax.broadcasted_iota(I32, (E, tm), 0)
    idxs, gates = [], []
    for _ in range(TOP_K):
        mx = jnp.max(cand, axis=0, keepdims=True)
        ix = jnp.min(jnp.where(cand == mx, eio, BIG_I32), axis=0, keepdims=True)
        hit = eio == ix
        gates.append(jnp.sum(jnp.where(hit, scores, 0.0), axis=0, keepdims=True))
        idxs.append(ix)
        cand = jnp.where(hit, NEG_INF, cand)
    eidx = jnp.concatenate(idxs, axis=0)
    gate = jnp.concatenate(gates, axis=0)
    gate = gate / jnp.sum(gate, axis=0, keepdims=True) * ROUTED_SCALE
    return eidx, gate


def _post_kernel(x_ref, ol_ref, os_ref, gtm_ref, shf_ref, scf_ref, gtf_ref, wuv_ref, goa_ref, wout_ref,
                 gffn_ref, wrh_ref, wrl_ref, br_ref, wsg_ref, wsu_ref, wsd_ref,
                 h2_ref, base_ref, eidx_ref, gate_ref):
    x = x_ref[0]
    oa = jnp.concatenate([_dot(ol_ref[0, hd], wuv_ref[hd]) for hd in range(MLA_HEADS)], axis=1)
    oan = _rms(oa, goa_ref[...]).astype(BF16)
    wa = oan.shape[1]
    mix = _dot(oan, wout_ref[:wa, :]) + _dot(os_ref[0], wout_ref[wa:, :])
    x1 = x + gtm_ref[0] * mix
    h2 = _rms(x1, gffn_ref[...]) * (1.0 + scf_ref[0]) + shf_ref[0]
    h2_hi, h2_lo = _split(h2)
    for j, plane in enumerate(_pack_rows(h2)):
        h2_ref[j] = plane
    sh =_dot((_silu(_dot(h2_hi, wsg_ref[...])) * _dot(h2_hi, wsu_ref[...])).astype(BF16), wsd_ref[...])
    base_ref[0] = x1 + gtf_ref[0] * sh
    wr_hi = wrh_ref[...]
    logits = _dot_nt(wr_hi, h2_hi) + _dot_nt(wrl_ref[...], h2_hi) + _dot_nt(wr_hi, h2_lo)
    scores = jax.nn.sigmoid(logits)
    eidx, gate = _route(scores, scores + br_ref[...])
    eidx_ref[...] = eidx
    gate_ref[...] = gate


def _post(x, o_lat, o_s5, mods, wts, tm):
    B, L, D = x.shape
    nt = L // tm
    per_row = mods[0].shape[1] != 1
    mod_spec = (pl.BlockSpec((1, tm, D), lambda b, i: (b, i, 0)) if per_row
                else pl.BlockSpec((1, 1, D), lambda b, i: (b, 0, 0)))
    W = o_s5.shape[-1]
    consts = [wts["w_uv"], wts["g_out_attn"], wts["w_out"], wts["g_ffn"], wts["wr_hi"], wts["wr_lo"],
              wts["b_router"], wts["w_sg"], wts["w_su"], wts["w_sd"]]
    in_specs = [pl.BlockSpec((1, tm, D), lambda b, i: (b, i, 0)),
                pl.BlockSpec((1, MLA_HEADS, tm, KV_LORA), lambda b, i: (b, 0, i, 0)),
                pl.BlockSpec((1, tm, W), lambda b, i: (b, i, 0)),
                mod_spec, mod_spec, mod_spec, mod_spec] + [_const_spec(c.shape) for c in consts]
    planes = D // (2 * LANES)
    out_shape = (jax.ShapeDtypeStruct((planes, B * L, LANES), U32), jax.ShapeDtypeStruct((B, L, D), F32),
                 jax.ShapeDtypeStruct((TOP_K, B * L), I32), jax.ShapeDtypeStruct((TOP_K, B * L), F32))
    out_specs = (pl.BlockSpec((planes, tm, LANES), lambda b, i: (0, b * nt + i, 0)),
                 pl.BlockSpec((1, tm, D), lambda b, i: (b, i, 0)),
                 pl.BlockSpec((TOP_K, tm), lambda b, i: (0, b * nt + i)),
                 pl.BlockSpec((TOP_K, tm), lambda b, i: (0, b * nt + i)))
    return pl.pallas_call(
        _post_kernel, grid=(B, nt), in_specs=in_specs, out_specs=out_specs, out_shape=out_shape,
        compiler_params=_params(("arbitrary", "arbitrary")), name="post",
    )(x, o_lat, o_s5, *mods, *consts)


def _rank_kernel(eidx_ref, tri_ref, rank_ref, cnt_ref, carry_scr):
    i = pl.program_id(0)

    @pl.when(i == 0)
    def _():
        carry_scr[...] = jnp.zeros(carry_scr.shape, F32)

    eidx = eidx_ref[...]
    tt = eidx.shape[1]
    eio = lax.broadcasted_iota(I32, (N_EXPERTS, tt), 0)
    hits = [eio == eidx[k:k + 1, :] for k in range(TOP_K)]
    onehot = jnp.zeros((N_EXPERTS, tt), F32)
    for hit in hits:
        onehot = jnp.where(hit, 1.0, onehot)
    before = _dot(onehot.astype(BF16), tri_ref[...]) + carry_scr[...]
    ranks = [jnp.sum(jnp.where(hit, before, 0.0), axis=0, keepdims=True) for hit in hits]
    rank_ref[...] = jnp.concatenate(ranks, axis=0).astype(I32)
    carry = carry_scr[...] + jnp.sum(onehot, axis=1, keepdims=True)
    carry_scr[...] = carry
    cnt_ref[...] = carry


def _rank(eidx, tt):
    K, T = eidx.shape
    r = np.arange(tt)
    tri = jnp.asarray((r[:, None] < r[None, :]).astype(np.float32), BF16)
    return pl.pallas_call(
        _rank_kernel, grid=(T // tt,),
        in_specs=[pl.BlockSpec((K, tt), lambda i: (0, i)), _const_spec((tt, tt))],
        out_specs=(pl.BlockSpec((K, tt), lambda i: (0, i)), _const_spec((N_EXPERTS, 1))),
        out_shape=(jax.ShapeDtypeStruct((K, T), I32), jax.ShapeDtypeStruct((N_EXPERTS, 1), F32)),
        scratch_shapes=[pltpu.VMEM((N_EXPERTS, 1), F32)],
        compiler_params=_params(("arbitrary",)), name="rank",
    )(eidx, tri)


def _pos_kernel(eidx_ref, rank_ref, pstart_ref, idx_ref, *, rows_pad):
    eidx = eidx_ref[...]
    tt = eidx.shape[1]
    eio = lax.broadcasted_iota(I32, (N_EXPERTS, tt), 0)
    pstart = pstart_ref[...]
    starts = [jnp.sum(jnp.where(eio == eidx[k:k + 1, :], pstart, 0.0), axis=0, keepdims=True)
              for k in range(TOP_K)]
    pos = jnp.concatenate(starts, axis=0).astype(I32) + rank_ref[...]
    for j in range(idx_ref.shape[0]):
        idx_ref[j] = pos + j * rows_pad


def _pair_rows(eidx, rank, pstart, planes, rows_pad, tt):
    K, T = eidx.shape
    return pl.pallas_call(
        functools.partial(_pos_kernel, rows_pad=rows_pad), grid=(T // tt,),
        in_specs=[pl.BlockSpec((K, tt), lambda i: (0, i)), pl.BlockSpec((K, tt), lambda i: (0, i)),
                  _const_spec((N_EXPERTS, 1))],
        out_specs=pl.BlockSpec((planes, K, tt), lambda i: (0, 0, i)),
        out_shape=jax.ShapeDtypeStruct((planes, K, T), I32),
        compiler_params=_params(("arbitrary",)), name="pair_rows",
    )(eidx, rank, pstart)


def _expert_kernel(blk_e_ref, nvalid_ref, nused_ref, x_ref, wg_ref, wu_ref, wd_ref, o_ref):
    b = pl.program_id(0)

    @pl.when(b < nused_ref[0])
    def _():
        planes = x_ref.shape[0]
        x = _unpack_rows([x_ref[j] for j in range(planes)])
        live = lax.broadcasted_iota(I32, x.shape, 0) < nvalid_ref[b]
        x = jnp.where(live, x, jnp.zeros_like(x))
        g = _dot(x, wg_ref[0].astype(BF16))
        u = _dot(x, wu_ref[0].astype(BF16))
        y = _dot((_silu(g) * u).astype(BF16), wd_ref[0].astype(BF16))
        for j, plane in enumerate(_pack_rows(y)):
            o_ref[j] = plane


def _experts(xs, blk_e, nvalid, nused, wg, wu, wd):
    planes, rows, _ = xs.shape
    nb = rows // MOE_ROWS
    _, D, F = wg.shape

    def xmap(b, blk_e_ref, nvalid_ref, nused_ref):
        return (0, jnp.minimum(b, nused_ref[0] - 1), 0)

    def wmap(b, blk_e_ref, nvalid_ref, nused_ref):
        return (blk_e_ref[b], 0, 0)

    grid_spec = pltpu.PrefetchScalarGridSpec(
        num_scalar_prefetch=3, grid=(nb,),
        in_specs=[pl.BlockSpec((planes, MOE_ROWS, LANES), xmap),
                  pl.BlockSpec((1, D, F), wmap), pl.BlockSpec((1, D, F), wmap), pl.BlockSpec((1, F, D), wmap)],
        out_specs=pl.BlockSpec((planes, MOE_ROWS, LANES), xmap))
    return pl.pallas_call(
        _expert_kernel, grid_spec=grid_spec, out_shape=jax.ShapeDtypeStruct(xs.shape, U32),
        compiler_params=_params(("arbitrary",)), name="experts",
    )(blk_e, nvalid, nused, xs, wg, wu, wd)


def _final_kernel(base_ref, ys_ref, gate_ref, gtf_ref, gfin_ref, o_ref):
    planes = ys_ref.shape[0]
    gate = gate_ref[...]
    routed = jnp.zeros(base_ref.shape[1:], F32)
    for k in range(TOP_K):
        rows = _unpack_rows([ys_ref[j, k] for j in range(planes)])
        routed = routed + gate[:, k:k + 1] * rows.astype(F32)
    y = base_ref[0] + gtf_ref[0] * routed
    o_ref[0] = _rms(y, gfin_ref[...])


def _final(base, ysg, gate_t, gt_f, g_final, tm, row0):
    B, L, D = base.shape
    planes = ysg.shape[0]
    nt = L // tm
    off = row0 // tm
    per_row = gt_f.shape[1] != 1
    mod_spec = (pl.BlockSpec((1, tm, D), lambda b, i: (b, i, 0)) if per_row
                else pl.BlockSpec((1, 1, D), lambda b, i: (b, 0, 0)))
    return pl.pallas_call(
        _final_kernel, grid=(B, nt),
        in_specs=[pl.BlockSpec((1, tm, D), lambda b, i: (b, i, 0)),
                  pl.BlockSpec((planes, TOP_K, tm, LANES), lambda b, i: (0, 0, off + b * nt + i, 0)),
                  pl.BlockSpec((tm, TOP_K), lambda b, i: (off + b * nt + i, 0)),
                  mod_spec, _const_spec((1, D))],
        out_specs=pl.BlockSpec((1, tm, D), lambda b, i: (b, i, 0)),
        out_shape=jax.ShapeDtypeStruct((B, L, D), F32),
        compiler_params=_params(("arbitrary", "arbitrary")), name="final",
    )(base, ysg, gate_t, gt_f, g_final)


def _sc_mesh():
    return plsc.VectorSubcoreMesh(core_axis_name="c", subcore_axis_name="s")


def _sc_worker():
    return lax.axis_index("s") * SC_CORES + lax.axis_index("c")


def _sc_dispatch(src, idx, out_rows, planes, n_tok):
    n_win = n_tok // SC_WINDOW
    n_iter = -(-n_win // SC_WORKERS)

    @functools.partial(
        pl.kernel, mesh=_sc_mesh(), out_type=jax.ShapeDtypeStruct((out_rows, LANES), src.dtype),
        scratch_types=[pltpu.VMEM((SC_WINDOW,), I32), pltpu.VMEM((SC_WINDOW, LANES), src.dtype),
                       pltpu.SemaphoreType.DMA],
        name="sc_dispatch")
    def k(src_hbm, idx_hbm, out_hbm, idx_v, rows_v, sem):
        wid = _sc_worker()

        @pl.loop(0, n_iter)
        def _(i):
            win = i * SC_WORKERS + wid

            @pl.when(win < n_win)
            def _():
                t0 = win * SC_WINDOW
                for j in range(planes):
                    pltpu.sync_copy(src_hbm.at[pl.ds(pl.multiple_of(j * n_tok + t0, SC_WINDOW), SC_WINDOW)], rows_v)
                    for kk in range(TOP_K):
                        off = pl.multiple_of((j * TOP_K + kk) * n_tok + t0, SC_WINDOW)
                        pltpu.sync_copy(idx_hbm.at[pl.ds(off, SC_WINDOW)], idx_v)
                        pltpu.async_copy(rows_v, out_hbm.at[idx_v], sem).wait()

    return k(src, idx)


def _sc_gather(table, idx):
    n_idx = idx.shape[0]
    n_iter = n_idx // (SC_WINDOW * SC_WORKERS)

    @functools.partial(
        pl.kernel, mesh=_sc_mesh(), out_type=jax.ShapeDtypeStruct((n_idx, LANES), table.dtype),
        scratch_types=[pltpu.VMEM((SC_WINDOW,), I32), pltpu.VMEM((SC_WINDOW, LANES), table.dtype),
                       pltpu.SemaphoreType.DMA],
        name="sc_gather")
    def k(table_hbm, idx_hbm, out_hbm, idx_v, rows_v, sem):
        wid = _sc_worker()

        @pl.loop(0, n_iter)
        def _(i):
            base = pl.multiple_of((i * SC_WORKERS + wid) * SC_WINDOW, SC_WINDOW)
            pltpu.sync_copy(idx_hbm.at[pl.ds(base, SC_WINDOW)], idx_v)
            pltpu.async_copy(table_hbm.at[idx_v], rows_v, sem).wait()
            pltpu.sync_copy(rows_v, out_hbm.at[pl.ds(base, SC_WINDOW)])

    return k(table, idx)


def _rope_tables(pos):
    half = ROPE_DIM // 2
    inv = ROPE_THETA ** (-jnp.arange(half, dtype=F32) / half)
    ang = pos.astype(F32)[:, None] * inv[None, :]
    cos = jnp.tile(jnp.cos(ang), (1, 2 * MLA_HEADS))
    sin = jnp.tile(jnp.sin(ang), (1, 2 * MLA_HEADS))
    return cos, sin


def _rot_cols(w):
    half = ROPE_DIM // 2
    return jnp.concatenate([-w[..., half:], w[..., :half]], axis=-1)


def _layer_weights(w_in, g_mix, g_q, g_kv, w_uq, w_uk, w_uv, s5_D, s5_w_glu, s5_b_glu, g_out_attn, g_out_s5,
                   w_out, g_ffn, w_router, b_router, w_sh_gate, w_sh_up, w_sh_down):
    D = w_in.shape[0]
    o1, o2, o3 = Q_LORA, Q_LORA + KV_LORA, Q_LORA + KV_LORA + ROPE_DIM
    s5w = w_in.shape[1] - o3
    w_rope = w_in[:, o2:o3]
    w_in_ext = jnp.concatenate([w_in[:, :o2], w_in[:, o3:], w_rope, _rot_cols(w_rope)], axis=1).astype(BF16)
    wq = w_uq.reshape(Q_LORA, MLA_HEADS, NOPE_DIM + ROPE_DIM)
    wq_rope = wq[:, :, NOPE_DIM:]
    wr_hi, wr_lo = _split(w_router.T)
    return {
        "s5w": s5w,
        "g_mix": g_mix.reshape(1, D), "w_in": w_in_ext,
        "g_q": g_q.reshape(1, Q_LORA), "g_kv": g_kv.reshape(1, KV_LORA),
        "wq_nope": wq[:, :, :NOPE_DIM].reshape(Q_LORA, MLA_HEADS * NOPE_DIM).astype(BF16),
        "wq_rope": wq_rope.reshape(Q_LORA, MLA_HEADS * ROPE_DIM).astype(BF16),
        "wq_rot": _rot_cols(wq_rope).reshape(Q_LORA, MLA_HEADS * ROPE_DIM).astype(BF16),
        "w_ukT": jnp.transpose(w_uk, (1, 2, 0)).astype(BF16),
        "w_uv": jnp.transpose(w_uv, (1, 0, 2)).astype(BF16),
        "s5_d": s5_D.reshape(1, s5w), "w_glu": s5_w_glu.astype(BF16), "b_glu": s5_b_glu.reshape(1, s5w),
        "g_out_attn": g_out_attn.reshape(1, -1), "g_out_s5": g_out_s5.reshape(1, s5w),
        "w_out": w_out.astype(BF16), "g_ffn": g_ffn.reshape(1, D),
        "wr_hi": wr_hi, "wr_lo": wr_lo, "b_router": b_router.reshape(N_EXPERTS, 1),
        "w_sg": w_sh_gate.astype(BF16), "w_su": w_sh_up.astype(BF16), "w_sd": w_sh_down.astype(BF16),
    }


def _state_in(s_re, s_im):
    B = s_re.shape[0]
    return jnp.stack([s_re.reshape(B, -1), s_im.reshape(B, -1)], axis=1).astype(F32)


def kernel(x_prompt, x_sample, c_prompt, c_sample, cache_ckv, cache_krope, state_s5_re, state_s5_im, w_ada, b_ada, g_mix, g_ffn, w_in, g_q, w_uq, g_kv, w_uk, w_uv, s5_A_re, s5_A_im, s5_B_re, s5_B_im, s5_C_re, s5_C_im, s5_D, s5_log_dt, s5_w_glu, s5_b_glu, g_out_attn, g_out_s5, w_out, w_router, b_router, w_exp_gate, w_exp_up, w_exp_down, w_sh_gate, w_sh_up, w_sh_down, g_final):
    Bp, Lp, D = x_prompt.shape
    Bs, Ls, _ = x_sample.shape
    depth = w_ada.shape[0]
    assert depth == 1, "single-layer step"
    past = cache_ckv.shape[2]
    G, N = s5_A_re.shape[1:]
    Ts = Bs * Ls
    Tp = Bp * Lp
    l = 0

    wts = _layer_weights(w_in[l], g_mix[l], g_q[l], g_kv[l], w_uq[l], w_uk[l], w_uv[l], s5_D[l], s5_w_glu[l],
                         s5_b_glu[l], g_out_attn[l], g_out_s5[l], w_out[l], g_ffn[l], w_router[l], b_router[l],
                         w_sh_gate[l], w_sh_up[l], w_sh_down[l])

    mod = _ada(jnp.concatenate([c_prompt, c_sample], axis=0), w_ada[l], b_ada[l])
    mod_p = [m.reshape(Bp, 1, D) for m in jnp.split(mod[:Bp], 6, axis=-1)]
    mod_s = [jnp.broadcast_to(m[:, None, :], (Bs, Ls, D)).reshape(1, Ts, D)
             for m in jnp.split(mod[Bp:], 6, axis=-1)]

    tm_p = 256
    cos_p, sin_p = _rope_tables(jnp.arange(Lp))
    cos_s, sin_s = _rope_tables(jnp.tile(past + jnp.arange(Ls), Bs))

    ckv_p, kr_p, u_p, kcat_p, q_p = _pre(x_prompt, mod_p[0], mod_p[1], cos_p, sin_p, wts, tm_p)
    olat_p = _attn_prompt(q_p, kcat_p, tq=256, tk=512)
    tabs_p = _s5_tables(s5_A_re[l], s5_A_im[l], s5_B_re[l], s5_B_im[l], s5_C_re[l], s5_C_im[l], s5_log_dt[l], tm_p)
    os5_p, hl_p = _s5(u_p, jnp.zeros((Bp, 2, G * N), F32), tabs_p, wts, tm_p, precise=False)
    h2_p, base_p, eidx_p, gate_p = _post(x_prompt, olat_p, os5_p, [mod_p[2], mod_p[3], mod_p[4], mod_p[5]], wts, tm_p)

    xs_rows = x_sample.reshape(1, Ts, D)
    ckv_s, kr_s, u_s, kcat_s, q_s = _pre(xs_rows, mod_s[0], mod_s[1], cos_s, sin_s, wts, Ts)
    olat_s = _attn_sample(q_s, kcat_s.reshape(Bs, Ls, QK_DIM), cache_ckv[l], cache_krope[l])
    tabs_s = _s5_tables(s5_A_re[l], s5_A_im[l], s5_B_re[l], s5_B_im[l], s5_C_re[l], s5_C_im[l], s5_log_dt[l], Ls)
    os5_s, hl_s = _s5(u_s.reshape(Bs, Ls, -1), _state_in(state_s5_re[l], state_s5_im[l]), tabs_s, wts, Ls,
                      precise=True)
    h2_s, base_s, eidx_s, gate_s = _post(xs_rows, olat_s, os5_s.reshape(1, Ts, -1),
                                         [mod_s[2], mod_s[3], mod_s[4], mod_s[5]], wts, Ts)

    T_all = Tp + Ts
    eidx = jnp.concatenate([eidx_p, eidx_s], axis=1)
    gate = jnp.concatenate([gate_p, gate_s], axis=1)
    rank, counts = _rank(eidx, 256)
    counts = counts.reshape(N_EXPERTS).astype(I32)
    nblk = (counts + MOE_ROWS - 1) // MOE_ROWS
    blk_end = jnp.cumsum(nblk)
    pstart = (blk_end - nblk) * MOE_ROWS
    nb_max = (T_all * TOP_K) // MOE_ROWS + N_EXPERTS
    nused = blk_end[-1]
    blk_e = jnp.minimum(jnp.searchsorted(blk_end, jnp.arange(nb_max, dtype=I32), side="right"),
                        N_EXPERTS - 1).astype(I32)
    bidx = jnp.arange(nb_max, dtype=I32)
    blk_e = jnp.where(bidx < nused, blk_e, blk_e[jnp.maximum(nused - 1, 0)])
    nvalid = jnp.clip(counts[blk_e] - (bidx - (blk_end - nblk)[blk_e]) * MOE_ROWS, 0, MOE_ROWS).astype(I32)
    rows_pad = nb_max * MOE_ROWS
    planes = h2_p.shape[0]
    idx = _pair_rows(eidx, rank, pstart.astype(F32).reshape(N_EXPERTS, 1), planes, rows_pad, 256).reshape(-1)
    h2_all = jnp.concatenate([h2_p, h2_s], axis=1).reshape(planes * T_all, LANES)
    xs_sorted = _sc_dispatch(h2_all, idx, planes * rows_pad, planes, T_all).reshape(planes, rows_pad, LANES)
    ys = _experts(xs_sorted, blk_e, nvalid, nused.reshape(1).astype(I32), w_exp_gate[l], w_exp_up[l], w_exp_down[l])
    ysg = _sc_gather(ys.reshape(planes * rows_pad, LANES), idx).reshape(planes, TOP_K, T_all, LANES)
    gate_t = gate.T

    gfin = g_final.reshape(1, D)
    y_p = _final(base_p, ysg, gate_t, mod_p[5], gfin, tm_p, 0)
    y_s = _final(base_s, ysg, gate_t, mod_s[5], gfin, Ts, Tp).reshape(Bs, Ls, D)

    def state_out(hl, B):
        return hl[:, 0].reshape(1, B, G, N), hl[:, 1].reshape(1, B, G, N)

    sre_p, sim_p = state_out(hl_p, Bp)
    sre_s, sim_s = state_out(hl_s, Bs)
    return (y_p, y_s, ckv_p[None], kr_p[None], sre_p, sim_p,
            ckv_s.reshape(1, Bs, Ls, KV_LORA), kr_s.reshape(1, Bs, Ls, ROPE_DIM), sre_s, sim_s)
```

```python
import functools
import math

import numpy as np
import jax
import jax.numpy as jnp
from jax import lax
from jax.experimental import pallas as pl
from jax.experimental.pallas import tpu as pltpu
from jax.experimental.pallas import tpu_sc as plsc

F32 = jnp.float32
BF16 = jnp.bfloat16
I32 = jnp.int32
U32 = jnp.uint32

EPS = 1e-6
CHUNK = 64
MLA_HEADS = 4
NOPE_DIM = 128
ROPE_DIM = 64
V_DIM = 128
Q_LORA = 256
KV_LORA = 256
QK_HEAD = NOPE_DIM + ROPE_DIM
ROPE_THETA = 10000.0
S5_GROUP_CH = 16
S5_STATE = 64
N_EXPERTS = 256
TOP_K = 8
N_EXPERT_GROUPS = 8
TOPK_GROUPS = 4
ROUTED_SCALE = 2.5

S5_SUB = 16
MOE_ROWS = 256
LANES = 128
VMEM_LIMIT = 56 * 1024 * 1024
SC_CORES = 2
SC_SUBCORES = 16
SC_WORKERS = SC_CORES * SC_SUBCORES
SC_WINDOW = 128

NEG_INF = float("-inf")
BIG_I32 = 1 << 30


def _dot(a, b):
    return jnp.dot(a, b, preferred_element_type=F32)


def _dot_nt(a, b):
    return lax.dot_general(a, b, (((1,), (1,)), ((), ())), preferred_element_type=F32)


def _split(a):
    hi = a.astype(BF16)
    lo = (a - hi.astype(F32)).astype(BF16)
    return hi, lo


def _lane_tile(x, n):
    return jnp.concatenate([x] * n, axis=1)


def _rms(x, g):
    return x * lax.rsqrt(jnp.mean(x * x, axis=-1, keepdims=True) + EPS) * g


def _silu(x):
    return x * jax.nn.sigmoid(x)


def _pack_rows(x):
    half = x.shape[1] // 2
    hi = lax.bitcast_convert_type(x[:, :half].astype(BF16).astype(F32), U32)
    lo = lax.bitcast_convert_type(x[:, half:].astype(BF16).astype(F32), U32)
    w = hi | (lo >> 16)
    return [w[:, j * LANES:(j + 1) * LANES] for j in range(half // LANES)]


def _unpack_rows(planes):
    his = [lax.bitcast_convert_type(p & jnp.uint32(0xFFFF0000), F32).astype(BF16) for p in planes]
    los = [lax.bitcast_convert_type(p << 16, F32).astype(BF16) for p in planes]
    return jnp.concatenate(his + los, axis=1)


def _params(sem):
    return pltpu.CompilerParams(dimension_semantics=sem, vmem_limit_bytes=VMEM_LIMIT)


def _const_spec(shape):
    nd = len(shape)
    return pl.BlockSpec(shape, lambda *_: (0,) * nd)


def _ada_kernel(c_ref, whi_ref, wlo_ref, b_ref, o_ref):
    c = c_ref[...]
    s_hi, s_lo = _split(_silu(c))
    w_hi = whi_ref[...]
    o_ref[...] = _dot(s_hi, w_hi) + _dot(s_hi, wlo_ref[...]) + _dot(s_lo, w_hi) + b_ref[...]


def _ada(c, w_ada, b_ada):
    rows, d = c.shape
    n = w_ada.shape[1]
    tn = 512
    w_hi, w_lo = _split(w_ada)
    return pl.pallas_call(
        _ada_kernel,
        grid=(n // tn,),
        in_specs=[_const_spec((rows, d)),
                  pl.BlockSpec((d, tn), lambda j: (0, j)),
                  pl.BlockSpec((d, tn), lambda j: (0, j)),
                  pl.BlockSpec((1, tn), lambda j: (0, j))],
        out_specs=pl.BlockSpec((rows, tn), lambda j: (0, j)),
        out_shape=jax.ShapeDtypeStruct((rows, n), F32),
        compiler_params=_params(("arbitrary",)),
        name="ada",
    )(c, w_hi, w_lo, b_ada.reshape(1, n))


def _pre_kernel(x_ref, sh_ref, sc_ref, g_ref, win_ref, gq_ref, gkv_ref, wqn_ref, wqr_ref, wqt_ref,
                wuk_ref, cos_ref, sin_ref, ckv_ref, kr_ref, u_ref, kcat_ref, v_ref, q_ref, *, scale):
    x = x_ref[0]
    h = _rms(x, g_ref[...]) * (1.0 + sc_ref[0]) + sh_ref[0]
    z = _dot(h.astype(BF16), win_ref[...])
    cq = _rms(z[:, :Q_LORA], gq_ref[...])
    ckv = _rms(z[:, Q_LORA:Q_LORA + KV_LORA], gkv_ref[...])
    o_s5 = Q_LORA + KV_LORA
    s5w = u_ref.shape[-1]
    u_ref[0] = z[:, o_s5:o_s5 + s5w]
    o_r = o_s5 + s5w
    cos = cos_ref[...]
    sin = sin_ref[...]
    kr = z[:, o_r:o_r + ROPE_DIM] * cos[:, :ROPE_DIM] + z[:, o_r + ROPE_DIM:o_r + 2 * ROPE_DIM] * sin[:, :ROPE_DIM]
    ckv_ref[0] = ckv
    kr_ref[0] = kr
    ckvb = ckv.astype(BF16)
    krb = kr.astype(BF16)
    v_ref[0] = ckvb
    kn = _dot(ckvb, wuk_ref[...])
    cqb = cq.astype(BF16)
    qn = _dot(cqb, wqn_ref[...]) * scale
    qr = (_dot(cqb, wqr_ref[...]) * cos + _dot(cqb, wqt_ref[...]) * sin) * scale
    for hd in range(MLA_HEADS):
        kcat_ref[0, hd, :, :NOPE_DIM] = kn[:, hd * NOPE_DIM:(hd + 1) * NOPE_DIM].astype(BF16)
        kcat_ref[0, hd, :, NOPE_DIM:] = krb
        q_ref[0, hd, :, :NOPE_DIM] = qn[:, hd * NOPE_DIM:(hd + 1) * NOPE_DIM].astype(BF16)
        q_ref[0, hd, :, NOPE_DIM:] = qr[:, hd * ROPE_DIM:(hd + 1) * ROPE_DIM].astype(BF16)


def _pre(x, shift, scale_mod, cos_t, sin_t, wts, tm):
    B, L, D = x.shape
    nt = L // tm
    per_row = shift.shape[1] != 1
    mod_spec = (pl.BlockSpec((1, tm, D), lambda b, i: (b, i, 0)) if per_row
                else pl.BlockSpec((1, 1, D), lambda b, i: (b, 0, 0)))
    s5w = wts["s5w"]
    hr = MLA_HEADS * ROPE_DIM
    kern = functools.partial(_pre_kernel, scale=QK_HEAD ** -0.5 * math.log2(math.e))
    consts = [wts["g_mix"], wts["w_in"], wts["g_q"], wts["g_kv"], wts["wq_nope"], wts["wq_rope"],
              wts["wq_rot"], wts["w_uk"]]
    in_specs = [pl.BlockSpec((1, tm, D), lambda b, i: (b, i, 0)), mod_spec, mod_spec]
    in_specs += [_const_spec(c.shape) for c in consts]
    in_specs += [pl.BlockSpec((tm, hr), lambda b, i: (i, 0)), pl.BlockSpec((tm, hr), lambda b, i: (i, 0))]
    out_shape = (jax.ShapeDtypeStruct((B, L, KV_LORA), F32),
                 jax.ShapeDtypeStruct((B, L, ROPE_DIM), F32),
                 jax.ShapeDtypeStruct((B, L, s5w), F32),
                 jax.ShapeDtypeStruct((B, MLA_HEADS, L, QK_HEAD), BF16),
                 jax.ShapeDtypeStruct((B, L, KV_LORA), BF16),
                 jax.ShapeDtypeStruct((B, MLA_HEADS, L, QK_HEAD), BF16))
    out_specs = (pl.BlockSpec((1, tm, KV_LORA), lambda b, i: (b, i, 0)),
                 pl.BlockSpec((1, tm, ROPE_DIM), lambda b, i: (b, i, 0)),
                 pl.BlockSpec((1, tm, s5w), lambda b, i: (b, i, 0)),
                 pl.BlockSpec((1, MLA_HEADS, tm, QK_HEAD), lambda b, i: (b, 0, i, 0)),
                 pl.BlockSpec((1, tm, KV_LORA), lambda b, i: (b, i, 0)),
                 pl.BlockSpec((1, MLA_HEADS, tm, QK_HEAD), lambda b, i: (b, 0, i, 0)))
    return pl.pallas_call(
        kern, grid=(B, nt), in_specs=in_specs, out_specs=out_specs, out_shape=out_shape,
        compiler_params=_params(("arbitrary", "arbitrary")), name="pre",
    )(x, shift, scale_mod, *consts, cos_t, sin_t)


def _attn_kernel(q_ref, k_ref, v_ref, o_ref, m_scr, l_scr, acc_scr, *, t):
    i = pl.program_id(1)
    m_scr[...] = jnp.full(m_scr.shape, NEG_INF, F32)
    l_scr[...] = jnp.zeros(l_scr.shape, F32)
    acc_scr[...] = jnp.zeros(acc_scr.shape, F32)
    visible = (lax.broadcasted_iota(I32, (t, t), 1) // CHUNK) <= (lax.broadcasted_iota(I32, (t, t), 0) // CHUNK)

    def step(j0, masked):
        v = v_ref[0, pl.ds(j0, t), :]
        for hd in range(MLA_HEADS):
            rs = slice(hd * t, (hd + 1) * t)
            s = _dot_nt(q_ref[0, hd], k_ref[0, hd, pl.ds(j0, t), :])
            if masked:
                s = jnp.where(visible, s, NEG_INF)
            m_prev = m_scr[rs]
            m_next = jnp.maximum(m_prev, jnp.max(s, axis=1, keepdims=True))
            alpha = jnp.exp2(m_prev - m_next)
            p = jnp.exp2(s - _lane_tile(m_next, t // LANES))
            l_scr[rs] = alpha * l_scr[rs] + jnp.sum(p, axis=1, keepdims=True)
            m_scr[rs] = m_next
            acc_scr[rs] = acc_scr[rs] * _lane_tile(alpha, KV_LORA // LANES) + _dot(p.astype(BF16), v)

    def body(j, carry):
        step(pl.multiple_of(j * t, t), False)
        return carry

    lax.fori_loop(0, i, body, 0)
    step(pl.multiple_of(i * t, t), True)
    for hd in range(MLA_HEADS):
        rs = slice(hd * t, (hd + 1) * t)
        inv = 1.0 / l_scr[rs]
        o_ref[0, hd] = (acc_scr[rs] * _lane_tile(inv, KV_LORA // LANES)).astype(BF16)


def _attn_prompt(q, kcat, v, t):
    B, H, L, _ = q.shape
    assert L % t == 0 and t % CHUNK == 0
    rows = H * t
    kern = functools.partial(_attn_kernel, t=t)
    resident = pl.Buffered(1)
    return pl.pallas_call(
        kern, grid=(B, L // t),
        in_specs=[pl.BlockSpec((1, H, t, QK_HEAD), lambda b, i: (b, 0, i, 0)),
                  pl.BlockSpec((1, H, L, QK_HEAD), lambda b, i: (b, 0, 0, 0), pipeline_mode=resident),
                  pl.BlockSpec((1, L, KV_LORA), lambda b, i: (b, 0, 0), pipeline_mode=resident)],
        out_specs=pl.BlockSpec((1, H, t, KV_LORA), lambda b, i: (b, 0, i, 0)),
        out_shape=jax.ShapeDtypeStruct((B, H, L, KV_LORA), BF16),
        scratch_shapes=[pltpu.VMEM((rows, LANES), F32), pltpu.VMEM((rows, LANES), F32),
                        pltpu.VMEM((rows, KV_LORA), F32)],
        compiler_params=_params(("arbitrary", "arbitrary")), name="attn_prompt",
    )(q, kcat, v)


def _attn_sample_kernel(q_ref, kn_ref, vn_ref, pc_ref, pr_ref, wuk_ref, o_ref, *, past, lq):
    pc = pc_ref[0].astype(BF16)
    pr = pr_ref[0].astype(BF16)
    kp = _dot(pc, wuk_ref[...]).astype(BF16)
    vn = vn_ref[0]
    qchunk_p = (past + lax.broadcasted_iota(I32, (lq, past), 0)) // CHUNK
    vis_p = lax.broadcasted_iota(I32, (lq, past), 1) // CHUNK <= qchunk_p
    qchunk_n = (past + lax.broadcasted_iota(I32, (lq, lq), 0)) // CHUNK
    vis_n = (past + lax.broadcasted_iota(I32, (lq, lq), 1)) // CHUNK <= qchunk_n
    for hd in range(MLA_HEADS):
        q = q_ref[0, hd]
        s_p = (_dot_nt(q[:, :NOPE_DIM], kp[:, hd * NOPE_DIM:(hd + 1) * NOPE_DIM])
               + _dot_nt(q[:, NOPE_DIM:], pr))
        s_n = _dot_nt(q, kn_ref[0, hd])
        s_p = jnp.where(vis_p, s_p, NEG_INF)
        s_n = jnp.where(vis_n, s_n, NEG_INF)
        m = jnp.maximum(jnp.max(s_p, axis=1, keepdims=True), jnp.max(s_n, axis=1, keepdims=True))
        p_p = jnp.exp2(s_p - m)
        p_n = jnp.exp2(s_n - m)
        l = jnp.sum(p_p, axis=1, keepdims=True) + jnp.sum(p_n, axis=1, keepdims=True)
        o = _dot(p_p.astype(BF16), pc) + _dot(p_n.astype(BF16), vn)
        o_ref[0, hd] = (o / l).astype(BF16)


def _attn_sample(q, kcat, v, past_ckv, past_kr, w_uk):
    B, past, _ = past_ckv.shape
    H = MLA_HEADS
    lq = q.shape[2] // B
    kern = functools.partial(_attn_sample_kernel, past=past, lq=lq)
    return pl.pallas_call(
        kern, grid=(B,),
        in_specs=[pl.BlockSpec((1, H, lq, QK_HEAD), lambda b: (0, 0, b, 0)),
                  pl.BlockSpec((1, H, lq, QK_HEAD), lambda b: (0, 0, b, 0)),
                  pl.BlockSpec((1, lq, KV_LORA), lambda b: (0, b, 0)),
                  pl.BlockSpec((1, past, KV_LORA), lambda b: (b, 0, 0)),
                  pl.BlockSpec((1, past, ROPE_DIM), lambda b: (b, 0, 0)),
                  _const_spec(w_uk.shape)],
        out_specs=pl.BlockSpec((1, H, lq, KV_LORA), lambda b: (0, 0, b, 0)),
        out_shape=jax.ShapeDtypeStruct((1, H, B * lq, KV_LORA), BF16),
        compiler_params=_params(("arbitrary",)), name="attn_sample",
    )(q, kcat, v, past_ckv, past_kr, w_uk)


def _s5_kernel(u_ref, h0_ref, bre_ref, bim_ref, brel_ref, biml_ref, lt_ref, pinr_ref, pini_ref,
               pwr_ref, pwi_ref, a_ref, cre_ref, cim_ref, d_ref, wglu_ref, bglu_ref, gout_ref,
               o_ref, hl_ref, st_scr, cum_scr, hs_scr, *, tm, precise):
    i = pl.program_id(1)
    ns = st_scr.shape[1]
    half = ns // 2
    wh = u_ref.shape[-1] // 2

    @pl.when(i == 0)
    def _():
        st_scr[...] = h0_ref[0]

    u = u_ref[0]
    lt = lt_ref[...]
    for hf in range(2):
        uh = u[:, hf * wh:(hf + 1) * wh]
        if precise:
            u_hi, u_lo = _split(uh)
            bu_re = _dot(u_hi, bre_ref[hf]) + _dot(u_lo, bre_ref[hf]) + _dot(u_hi, brel_ref[hf])
            bu_im = _dot(u_hi, bim_ref[hf]) + _dot(u_lo, bim_ref[hf]) + _dot(u_hi, biml_ref[hf])
        else:
            u_hi = uh.astype(BF16)
            bu_re = _dot(u_hi, bre_ref[hf])
            bu_im = _dot(u_hi, bim_ref[hf])
        sl = slice(hf * half, (hf + 1) * half)
        pr = pinr_ref[:, sl]
        pi = pini_ref[:, sl]
        v_re = pr * bu_re - pi * bu_im
        v_im = pr * bu_im + pi * bu_re
        for part, v in ((0, v_re), (1, v_im)):
            if precise:
                v_hi, v_lo = _split(v)
                c = _dot(lt, v_hi) + _dot(lt, v_lo)
            else:
                c = _dot(lt, v.astype(BF16))
            cum_scr[:, part * ns + hf * half:part * ns + (hf + 1) * half] = c

    a_re = a_ref[0:1, :]
    a_im = a_ref[1:2, :]
    pw_re = pwr_ref[...]
    pw_im = pwi_ref[...]

    def chunk(c, carry):
        s_re, s_im = carry
        r0 = pl.multiple_of(c * S5_SUB, S5_SUB)
        t_re = cum_scr[pl.ds(r0, S5_SUB), 0:ns] + (a_re * s_re - a_im * s_im)
        t_im = cum_scr[pl.ds(r0, S5_SUB), ns:2 * ns] + (a_re * s_im + a_im * s_re)
        h_re = pw_re * t_re - pw_im * t_im
        h_im = pw_re * t_im + pw_im * t_re
        hs_scr[pl.ds(r0, S5_SUB), 0:ns] = h_re.astype(BF16)
        hs_scr[pl.ds(r0, S5_SUB), ns:2 * ns] = h_im.astype(BF16)
        return h_re[S5_SUB - 1:S5_SUB, :], h_im[S5_SUB - 1:S5_SUB, :]

    s_re, s_im = lax.fori_loop(0, tm // S5_SUB, chunk, (st_scr[0:1, :], st_scr[1:2, :]))
    st_scr[0:1, :] = s_re
    st_scr[1:2, :] = s_im
    hl_ref[0, 0:1, :] = s_re
    hl_ref[0, 1:2, :] = s_im

    ys = []
    for hf in range(2):
        hre = hs_scr[:, hf * half:(hf + 1) * half]
        him = hs_scr[:, ns + hf * half:ns + (hf + 1) * half]
        ys.append(_dot(hre, cre_ref[hf]) + _dot(him, cim_ref[hf]))
    y = jnp.concatenate(ys, axis=1) + d_ref[...] * u
    zg = jax.nn.gelu(y)
    gl = _dot(zg.astype(BF16), wglu_ref[...]) + bglu_ref[...]
    o = zg * jax.nn.sigmoid(gl)
    o_ref[0] = _rms(o, gout_ref[...]).astype(BF16)


def _s5(u, h0, tabs, wts, tm, precise):
    B, L, W = u.shape
    ns = h0.shape[-1]
    consts = [tabs["b_re"], tabs["b_im"], tabs["b_re_lo"], tabs["b_im_lo"], tabs["lt"], tabs["pin_re"],
              tabs["pin_im"], tabs["pw_re"], tabs["pw_im"], tabs["a"], tabs["c_re"], tabs["c_im"],
              wts["s5_d"], wts["w_glu"], wts["b_glu"], wts["g_out_s5"]]
    kern = functools.partial(_s5_kernel, tm=tm, precise=precise)
    return pl.pallas_call(
        kern, grid=(B, L // tm),
        in_specs=[pl.BlockSpec((1, tm, W), lambda b, i: (b, i, 0)),
                  pl.BlockSpec((1, 2, ns), lambda b, i: (b, 0, 0))] + [_const_spec(c.shape) for c in consts],
        out_specs=(pl.BlockSpec((1, tm, W), lambda b, i: (b, i, 0)),
                   pl.BlockSpec((1, 2, ns), lambda b, i: (b, 0, 0))),
        out_shape=(jax.ShapeDtypeStruct((B, L, W), BF16), jax.ShapeDtypeStruct((B, 2, ns), F32)),
        scratch_shapes=[pltpu.VMEM((2, ns), F32), pltpu.VMEM((tm, 2 * ns), F32), pltpu.VMEM((tm, 2 * ns), BF16)],
        compiler_params=_params(("arbitrary", "arbitrary")), name="s5",
    )(u, h0, *consts)


def _s5_tables(a_re_p, a_im_p, b_re_p, b_im_p, c_re_p, c_im_p, log_dt, tm):
    G, N = a_re_p.shape
    CH = b_re_p.shape[-1]
    dt = jnp.exp(log_dt.astype(F32))[:, None]
    lr = a_re_p.astype(F32) * dt
    li = a_im_p.astype(F32) * dt
    er = jnp.exp(lr)
    ab_re, ab_im = er * jnp.cos(li), er * jnp.sin(li)
    lam2 = a_re_p.astype(F32) ** 2 + a_im_p.astype(F32) ** 2
    nr, ni = ab_re - 1.0, ab_im
    f_re = (nr * a_re_p + ni * a_im_p) / lam2
    f_im = (ni * a_re_p - nr * a_im_p) / lam2
    bb_re = f_re[..., None] * b_re_p - f_im[..., None] * b_im_p
    bb_im = f_re[..., None] * b_im_p + f_im[..., None] * b_re_p
    gh = G // 2
    eye = jnp.eye(gh, dtype=F32)

    def blk_b(bb):
        t = bb.reshape(2, gh, N, CH)
        m = jnp.einsum("hgnc,gk->hgckn", t, eye)
        return m.reshape(2, gh * CH, gh * N)

    def blk_c(cc):
        t = cc.reshape(2, gh, CH, N)
        m = jnp.einsum("hgcn,gk->hgnkc", t, eye)
        return m.reshape(2, gh * N, gh * CH)

    b_re_m, b_im_m = blk_b(bb_re), blk_b(bb_im)
    b_re_hi, b_re_lo = _split(b_re_m)
    b_im_hi, b_im_lo = _split(b_im_m)
    s = jnp.arange(S5_SUB, dtype=F32)[:, None, None]

    def powers(sign):
        e = jnp.exp(sign * lr[None] * s)
        return ((e * jnp.cos(sign * li[None] * s)).reshape(S5_SUB, G * N),
                (e * jnp.sin(sign * li[None] * s)).reshape(S5_SUB, G * N))

    pin_re, pin_im = powers(-1.0)
    pw_re, pw_im = powers(1.0)
    reps = tm // S5_SUB
    r = np.arange(tm)
    lt = ((r[:, None] // S5_SUB == r[None, :] // S5_SUB) & (r[None, :] <= r[:, None])).astype(np.float32)
    return {
        "b_re": b_re_hi, "b_im": b_im_hi, "b_re_lo": b_re_lo, "b_im_lo": b_im_lo,
        "lt": jnp.asarray(lt, BF16),
        "pin_re": jnp.tile(pin_re, (reps, 1)), "pin_im": jnp.tile(pin_im, (reps, 1)),
        "pw_re": pw_re, "pw_im": pw_im,
        "a": jnp.stack([ab_re.reshape(G * N), ab_im.reshape(G * N)]),
        "c_re": blk_c(c_re_p.astype(F32)).astype(BF16), "c_im": (-blk_c(c_im_p.astype(F32))).astype(BF16),
    }


def _route(scores, sel):
    E, tm = scores.shape
    ge = E // N_EXPERT_GROUPS
    io_g = lax.broadcasted_iota(I32, (ge, tm), 0)
    gs_rows = []
    for g in range(N_EXPERT_GROUPS):
        sg = sel[g * ge:(g + 1) * ge, :]
        m1 = jnp.max(sg, axis=0, keepdims=True)
        i1 = jnp.min(jnp.where(sg == m1, io_g, BIG_I32), axis=0, keepdims=True)
        m2 = jnp.max(jnp.where(io_g == i1, NEG_INF, sg), axis=0, keepdims=True)
        gs_rows.append(m1 + m2)
    gs = jnp.concatenate(gs_rows, axis=0)
    gio = lax.broadcasted_iota(I32, gs.shape, 0)
    gsel = jnp.zeros(gs.shape, F32)
    for _ in range(TOPK_GROUPS):
        mx = jnp.max(gs, axis=0, keepdims=True)
        ix = jnp.min(jnp.where(gs == mx, gio, BIG_I32), axis=0, keepdims=True)
        hit = gio == ix
        gsel = jnp.where(hit, 1.0, gsel)
        gs = jnp.where(hit, NEG_INF, gs)
    emask = jnp.concatenate([jnp.broadcast_to(gsel[g:g + 1, :], (ge, tm)) for g in range(N_EXPERT_GROUPS)], axis=0)
    cand = jnp.where(emask > 0.0, sel, NEG_INF)
    eio = lax.broadcasted_iota(I32, (E, tm), 0)
    idxs, gates = [], []
    for _ in range(TOP_K):
        mx = jnp.max(cand, axis=0, keepdims=True)
        ix = jnp.min(jnp.where(cand == mx, eio, BIG_I32), axis=0, keepdims=True)
        hit = eio == ix
        gates.append(jnp.sum(jnp.where(hit, scores, 0.0), axis=0, keepdims=True))
        idxs.append(ix)
        cand = jnp.where(hit, NEG_INF, cand)
    eidx = jnp.concatenate(idxs, axis=0)
    gate = jnp.concatenate(gates, axis=0)
    gate = gate / jnp.sum(gate, axis=0, keepdims=True) * ROUTED_SCALE
    return eidx, gate


def _post_kernel(x_ref, ol_ref, os_ref, gtm_ref, shf_ref, scf_ref, gtf_ref, wuv_ref, goa_ref, wout_ref,
                 gffn_ref, wrh_ref, wrl_ref, br_ref, wsg_ref, wsu_ref, wsd_ref,
                 h2_ref, base_ref, eidx_ref, gate_ref):
    x = x_ref[0]
    oa = jnp.concatenate([_dot(ol_ref[0, hd], wuv_ref[hd]) for hd in range(MLA_HEADS)], axis=1)
    oan = _rms(oa, goa_ref[...]).astype(BF16)
    wa = oan.shape[1]
    mix = _dot(oan, wout_ref[:wa, :]) + _dot(os_ref[0], wout_ref[wa:, :])
    x1 = x + gtm_ref[0] * mix
    h2 = _rms(x1, gffn_ref[...]) * (1.0 + scf_ref[0]) + shf_ref[0]
    h2_hi, h2_lo = _split(h2)
    for j, plane in enumerate(_pack_rows(h2)):
        h2_ref[j] = plane
    sh =_dot((_silu(_dot(h2_hi, wsg_ref[...])) * _dot(h2_hi, wsu_ref[...])).astype(BF16), wsd_ref[...])
    base_ref[0] = x1 + gtf_ref[0] * sh
    wr_hi = wrh_ref[...]
    logits = _dot_nt(wr_hi, h2_hi) + _dot_nt(wrl_ref[...], h2_hi) + _dot_nt(wr_hi, h2_lo)
    scores = jax.nn.sigmoid(logits)
    eidx, gate = _route(scores, scores + br_ref[...])
    eidx_ref[...] = eidx
    gate_ref[...] = gate


def _post(x, o_lat, o_s5, mods, wts, tm):
    B, L, D = x.shape
    nt = L // tm
    per_row = mods[0].shape[1] != 1
    mod_spec = (pl.BlockSpec((1, tm, D), lambda b, i: (b, i, 0)) if per_row
                else pl.BlockSpec((1, 1, D), lambda b, i: (b, 0, 0)))
    W = o_s5.shape[-1]
    consts = [wts["w_uv"], wts["g_out_attn"], wts["w_out"], wts["g_ffn"], wts["wr_hi"], wts["wr_lo"],
              wts["b_router"], wts["w_sg"], wts["w_su"], wts["w_sd"]]
    in_specs = [pl.BlockSpec((1, tm, D), lambda b, i: (b, i, 0)),
                pl.BlockSpec((1, MLA_HEADS, tm, KV_LORA), lambda b, i: (b, 0, i, 0)),
                pl.BlockSpec((1, tm, W), lambda b, i: (b, i, 0)),
                mod_spec, mod_spec, mod_spec, mod_spec] + [_const_spec(c.shape) for c in consts]
    planes = D // (2 * LANES)
    out_shape = (jax.ShapeDtypeStruct((planes, B * L, LANES), U32), jax.ShapeDtypeStruct((B, L, D), F32),
                 jax.ShapeDtypeStruct((TOP_K, B * L), I32), jax.ShapeDtypeStruct((TOP_K, B * L), F32))
    out_specs = (pl.BlockSpec((planes, tm, LANES), lambda b, i: (0, b * nt + i, 0)),
                 pl.BlockSpec((1, tm, D), lambda b, i: (b, i, 0)),
                 pl.BlockSpec((TOP_K, tm), lambda b, i: (0, b * nt + i)),
                 pl.BlockSpec((TOP_K, tm), lambda b, i: (0, b * nt + i)))
    return pl.pallas_call(
        _post_kernel, grid=(B, nt), in_specs=in_specs, out_specs=out_specs, out_shape=out_shape,
        compiler_params=_params(("arbitrary", "arbitrary")), name="post",
    )(x, o_lat, o_s5, *mods, *consts)


def _rank_kernel(eidx_ref, tri_ref, rank_ref, cnt_ref, carry_scr):
    i = pl.program_id(0)

    @pl.when(i == 0)
    def _():
        carry_scr[...] = jnp.zeros(carry_scr.shape, F32)

    eidx = eidx_ref[...]
    tt = eidx.shape[1]
    eio = lax.broadcasted_iota(I32, (N_EXPERTS, tt), 0)
    hits = [eio == eidx[k:k + 1, :] for k in range(TOP_K)]
    onehot = jnp.zeros((N_EXPERTS, tt), F32)
    for hit in hits:
        onehot = jnp.where(hit, 1.0, onehot)
    before = _dot(onehot.astype(BF16), tri_ref[...]) + carry_scr[...]
    ranks = [jnp.sum(jnp.where(hit, before, 0.0), axis=0, keepdims=True) for hit in hits]
    rank_ref[...] = jnp.concatenate(ranks, axis=0).astype(I32)
    carry = carry_scr[...] + jnp.sum(onehot, axis=1, keepdims=True)
    carry_scr[...] = carry
    cnt_ref[...] = carry


def _rank(eidx, tt):
    K, T = eidx.shape
    r = np.arange(tt)
    tri = jnp.asarray((r[:, None] < r[None, :]).astype(np.float32), BF16)
    return pl.pallas_call(
        _rank_kernel, grid=(T // tt,),
        in_specs=[pl.BlockSpec((K, tt), lambda i: (0, i)), _const_spec((tt, tt))],
        out_specs=(pl.BlockSpec((K, tt), lambda i: (0, i)), _const_spec((N_EXPERTS, 1))),
        out_shape=(jax.ShapeDtypeStruct((K, T), I32), jax.ShapeDtypeStruct((N_EXPERTS, 1), F32)),
        scratch_shapes=[pltpu.VMEM((N_EXPERTS, 1), F32)],
        compiler_params=_params(("arbitrary",)), name="rank",
    )(eidx, tri)


def _pos_kernel(eidx_ref, rank_ref, pstart_ref, idx_ref, *, rows_pad):
    eidx = eidx_ref[...]
    tt = eidx.shape[1]
    eio = lax.broadcasted_iota(I32, (N_EXPERTS, tt), 0)
    pstart = pstart_ref[...]
    starts = [jnp.sum(jnp.where(eio == eidx[k:k + 1, :], pstart, 0.0), axis=0, keepdims=True)
              for k in range(TOP_K)]
    pos = jnp.concatenate(starts, axis=0).astype(I32) + rank_ref[...]
    for j in range(idx_ref.shape[0]):
        idx_ref[j] = pos + j * rows_pad


def _pair_rows(eidx, rank, pstart, planes, rows_pad, tt):
    K, T = eidx.shape
    return pl.pallas_call(
        functools.partial(_pos_kernel, rows_pad=rows_pad), grid=(T // tt,),
        in_specs=[pl.BlockSpec((K, tt), lambda i: (0, i)), pl.BlockSpec((K, tt), lambda i: (0, i)),
                  _const_spec((N_EXPERTS, 1))],
        out_specs=pl.BlockSpec((planes, K, tt), lambda i: (0, 0, i)),
        out_shape=jax.ShapeDtypeStruct((planes, K, T), I32),
        compiler_params=_params(("arbitrary",)), name="pair_rows",
    )(eidx, rank, pstart)


def _expert_kernel(blk_e_ref, nvalid_ref, nused_ref, x_ref, wg_ref, wu_ref, wd_ref, o_ref):
    b = pl.program_id(0)

    @pl.when(b < nused_ref[0])
    def _():
        planes = x_ref.shape[0]
        x = _unpack_rows([x_ref[j] for j in range(planes)])
        live = lax.broadcasted_iota(I32, x.shape, 0) < nvalid_ref[b]
        x = jnp.where(live, x, jnp.zeros_like(x))
        g = _dot(x, wg_ref[0].astype(BF16))
        u = _dot(x, wu_ref[0].astype(BF16))
        y = _dot((_silu(g) * u).astype(BF16), wd_ref[0].astype(BF16))
        for j, plane in enumerate(_pack_rows(y)):
            o_ref[j] = plane


def _experts(xs, blk_e, nvalid, nused, wg, wu, wd):
    planes, rows, _ = xs.shape
    nb = rows // MOE_ROWS
    _, D, F = wg.shape

    def xmap(b, blk_e_ref, nvalid_ref, nused_ref):
        return (0, jnp.minimum(b, nused_ref[0] - 1), 0)

    def wmap(b, blk_e_ref, nvalid_ref, nused_ref):
        return (blk_e_ref[b], 0, 0)

    grid_spec = pltpu.PrefetchScalarGridSpec(
        num_scalar_prefetch=3, grid=(nb,),
        in_specs=[pl.BlockSpec((planes, MOE_ROWS, LANES), xmap),
                  pl.BlockSpec((1, D, F), wmap), pl.BlockSpec((1, D, F), wmap), pl.BlockSpec((1, F, D), wmap)],
        out_specs=pl.BlockSpec((planes, MOE_ROWS, LANES), xmap))
    return pl.pallas_call(
        _expert_kernel, grid_spec=grid_spec, out_shape=jax.ShapeDtypeStruct(xs.shape, U32),
        compiler_params=_params(("arbitrary",)), name="experts",
    )(blk_e, nvalid, nused, xs, wg, wu, wd)


def _final_kernel(base_ref, ys_ref, gate_ref, gtf_ref, gfin_ref, o_ref):
    planes = ys_ref.shape[0]
    gate = gate_ref[...]
    routed = jnp.zeros(base_ref.shape[1:], F32)
    for k in range(TOP_K):
        rows = _unpack_rows([ys_ref[j, k] for j in range(planes)])
        routed = routed + gate[:, k:k + 1] * rows.astype(F32)
    y = base_ref[0] + gtf_ref[0] * routed
    o_ref[0] = _rms(y, gfin_ref[...])


def _final(base, ysg, gate_t, gt_f, g_final, tm, row0):
    B, L, D = base.shape
    planes = ysg.shape[0]
    nt = L // tm
    off = row0 // tm
    per_row = gt_f.shape[1] != 1
    mod_spec = (pl.BlockSpec((1, tm, D), lambda b, i: (b, i, 0)) if per_row
                else pl.BlockSpec((1, 1, D), lambda b, i: (b, 0, 0)))
    return pl.pallas_call(
        _final_kernel, grid=(B, nt),
        in_specs=[pl.BlockSpec((1, tm, D), lambda b, i: (b, i, 0)),
                  pl.BlockSpec((planes, TOP_K, tm, LANES), lambda b, i: (0, 0, off + b * nt + i, 0)),
                  pl.BlockSpec((tm, TOP_K), lambda b, i: (off + b * nt + i, 0)),
                  mod_spec, _const_spec((1, D))],
        out_specs=pl.BlockSpec((1, tm, D), lambda b, i: (b, i, 0)),
        out_shape=jax.ShapeDtypeStruct((B, L, D), F32),
        compiler_params=_params(("arbitrary", "arbitrary")), name="final",
    )(base, ysg, gate_t, gt_f, g_final)


def _sc_mesh():
    return plsc.VectorSubcoreMesh(core_axis_name="c", subcore_axis_name="s")


def _sc_worker():
    return lax.axis_index("s") * SC_CORES + lax.axis_index("c")


def _sc_dispatch(src, idx, out_rows, planes, n_tok):
    n_win = n_tok // SC_WINDOW
    n_iter = -(-n_win // SC_WORKERS)

    @functools.partial(
        pl.kernel, mesh=_sc_mesh(), out_type=jax.ShapeDtypeStruct((out_rows, LANES), src.dtype),
        scratch_types=[pltpu.VMEM((SC_WINDOW,), I32), pltpu.VMEM((SC_WINDOW, LANES), src.dtype),
                       pltpu.SemaphoreType.DMA],
        name="sc_dispatch")
    def k(src_hbm, idx_hbm, out_hbm, idx_v, rows_v, sem):
        wid = _sc_worker()

        @pl.loop(0, n_iter)
        def _(i):
            win = i * SC_WORKERS + wid

            @pl.when(win < n_win)
            def _():
                t0 = win * SC_WINDOW
                for j in range(planes):
                    pltpu.sync_copy(src_hbm.at[pl.ds(pl.multiple_of(j * n_tok + t0, SC_WINDOW), SC_WINDOW)], rows_v)
                    for kk in range(TOP_K):
                        off = pl.multiple_of((j * TOP_K + kk) * n_tok + t0, SC_WINDOW)
                        pltpu.sync_copy(idx_hbm.at[pl.ds(off, SC_WINDOW)], idx_v)
                        pltpu.async_copy(rows_v, out_hbm.at[idx_v], sem).wait()

    return k(src, idx)


def _sc_gather(table, idx):
    n_idx = idx.shape[0]
    n_iter = n_idx // (SC_WINDOW * SC_WORKERS)

    @functools.partial(
        pl.kernel, mesh=_sc_mesh(), out_type=jax.ShapeDtypeStruct((n_idx, LANES), table.dtype),
        scratch_types=[pltpu.VMEM((SC_WINDOW,), I32), pltpu.VMEM((SC_WINDOW, LANES), table.dtype),
                       pltpu.SemaphoreType.DMA],
        name="sc_gather")
    def k(table_hbm, idx_hbm, out_hbm, idx_v, rows_v, sem):
        wid = _sc_worker()

        @pl.loop(0, n_iter)
        def _(i):
            base = pl.multiple_of((i * SC_WORKERS + wid) * SC_WINDOW, SC_WINDOW)
            pltpu.sync_copy(idx_hbm.at[pl.ds(base, SC_WINDOW)], idx_v)
            pltpu.async_copy(table_hbm.at[idx_v], rows_v, sem).wait()
            pltpu.sync_copy(rows_v, out_hbm.at[pl.ds(base, SC_WINDOW)])

    return k(table, idx)


def _rope_tables(pos):
    half = ROPE_DIM // 2
    inv = ROPE_THETA ** (-jnp.arange(half, dtype=F32) / half)
    ang = pos.astype(F32)[:, None] * inv[None, :]
    cos = jnp.tile(jnp.cos(ang), (1, 2 * MLA_HEADS))
    sin = jnp.tile(jnp.sin(ang), (1, 2 * MLA_HEADS))
    return cos, sin


def _rot_cols(w):
    half = ROPE_DIM // 2
    return jnp.concatenate([-w[..., half:], w[..., :half]], axis=-1)


def _layer_weights(w_in, g_mix, g_q, g_kv, w_uq, w_uk, w_uv, s5_D, s5_w_glu, s5_b_glu, g_out_attn, g_out_s5,
                   w_out, g_ffn, w_router, b_router, w_sh_gate, w_sh_up, w_sh_down):
    D = w_in.shape[0]
    o1, o2, o3 = Q_LORA, Q_LORA + KV_LORA, Q_LORA + KV_LORA + ROPE_DIM
    s5w = w_in.shape[1] - o3
    w_rope = w_in[:, o2:o3]
    w_in_ext = jnp.concatenate([w_in[:, :o2], w_in[:, o3:], w_rope, _rot_cols(w_rope)], axis=1).astype(BF16)
    wq = w_uq.reshape(Q_LORA, MLA_HEADS, NOPE_DIM + ROPE_DIM)
    wq_rope = wq[:, :, NOPE_DIM:]
    wr_hi, wr_lo = _split(w_router.T)
    return {
        "s5w": s5w,
        "g_mix": g_mix.reshape(1, D), "w_in": w_in_ext,
        "g_q": g_q.reshape(1, Q_LORA), "g_kv": g_kv.reshape(1, KV_LORA),
        "wq_nope": wq[:, :, :NOPE_DIM].reshape(Q_LORA, MLA_HEADS * NOPE_DIM).astype(BF16),
        "wq_rope": wq_rope.reshape(Q_LORA, MLA_HEADS * ROPE_DIM).astype(BF16),
        "wq_rot": _rot_cols(wq_rope).reshape(Q_LORA, MLA_HEADS * ROPE_DIM).astype(BF16),
        "w_uk": w_uk.reshape(KV_LORA, MLA_HEADS * NOPE_DIM).astype(BF16),
        "w_uv": jnp.transpose(w_uv, (1, 0, 2)).astype(BF16),
        "s5_d": s5_D.reshape(1, s5w), "w_glu": s5_w_glu.astype(BF16), "b_glu": s5_b_glu.reshape(1, s5w),
        "g_out_attn": g_out_attn.reshape(1, -1), "g_out_s5": g_out_s5.reshape(1, s5w),
        "w_out": w_out.astype(BF16), "g_ffn": g_ffn.reshape(1, D),
        "wr_hi": wr_hi, "wr_lo": wr_lo, "b_router": b_router.reshape(N_EXPERTS, 1),
        "w_sg": w_sh_gate.astype(BF16), "w_su": w_sh_up.astype(BF16), "w_sd": w_sh_down.astype(BF16),
    }


def _state_in(s_re, s_im):
    B = s_re.shape[0]
    return jnp.stack([s_re.reshape(B, -1), s_im.reshape(B, -1)], axis=1).astype(F32)


def kernel(x_prompt, x_sample, c_prompt, c_sample, cache_ckv, cache_krope, state_s5_re, state_s5_im, w_ada, b_ada, g_mix, g_ffn, w_in, g_q, w_uq, g_kv, w_uk, w_uv, s5_A_re, s5_A_im, s5_B_re, s5_B_im, s5_C_re, s5_C_im, s5_D, s5_log_dt, s5_w_glu, s5_b_glu, g_out_attn, g_out_s5, w_out, w_router, b_router, w_exp_gate, w_exp_up, w_exp_down, w_sh_gate, w_sh_up, w_sh_down, g_final):
    Bp, Lp, D = x_prompt.shape
    Bs, Ls, _ = x_sample.shape
    depth = w_ada.shape[0]
    assert depth == 1, "single-layer step"
    past = cache_ckv.shape[2]
    G, N = s5_A_re.shape[1:]
    Ts = Bs * Ls
    Tp = Bp * Lp
    l = 0

    wts = _layer_weights(w_in[l], g_mix[l], g_q[l], g_kv[l], w_uq[l], w_uk[l], w_uv[l], s5_D[l], s5_w_glu[l],
                         s5_b_glu[l], g_out_attn[l], g_out_s5[l], w_out[l], g_ffn[l], w_router[l], b_router[l],
                         w_sh_gate[l], w_sh_up[l], w_sh_down[l])

    mod = _ada(jnp.concatenate([c_prompt, c_sample], axis=0), w_ada[l], b_ada[l])
    mod_p = [m.reshape(Bp, 1, D) for m in jnp.split(mod[:Bp], 6, axis=-1)]
    mod_s = [jnp.broadcast_to(m[:, None, :], (Bs, Ls, D)).reshape(1, Ts, D)
             for m in jnp.split(mod[Bp:], 6, axis=-1)]

    tm_p = 256
    cos_p, sin_p = _rope_tables(jnp.arange(Lp))
    cos_s, sin_s = _rope_tables(jnp.tile(past + jnp.arange(Ls), Bs))

    ckv_p, kr_p, u_p, kcat_p, v_p, q_p = _pre(x_prompt, mod_p[0], mod_p[1], cos_p, sin_p, wts, tm_p)
    olat_p = _attn_prompt(q_p, kcat_p, v_p, 512)
    tabs_p = _s5_tables(s5_A_re[l], s5_A_im[l], s5_B_re[l], s5_B_im[l], s5_C_re[l], s5_C_im[l], s5_log_dt[l], tm_p)
    os5_p, hl_p = _s5(u_p, jnp.zeros((Bp, 2, G * N), F32), tabs_p, wts, tm_p, precise=False)
    h2_p, base_p, eidx_p, gate_p = _post(x_prompt, olat_p, os5_p, [mod_p[2], mod_p[3], mod_p[4], mod_p[5]], wts, tm_p)

    xs_rows = x_sample.reshape(1, Ts, D)
    ckv_s, kr_s, u_s, kcat_s, v_s, q_s = _pre(xs_rows, mod_s[0], mod_s[1], cos_s, sin_s, wts, Ts)
    olat_s = _attn_sample(q_s, kcat_s, v_s, cache_ckv[l], cache_krope[l], wts["w_uk"])
    tabs_s = _s5_tables(s5_A_re[l], s5_A_im[l], s5_B_re[l], s5_B_im[l], s5_C_re[l], s5_C_im[l], s5_log_dt[l], Ls)
    os5_s, hl_s = _s5(u_s.reshape(Bs, Ls, -1), _state_in(state_s5_re[l], state_s5_im[l]), tabs_s, wts, Ls,
                      precise=True)
    h2_s, base_s, eidx_s, gate_s = _post(xs_rows, olat_s, os5_s.reshape(1, Ts, -1),
                                         [mod_s[2], mod_s[3], mod_s[4], mod_s[5]], wts, Ts)

    T_all = Tp + Ts
    eidx = jnp.concatenate([eidx_p, eidx_s], axis=1)
    gate = jnp.concatenate([gate_p, gate_s], axis=1)
    rank, counts = _rank(eidx, 256)
    counts = counts.reshape(N_EXPERTS).astype(I32)
    nblk = (counts + MOE_ROWS - 1) // MOE_ROWS
    blk_end = jnp.cumsum(nblk)
    pstart = (blk_end - nblk) * MOE_ROWS
    nb_max = (T_all * TOP_K) // MOE_ROWS + N_EXPERTS
    nused = blk_end[-1]
    blk_e = jnp.minimum(jnp.searchsorted(blk_end, jnp.arange(nb_max, dtype=I32), side="right"),
                        N_EXPERTS - 1).astype(I32)
    bidx = jnp.arange(nb_max, dtype=I32)
    blk_e = jnp.where(bidx < nused, blk_e, blk_e[jnp.maximum(nused - 1, 0)])
    nvalid = jnp.clip(counts[blk_e] - (bidx - (blk_end - nblk)[blk_e]) * MOE_ROWS, 0, MOE_ROWS).astype(I32)
    rows_pad = nb_max * MOE_ROWS
    planes = h2_p.shape[0]
    idx = _pair_rows(eidx, rank, pstart.astype(F32).reshape(N_EXPERTS, 1), planes, rows_pad, 256).reshape(-1)
    h2_all = jnp.concatenate([h2_p, h2_s], axis=1).reshape(planes * T_all, LANES)
    xs_sorted = _sc_dispatch(h2_all, idx, planes * rows_pad, planes, T_all).reshape(planes, rows_pad, LANES)
    ys = _experts(xs_sorted, blk_e, nvalid, nused.reshape(1).astype(I32), w_exp_gate[l], w_exp_up[l], w_exp_down[l])
    ysg = _sc_gather(ys.reshape(planes * rows_pad, LANES), idx).reshape(planes, TOP_K, T_all, LANES)
    gate_t = gate.T

    gfin = g_final.reshape(1, D)
    y_p = _final(base_p, ysg, gate_t, mod_p[5], gfin, tm_p, 0)
    y_s = _final(base_s, ysg, gate_t, mod_s[5], gfin, Ts, Tp).reshape(Bs, Ls, D)

    def state_out(hl, B):
        return hl[:, 0].reshape(1, B, G, N), hl[:, 1].reshape(1, B, G, N)

    sre_p, sim_p = state_out(hl_p, Bp)
    sre_s, sim_s = state_out(hl_s, Bs)
    return (y_p, y_s, ckv_p[None], kr_p[None], sre_p, sim_p,
            ckv_s.reshape(1, Bs, Ls, KV_LORA), kr_s.reshape(1, Bs, Ls, ROPE_DIM), sre_s, sim_s)
```

```python
import functools
import math

import numpy as np
import jax
import jax.numpy as jnp
from jax import lax
from jax.experimental import pallas as pl
from jax.experimental.pallas import tpu as pltpu
from jax.experimental.pallas import tpu_sc as plsc

F32 = jnp.float32
BF16 = jnp.bfloat16
I32 = jnp.int32
U32 = jnp.uint32

EPS = 1e-6
CHUNK = 64
MLA_HEADS = 4
NOPE_DIM = 128
ROPE_DIM = 64
V_DIM = 128
Q_LORA = 256
KV_LORA = 256
QK_HEAD = NOPE_DIM + ROPE_DIM
ROPE_THETA = 10000.0
S5_GROUP_CH = 16
S5_STATE = 64
N_EXPERTS = 256
TOP_K = 8
N_EXPERT_GROUPS = 8
TOPK_GROUPS = 4
ROUTED_SCALE = 2.5

S5_SUB = 16
MOE_ROWS = 512
LANES = 128
VMEM_LIMIT = 56 * 1024 * 1024
SC_CORES = 2
SC_SUBCORES = 16
SC_WORKERS = SC_CORES * SC_SUBCORES
SC_WINDOW = 128

NEG_INF = float("-inf")
BIG_I32 = 1 << 30


def _dot(a, b):
    return jnp.dot(a, b, preferred_element_type=F32)


def _dot_nt(a, b):
    return lax.dot_general(a, b, (((1,), (1,)), ((), ())), preferred_element_type=F32)


def _split(a):
    hi = a.astype(BF16)
    lo = (a - hi.astype(F32)).astype(BF16)
    return hi, lo


def _lane_tile(x, n):
    return jnp.concatenate([x] * n, axis=1)


def _rms(x, g):
    return x * lax.rsqrt(jnp.mean(x * x, axis=-1, keepdims=True) + EPS) * g


def _silu(x):
    return x * jax.nn.sigmoid(x)


def _pack_rows(x):
    half = x.shape[1] // 2
    hi = lax.bitcast_convert_type(x[:, :half].astype(BF16).astype(F32), U32)
    lo = lax.bitcast_convert_type(x[:, half:].astype(BF16).astype(F32), U32)
    w = hi | (lo >> 16)
    return [w[:, j * LANES:(j + 1) * LANES] for j in range(half // LANES)]


def _unpack_rows(planes):
    his = [lax.bitcast_convert_type(p & jnp.uint32(0xFFFF0000), F32).astype(BF16) for p in planes]
    los = [lax.bitcast_convert_type(p << 16, F32).astype(BF16) for p in planes]
    return jnp.concatenate(his + los, axis=1)


def _params(sem):
    return pltpu.CompilerParams(dimension_semantics=sem, vmem_limit_bytes=VMEM_LIMIT)


def _const_spec(shape):
    nd = len(shape)
    return pl.BlockSpec(shape, lambda *_: (0,) * nd)


def _ada_kernel(c_ref, whi_ref, wlo_ref, b_ref, o_ref):
    c = c_ref[...]
    s_hi, s_lo = _split(_silu(c))
    w_hi = whi_ref[...]
    o_ref[...] = _dot(s_hi, w_hi) + _dot(s_hi, wlo_ref[...]) + _dot(s_lo, w_hi) + b_ref[...]


def _ada(c, w_ada, b_ada):
    rows, d = c.shape
    n = w_ada.shape[1]
    tn = 512
    w_hi, w_lo = _split(w_ada)
    return pl.pallas_call(
        _ada_kernel,
        grid=(n // tn,),
        in_specs=[_const_spec((rows, d)),
                  pl.BlockSpec((d, tn), lambda j: (0, j)),
                  pl.BlockSpec((d, tn), lambda j: (0, j)),
                  pl.BlockSpec((1, tn), lambda j: (0, j))],
        out_specs=pl.BlockSpec((rows, tn), lambda j: (0, j)),
        out_shape=jax.ShapeDtypeStruct((rows, n), F32),
        compiler_params=_params(("arbitrary",)),
        name="ada",
    )(c, w_hi, w_lo, b_ada.reshape(1, n))


def _pre_kernel(x_ref, sh_ref, sc_ref, g_ref, win_ref, gq_ref, gkv_ref, wqn_ref, wqr_ref, wqt_ref,
                wuk_ref, cos_ref, sin_ref, ckv_ref, kr_ref, u_ref, kcat_ref, v_ref, q_ref, *, scale):
    x = x_ref[0]
    h = _rms(x, g_ref[...]) * (1.0 + sc_ref[0]) + sh_ref[0]
    z = _dot(h.astype(BF16), win_ref[...])
    cq = _rms(z[:, :Q_LORA], gq_ref[...])
    ckv = _rms(z[:, Q_LORA:Q_LORA + KV_LORA], gkv_ref[...])
    o_s5 = Q_LORA + KV_LORA
    s5w = u_ref.shape[-1]
    u_ref[0] = z[:, o_s5:o_s5 + s5w]
    o_r = o_s5 + s5w
    cos = cos_ref[...]
    sin = sin_ref[...]
    kr = z[:, o_r:o_r + ROPE_DIM] * cos[:, :ROPE_DIM] + z[:, o_r + ROPE_DIM:o_r + 2 * ROPE_DIM] * sin[:, :ROPE_DIM]
    ckv_ref[0] = ckv
    kr_ref[0] = kr
    ckvb = ckv.astype(BF16)
    krb = kr.astype(BF16)
    v_ref[0] = ckvb
    kn = _dot(ckvb, wuk_ref[...])
    cqb = cq.astype(BF16)
    qn = _dot(cqb, wqn_ref[...]) * scale
    qr = (_dot(cqb, wqr_ref[...]) * cos + _dot(cqb, wqt_ref[...]) * sin) * scale
    for hd in range(MLA_HEADS):
        kcat_ref[0, hd, :, :NOPE_DIM] = kn[:, hd * NOPE_DIM:(hd + 1) * NOPE_DIM].astype(BF16)
        kcat_ref[0, hd, :, NOPE_DIM:] = krb
        q_ref[0, hd, :, :NOPE_DIM] = qn[:, hd * NOPE_DIM:(hd + 1) * NOPE_DIM].astype(BF16)
        q_ref[0, hd, :, NOPE_DIM:] = qr[:, hd * ROPE_DIM:(hd + 1) * ROPE_DIM].astype(BF16)


def _pre(x, shift, scale_mod, cos_t, sin_t, wts, tm):
    B, L, D = x.shape
    nt = L // tm
    per_row = shift.shape[1] != 1
    mod_spec = (pl.BlockSpec((1, tm, D), lambda b, i: (b, i, 0)) if per_row
                else pl.BlockSpec((1, 1, D), lambda b, i: (b, 0, 0)))
    s5w = wts["s5w"]
    hr = MLA_HEADS * ROPE_DIM
    kern = functools.partial(_pre_kernel, scale=QK_HEAD ** -0.5 * math.log2(math.e))
    consts = [wts["g_mix"], wts["w_in"], wts["g_q"], wts["g_kv"], wts["wq_nope"], wts["wq_rope"],
              wts["wq_rot"], wts["w_uk"]]
    in_specs = [pl.BlockSpec((1, tm, D), lambda b, i: (b, i, 0)), mod_spec, mod_spec]
    in_specs += [_const_spec(c.shape) for c in consts]
    in_specs += [pl.BlockSpec((tm, hr), lambda b, i: (i, 0)), pl.BlockSpec((tm, hr), lambda b, i: (i, 0))]
    out_shape = (jax.ShapeDtypeStruct((B, L, KV_LORA), F32),
                 jax.ShapeDtypeStruct((B, L, ROPE_DIM), F32),
                 jax.ShapeDtypeStruct((B, L, s5w), F32),
                 jax.ShapeDtypeStruct((B, MLA_HEADS, L, QK_HEAD), BF16),
                 jax.ShapeDtypeStruct((B, L, KV_LORA), BF16),
                 jax.ShapeDtypeStruct((B, MLA_HEADS, L, QK_HEAD), BF16))
    out_specs = (pl.BlockSpec((1, tm, KV_LORA), lambda b, i: (b, i, 0)),
                 pl.BlockSpec((1, tm, ROPE_DIM), lambda b, i: (b, i, 0)),
                 pl.BlockSpec((1, tm, s5w), lambda b, i: (b, i, 0)),
                 pl.BlockSpec((1, MLA_HEADS, tm, QK_HEAD), lambda b, i: (b, 0, i, 0)),
                 pl.BlockSpec((1, tm, KV_LORA), lambda b, i: (b, i, 0)),
                 pl.BlockSpec((1, MLA_HEADS, tm, QK_HEAD), lambda b, i: (b, 0, i, 0)))
    return pl.pallas_call(
        kern, grid=(B, nt), in_specs=in_specs, out_specs=out_specs, out_shape=out_shape,
        compiler_params=_params(("arbitrary", "arbitrary")), name="pre",
    )(x, shift, scale_mod, *consts, cos_t, sin_t)


def _attn_kernel(q_ref, k_ref, v_ref, o_ref, m_scr, l_scr, acc_scr, *, t):
    i = pl.program_id(1)
    m_scr[...] = jnp.full(m_scr.shape, NEG_INF, F32)
    l_scr[...] = jnp.zeros(l_scr.shape, F32)
    acc_scr[...] = jnp.zeros(acc_scr.shape, F32)
    visible = (lax.broadcasted_iota(I32, (t, t), 1) // CHUNK) <= (lax.broadcasted_iota(I32, (t, t), 0) // CHUNK)

    def step(j0, masked):
        v = v_ref[0, pl.ds(j0, t), :]
        for hd in range(MLA_HEADS):
            rs = slice(hd * t, (hd + 1) * t)
            s = _dot_nt(q_ref[0, hd], k_ref[0, hd, pl.ds(j0, t), :])
            if masked:
                s = jnp.where(visible, s, NEG_INF)
            m_prev = m_scr[rs]
            m_next = jnp.maximum(m_prev, jnp.max(s, axis=1, keepdims=True))
            alpha = jnp.exp2(m_prev - m_next)
            p = jnp.exp2(s - _lane_tile(m_next, t // LANES))
            l_scr[rs] = alpha * l_scr[rs] + jnp.sum(p, axis=1, keepdims=True)
            m_scr[rs] = m_next
            acc_scr[rs] = acc_scr[rs] * _lane_tile(alpha, KV_LORA // LANES) + _dot(p.astype(BF16), v)

    def body(j, carry):
        step(pl.multiple_of(j * t, t), False)
        return carry

    lax.fori_loop(0, i, body, 0)
    step(pl.multiple_of(i * t, t), True)
    for hd in range(MLA_HEADS):
        rs = slice(hd * t, (hd + 1) * t)
        inv = 1.0 / l_scr[rs]
        o_ref[0, hd] = (acc_scr[rs] * _lane_tile(inv, KV_LORA // LANES)).astype(BF16)


def _attn_prompt(q, kcat, v, t):
    B, H, L, _ = q.shape
    assert L % t == 0 and t % CHUNK == 0
    rows = H * t
    kern = functools.partial(_attn_kernel, t=t)
    resident = pl.Buffered(1)
    return pl.pallas_call(
        kern, grid=(B, L // t),
        in_specs=[pl.BlockSpec((1, H, t, QK_HEAD), lambda b, i: (b, 0, i, 0)),
                  pl.BlockSpec((1, H, L, QK_HEAD), lambda b, i: (b, 0, 0, 0), pipeline_mode=resident),
                  pl.BlockSpec((1, L, KV_LORA), lambda b, i: (b, 0, 0), pipeline_mode=resident)],
        out_specs=pl.BlockSpec((1, H, t, KV_LORA), lambda b, i: (b, 0, i, 0)),
        out_shape=jax.ShapeDtypeStruct((B, H, L, KV_LORA), BF16),
        scratch_shapes=[pltpu.VMEM((rows, LANES), F32), pltpu.VMEM((rows, LANES), F32),
                        pltpu.VMEM((rows, KV_LORA), F32)],
        compiler_params=_params(("arbitrary", "arbitrary")), name="attn_prompt",
    )(q, kcat, v)


def _attn_sample_kernel(q_ref, kn_ref, vn_ref, pc_ref, pr_ref, wuk_ref, o_ref, *, past, lq):
    pc = pc_ref[0].astype(BF16)
    pr = pr_ref[0].astype(BF16)
    kp = _dot(pc, wuk_ref[...]).astype(BF16)
    vn = vn_ref[0]
    qchunk_p = (past + lax.broadcasted_iota(I32, (lq, past), 0)) // CHUNK
    vis_p = lax.broadcasted_iota(I32, (lq, past), 1) // CHUNK <= qchunk_p
    qchunk_n = (past + lax.broadcasted_iota(I32, (lq, lq), 0)) // CHUNK
    vis_n = (past + lax.broadcasted_iota(I32, (lq, lq), 1)) // CHUNK <= qchunk_n
    for hd in range(MLA_HEADS):
        q = q_ref[0, hd]
        s_p = (_dot_nt(q[:, :NOPE_DIM], kp[:, hd * NOPE_DIM:(hd + 1) * NOPE_DIM])
               + _dot_nt(q[:, NOPE_DIM:], pr))
        s_n = _dot_nt(q, kn_ref[0, hd])
        s_p = jnp.where(vis_p, s_p, NEG_INF)
        s_n = jnp.where(vis_n, s_n, NEG_INF)
        m = jnp.maximum(jnp.max(s_p, axis=1, keepdims=True), jnp.max(s_n, axis=1, keepdims=True))
        p_p = jnp.exp2(s_p - m)
        p_n = jnp.exp2(s_n - m)
        l = jnp.sum(p_p, axis=1, keepdims=True) + jnp.sum(p_n, axis=1, keepdims=True)
        o = _dot(p_p.astype(BF16), pc) + _dot(p_n.astype(BF16), vn)
        o_ref[0, hd] = (o / l).astype(BF16)


def _attn_sample(q, kcat, v, past_ckv, past_kr, w_uk):
    B, past, _ = past_ckv.shape
    H = MLA_HEADS
    lq = q.shape[2] // B
    kern = functools.partial(_attn_sample_kernel, past=past, lq=lq)
    return pl.pallas_call(
        kern, grid=(B,),
        in_specs=[pl.BlockSpec((1, H, lq, QK_HEAD), lambda b: (0, 0, b, 0)),
                  pl.BlockSpec((1, H, lq, QK_HEAD), lambda b: (0, 0, b, 0)),
                  pl.BlockSpec((1, lq, KV_LORA), lambda b: (0, b, 0)),
                  pl.BlockSpec((1, past, KV_LORA), lambda b: (b, 0, 0)),
                  pl.BlockSpec((1, past, ROPE_DIM), lambda b: (b, 0, 0)),
                  _const_spec(w_uk.shape)],
        out_specs=pl.BlockSpec((1, H, lq, KV_LORA), lambda b: (0, 0, b, 0)),
        out_shape=jax.ShapeDtypeStruct((1, H, B * lq, KV_LORA), BF16),
        compiler_params=_params(("arbitrary",)), name="attn_sample",
    )(q, kcat, v, past_ckv, past_kr, w_uk)


def _s5_kernel(u_ref, h0_ref, bre_ref, bim_ref, brel_ref, biml_ref, lt_ref, pinr_ref, pini_ref,
               pwr_ref, pwi_ref, a_ref, cre_ref, cim_ref, d_ref, wglu_ref, bglu_ref, gout_ref,
               o_ref, hl_ref, st_scr, cum_scr, hs_scr, *, tm, precise):
    i = pl.program_id(1)
    ns = st_scr.shape[1]
    half = ns // 2
    wh = u_ref.shape[-1] // 2

    @pl.when(i == 0)
    def _():
        st_scr[...] = h0_ref[0]

    u = u_ref[0]
    lt = lt_ref[...]
    for hf in range(2):
        uh = u[:, hf * wh:(hf + 1) * wh]
        if precise:
            u_hi, u_lo = _split(uh)
            bu_re = _dot(u_hi, bre_ref[hf]) + _dot(u_lo, bre_ref[hf]) + _dot(u_hi, brel_ref[hf])
            bu_im = _dot(u_hi, bim_ref[hf]) + _dot(u_lo, bim_ref[hf]) + _dot(u_hi, biml_ref[hf])
        else:
            u_hi = uh.astype(BF16)
            bu_re = _dot(u_hi, bre_ref[hf])
            bu_im = _dot(u_hi, bim_ref[hf])
        sl = slice(hf * half, (hf + 1) * half)
        pr = pinr_ref[:, sl]
        pi = pini_ref[:, sl]
        v_re = pr * bu_re - pi * bu_im
        v_im = pr * bu_im + pi * bu_re
        for part, v in ((0, v_re), (1, v_im)):
            if precise:
                v_hi, v_lo = _split(v)
                c = _dot(lt, v_hi) + _dot(lt, v_lo)
            else:
                c = _dot(lt, v.astype(BF16))
            cum_scr[:, part * ns + hf * half:part * ns + (hf + 1) * half] = c

    a_re = a_ref[0:1, :]
    a_im = a_ref[1:2, :]
    pw_re = pwr_ref[...]
    pw_im = pwi_ref[...]

    def chunk(c, carry):
        s_re, s_im = carry
        r0 = pl.multiple_of(c * S5_SUB, S5_SUB)
        t_re = cum_scr[pl.ds(r0, S5_SUB), 0:ns] + (a_re * s_re - a_im * s_im)
        t_im = cum_scr[pl.ds(r0, S5_SUB), ns:2 * ns] + (a_re * s_im + a_im * s_re)
        h_re = pw_re * t_re - pw_im * t_im
        h_im = pw_re * t_im + pw_im * t_re
        hs_scr[pl.ds(r0, S5_SUB), 0:ns] = h_re.astype(BF16)
        hs_scr[pl.ds(r0, S5_SUB), ns:2 * ns] = h_im.astype(BF16)
        return h_re[S5_SUB - 1:S5_SUB, :], h_im[S5_SUB - 1:S5_SUB, :]

    s_re, s_im = lax.fori_loop(0, tm // S5_SUB, chunk, (st_scr[0:1, :], st_scr[1:2, :]))
    st_scr[0:1, :] = s_re
    st_scr[1:2, :] = s_im
    hl_ref[0, 0:1, :] = s_re
    hl_ref[0, 1:2, :] = s_im

    ys = []
    for hf in range(2):
        hre = hs_scr[:, hf * half:(hf + 1) * half]
        him = hs_scr[:, ns + hf * half:ns + (hf + 1) * half]
        ys.append(_dot(hre, cre_ref[hf]) + _dot(him, cim_ref[hf]))
    y = jnp.concatenate(ys, axis=1) + d_ref[...] * u
    zg = jax.nn.gelu(y)
    gl = _dot(zg.astype(BF16), wglu_ref[...]) + bglu_ref[...]
    o = zg * jax.nn.sigmoid(gl)
    o_ref[0] = _rms(o, gout_ref[...]).astype(BF16)


def _s5(u, h0, tabs, wts, tm, precise):
    B, L, W = u.shape
    ns = h0.shape[-1]
    consts = [tabs["b_re"], tabs["b_im"], tabs["b_re_lo"], tabs["b_im_lo"], tabs["lt"], tabs["pin_re"],
              tabs["pin_im"], tabs["pw_re"], tabs["pw_im"], tabs["a"], tabs["c_re"], tabs["c_im"],
              wts["s5_d"], wts["w_glu"], wts["b_glu"], wts["g_out_s5"]]
    kern = functools.partial(_s5_kernel, tm=tm, precise=precise)
    return pl.pallas_call(
        kern, grid=(B, L // tm),
        in_specs=[pl.BlockSpec((1, tm, W), lambda b, i: (b, i, 0)),
                  pl.BlockSpec((1, 2, ns), lambda b, i: (b, 0, 0))] + [_const_spec(c.shape) for c in consts],
        out_specs=(pl.BlockSpec((1, tm, W), lambda b, i: (b, i, 0)),
                   pl.BlockSpec((1, 2, ns), lambda b, i: (b, 0, 0))),
        out_shape=(jax.ShapeDtypeStruct((B, L, W), BF16), jax.ShapeDtypeStruct((B, 2, ns), F32)),
        scratch_shapes=[pltpu.VMEM((2, ns), F32), pltpu.VMEM((tm, 2 * ns), F32), pltpu.VMEM((tm, 2 * ns), BF16)],
        compiler_params=_params(("arbitrary", "arbitrary")), name="s5",
    )(u, h0, *consts)


def _s5_tables(a_re_p, a_im_p, b_re_p, b_im_p, c_re_p, c_im_p, log_dt, tm):
    G, N = a_re_p.shape
    CH = b_re_p.shape[-1]
    dt = jnp.exp(log_dt.astype(F32))[:, None]
    lr = a_re_p.astype(F32) * dt
    li = a_im_p.astype(F32) * dt
    er = jnp.exp(lr)
    ab_re, ab_im = er * jnp.cos(li), er * jnp.sin(li)
    lam2 = a_re_p.astype(F32) ** 2 + a_im_p.astype(F32) ** 2
    nr, ni = ab_re - 1.0, ab_im
    f_re = (nr * a_re_p + ni * a_im_p) / lam2
    f_im = (ni * a_re_p - nr * a_im_p) / lam2
    bb_re = f_re[..., None] * b_re_p - f_im[..., None] * b_im_p
    bb_im = f_re[..., None] * b_im_p + f_im[..., None] * b_re_p
    gh = G // 2
    eye = jnp.eye(gh, dtype=F32)

    def blk_b(bb):
        t = bb.reshape(2, gh, N, CH)
        m = jnp.einsum("hgnc,gk->hgckn", t, eye)
        return m.reshape(2, gh * CH, gh * N)

    def blk_c(cc):
        t = cc.reshape(2, gh, CH, N)
        m = jnp.einsum("hgcn,gk->hgnkc", t, eye)
        return m.reshape(2, gh * N, gh * CH)

    b_re_m, b_im_m = blk_b(bb_re), blk_b(bb_im)
    b_re_hi, b_re_lo = _split(b_re_m)
    b_im_hi, b_im_lo = _split(b_im_m)
    s = jnp.arange(S5_SUB, dtype=F32)[:, None, None]

    def powers(sign):
        e = jnp.exp(sign * lr[None] * s)
        return ((e * jnp.cos(sign * li[None] * s)).reshape(S5_SUB, G * N),
                (e * jnp.sin(sign * li[None] * s)).reshape(S5_SUB, G * N))

    pin_re, pin_im = powers(-1.0)
    pw_re, pw_im = powers(1.0)
    reps = tm // S5_SUB
    r = np.arange(tm)
    lt = ((r[:, None] // S5_SUB == r[None, :] // S5_SUB) & (r[None, :] <= r[:, None])).astype(np.float32)
    return {
        "b_re": b_re_hi, "b_im": b_im_hi, "b_re_lo": b_re_lo, "b_im_lo": b_im_lo,
        "lt": jnp.asarray(lt, BF16),
        "pin_re": jnp.tile(pin_re, (reps, 1)), "pin_im": jnp.tile(pin_im, (reps, 1)),
        "pw_re": pw_re, "pw_im": pw_im,
        "a": jnp.stack([ab_re.reshape(G * N), ab_im.reshape(G * N)]),
        "c_re": blk_c(c_re_p.astype(F32)).astype(BF16), "c_im": (-blk_c(c_im_p.astype(F32))).astype(BF16),
    }


def _route(scores, sel):
    E, tm = scores.shape
    ge = E // N_EXPERT_GROUPS
    io_g = lax.broadcasted_iota(I32, (ge, tm), 0)
    gs_rows = []
    for g in range(N_EXPERT_GROUPS):
        sg = sel[g * ge:(g + 1) * ge, :]
        m1 = jnp.max(sg, axis=0, keepdims=True)
        i1 = jnp.min(jnp.where(sg == m1, io_g, BIG_I32), axis=0, keepdims=True)
        m2 = jnp.max(jnp.where(io_g == i1, NEG_INF, sg), axis=0, keepdims=True)
        gs_rows.append(m1 + m2)
    gs = jnp.concatenate(gs_rows, axis=0)
    gio = lax.broadcasted_iota(I32, gs.shape, 0)
    gsel = jnp.zeros(gs.shape, F32)
    for _ in range(TOPK_GROUPS):
        mx = jnp.max(gs, axis=0, keepdims=True)
        ix = jnp.min(jnp.where(gs == mx, gio, BIG_I32), axis=0, keepdims=True)
        hit = gio == ix
        gsel = jnp.where(hit, 1.0, gsel)
        gs = jnp.where(hit, NEG_INF, gs)
    emask = jnp.concatenate([jnp.broadcast_to(gsel[g:g + 1, :], (ge, tm)) for g in range(N_EXPERT_GROUPS)], axis=0)
    cand = jnp.where(emask > 0.0, sel, NEG_INF)
    eio = lax.broadcasted_iota(I32, (E, tm), 0)
    idxs, gates = [], []
    for _ in range(TOP_K):
        mx = jnp.max(cand, axis=0, keepdims=True)
        ix = jnp.min(jnp.where(cand == mx, eio, BIG_I32), axis=0, keepdims=True)
        hit = eio == ix
        gates.append(jnp.sum(jnp.where(hit, scores, 0.0), axis=0, keepdims=True))
        idxs.append(ix)
        cand = jnp.where(hit, NEG_INF, cand)
    eidx = jnp.concatenate(idxs, axis=0)
    gate = jnp.concatenate(gates, axis=0)
    gate = gate / jnp.sum(gate, axis=0, keepdims=True) * ROUTED_SCALE
    return eidx, gate


def _post_kernel(x_ref, ol_ref, os_ref, gtm_ref, shf_ref, scf_ref, gtf_ref, wuv_ref, goa_ref, wout_ref,
                 gffn_ref, wrh_ref, wrl_ref, br_ref, wsg_ref, wsu_ref, wsd_ref,
                 h2_ref, base_ref, eidx_ref, gate_ref):
    x = x_ref[0]
    oa = jnp.concatenate([_dot(ol_ref[0, hd], wuv_ref[hd]) for hd in range(MLA_HEADS)], axis=1)
    oan = _rms(oa, goa_ref[...]).astype(BF16)
    wa = oan.shape[1]
    mix = _dot(oan, wout_ref[:wa, :]) + _dot(os_ref[0], wout_ref[wa:, :])
    x1 = x + gtm_ref[0] * mix
    h2 = _rms(x1, gffn_ref[...]) * (1.0 + scf_ref[0]) + shf_ref[0]
    h2_hi, h2_lo = _split(h2)
    for j, plane in enumerate(_pack_rows(h2)):
        h2_ref[j] = plane
    sh =_dot((_silu(_dot(h2_hi, wsg_ref[...])) * _dot(h2_hi, wsu_ref[...])).astype(BF16), wsd_ref[...])
    base_ref[0] = x1 + gtf_ref[0] * sh
    wr_hi = wrh_ref[...]
    logits = _dot_nt(wr_hi, h2_hi) + _dot_nt(wrl_ref[...], h2_hi) + _dot_nt(wr_hi, h2_lo)
    scores = jax.nn.sigmoid(logits)
    eidx, gate = _route(scores, scores + br_ref[...])
    eidx_ref[...] = eidx
    gate_ref[...] = gate


def _post(x, o_lat, o_s5, mods, wts, tm):
    B, L, D = x.shape
    nt = L // tm
    per_row = mods[0].shape[1] != 1
    mod_spec = (pl.BlockSpec((1, tm, D), lambda b, i: (b, i, 0)) if per_row
                else pl.BlockSpec((1, 1, D), lambda b, i: (b, 0, 0)))
    W = o_s5.shape[-1]
    consts = [wts["w_uv"], wts["g_out_attn"], wts["w_out"], wts["g_ffn"], wts["wr_hi"], wts["wr_lo"],
              wts["b_router"], wts["w_sg"], wts["w_su"], wts["w_sd"]]
    in_specs = [pl.BlockSpec((1, tm, D), lambda b, i: (b, i, 0)),
                pl.BlockSpec((1, MLA_HEADS, tm, KV_LORA), lambda b, i: (b, 0, i, 0)),
                pl.BlockSpec((1, tm, W), lambda b, i: (b, i, 0)),
                mod_spec, mod_spec, mod_spec, mod_spec] + [_const_spec(c.shape) for c in consts]
    planes = D // (2 * LANES)
    out_shape = (jax.ShapeDtypeStruct((planes, B * L, LANES), U32), jax.ShapeDtypeStruct((B, L, D), F32),
                 jax.ShapeDtypeStruct((TOP_K, B * L), I32), jax.ShapeDtypeStruct((TOP_K, B * L), F32))
    out_specs = (pl.BlockSpec((planes, tm, LANES), lambda b, i: (0, b * nt + i, 0)),
                 pl.BlockSpec((1, tm, D), lambda b, i: (b, i, 0)),
                 pl.BlockSpec((TOP_K, tm), lambda b, i: (0, b * nt + i)),
                 pl.BlockSpec((TOP_K, tm), lambda b, i: (0, b * nt + i)))
    return pl.pallas_call(
        _post_kernel, grid=(B, nt), in_specs=in_specs, out_specs=out_specs, out_shape=out_shape,
        compiler_params=_params(("arbitrary", "arbitrary")), name="post",
    )(x, o_lat, o_s5, *mods, *consts)


def _rank_kernel(eidx_ref, tri_ref, rank_ref, cnt_ref, carry_scr):
    i = pl.program_id(0)

    @pl.when(i == 0)
    def _():
        carry_scr[...] = jnp.zeros(carry_scr.shape, F32)

    eidx = eidx_ref[...]
    tt = eidx.shape[1]
    eio = lax.broadcasted_iota(I32, (N_EXPERTS, tt), 0)
    hits = [eio == eidx[k:k + 1, :] for k in range(TOP_K)]
    onehot = jnp.zeros((N_EXPERTS, tt), F32)
    for hit in hits:
        onehot = jnp.where(hit, 1.0, onehot)
    before = _dot(onehot.astype(BF16), tri_ref[...]) + carry_scr[...]
    ranks = [jnp.sum(jnp.where(hit, before, 0.0), axis=0, keepdims=True) for hit in hits]
    rank_ref[...] = jnp.concatenate(ranks, axis=0).astype(I32)
    carry = carry_scr[...] + jnp.sum(onehot, axis=1, keepdims=True)
    carry_scr[...] = carry
    cnt_ref[...] = carry


def _rank(eidx, tt):
    K, T = eidx.shape
    r = np.arange(tt)
    tri = jnp.asarray((r[:, None] < r[None, :]).astype(np.float32), BF16)
    return pl.pallas_call(
        _rank_kernel, grid=(T // tt,),
        in_specs=[pl.BlockSpec((K, tt), lambda i: (0, i)), _const_spec((tt, tt))],
        out_specs=(pl.BlockSpec((K, tt), lambda i: (0, i)), _const_spec((N_EXPERTS, 1))),
        out_shape=(jax.ShapeDtypeStruct((K, T), I32), jax.ShapeDtypeStruct((N_EXPERTS, 1), F32)),
        scratch_shapes=[pltpu.VMEM((N_EXPERTS, 1), F32)],
        compiler_params=_params(("arbitrary",)), name="rank",
    )(eidx, tri)


def _pos_kernel(eidx_ref, rank_ref, pstart_ref, idx_ref, *, rows_pad):
    eidx = eidx_ref[...]
    tt = eidx.shape[1]
    eio = lax.broadcasted_iota(I32, (N_EXPERTS, tt), 0)
    pstart = pstart_ref[...]
    starts = [jnp.sum(jnp.where(eio == eidx[k:k + 1, :], pstart, 0.0), axis=0, keepdims=True)
              for k in range(TOP_K)]
    pos = jnp.concatenate(starts, axis=0).astype(I32) + rank_ref[...]
    planes = idx_ref.shape[1] // TOP_K
    for w in range(idx_ref.shape[0]):
        for j in range(planes):
            idx_ref[w, j * TOP_K:(j + 1) * TOP_K, :] = pos[:, w * SC_WINDOW:(w + 1) * SC_WINDOW] + j * rows_pad


def _pair_rows(eidx, rank, pstart, planes, rows_pad, tt):
    K, T = eidx.shape
    wpt = tt // SC_WINDOW
    return pl.pallas_call(
        functools.partial(_pos_kernel, rows_pad=rows_pad), grid=(T // tt,),
        in_specs=[pl.BlockSpec((K, tt), lambda i: (0, i)), pl.BlockSpec((K, tt), lambda i: (0, i)),
                  _const_spec((N_EXPERTS, 1))],
        out_specs=pl.BlockSpec((wpt, planes * K, SC_WINDOW), lambda i: (i, 0, 0)),
        out_shape=jax.ShapeDtypeStruct((T // SC_WINDOW, planes * K, SC_WINDOW), I32),
        compiler_params=_params(("arbitrary",)), name="pair_rows",
    )(eidx, rank, pstart)


def _expert_kernel(blk_e_ref, nvalid_ref, nused_ref, x_ref, wg_ref, wu_ref, wd_ref, o_ref):
    b = pl.program_id(0)

    @pl.when(b < nused_ref[0])
    def _():
        planes = x_ref.shape[0]
        x = _unpack_rows([x_ref[j] for j in range(planes)])
        live = lax.broadcasted_iota(I32, x.shape, 0) < nvalid_ref[b]
        x = jnp.where(live, x, jnp.zeros_like(x))
        g = _dot(x, wg_ref[0].astype(BF16))
        u = _dot(x, wu_ref[0].astype(BF16))
        y = _dot((_silu(g) * u).astype(BF16), wd_ref[0].astype(BF16))
        for j, plane in enumerate(_pack_rows(y)):
            o_ref[j] = plane


def _experts(xs, blk_e, nvalid, nused, wg, wu, wd):
    planes, rows, _ = xs.shape
    nb = rows // MOE_ROWS
    _, D, F = wg.shape

    def xmap(b, blk_e_ref, nvalid_ref, nused_ref):
        return (0, jnp.minimum(b, nused_ref[0] - 1), 0)

    def wmap(b, blk_e_ref, nvalid_ref, nused_ref):
        return (blk_e_ref[b], 0, 0)

    grid_spec = pltpu.PrefetchScalarGridSpec(
        num_scalar_prefetch=3, grid=(nb,),
        in_specs=[pl.BlockSpec((planes, MOE_ROWS, LANES), xmap),
                  pl.BlockSpec((1, D, F), wmap), pl.BlockSpec((1, D, F), wmap), pl.BlockSpec((1, F, D), wmap)],
        out_specs=pl.BlockSpec((planes, MOE_ROWS, LANES), xmap))
    return pl.pallas_call(
        _expert_kernel, grid_spec=grid_spec, out_shape=jax.ShapeDtypeStruct(xs.shape, U32),
        compiler_params=_params(("arbitrary",)), name="experts",
    )(blk_e, nvalid, nused, xs, wg, wu, wd)


def _final_kernel(base_ref, ys_ref, gate_ref, gtf_ref, gfin_ref, o_ref):
    nwin, planes = ys_ref.shape[:2]
    gate = gate_ref[...]
    routed = jnp.zeros(base_ref.shape[1:], F32)
    for k in range(TOP_K):
        rows = jnp.concatenate([_unpack_rows([ys_ref[w, j, k] for j in range(planes)]) for w in range(nwin)], axis=0)
        routed = routed + gate[:, k:k + 1] * rows.astype(F32)
    y = base_ref[0] + gtf_ref[0] * routed
    o_ref[0] = _rms(y, gfin_ref[...])


def _final(base, ysg, gate_t, gt_f, g_final, tm, row0):
    B, L, D = base.shape
    planes = ysg.shape[1]
    wpt = tm // SC_WINDOW
    nt = L // tm
    off = row0 // tm
    per_row = gt_f.shape[1] != 1
    mod_spec = (pl.BlockSpec((1, tm, D), lambda b, i: (b, i, 0)) if per_row
                else pl.BlockSpec((1, 1, D), lambda b, i: (b, 0, 0)))
    return pl.pallas_call(
        _final_kernel, grid=(B, nt),
        in_specs=[pl.BlockSpec((1, tm, D), lambda b, i: (b, i, 0)),
                  pl.BlockSpec((wpt, planes, TOP_K, SC_WINDOW, LANES), lambda b, i: (off + b * nt + i, 0, 0, 0, 0)),
                  pl.BlockSpec((tm, TOP_K), lambda b, i: (off + b * nt + i, 0)),
                  mod_spec, _const_spec((1, D))],
        out_specs=pl.BlockSpec((1, tm, D), lambda b, i: (b, i, 0)),
        out_shape=jax.ShapeDtypeStruct((B, L, D), F32),
        compiler_params=_params(("arbitrary", "arbitrary")), name="final",
    )(base, ysg, gate_t, gt_f, g_final)


def _sc_mesh():
    return plsc.VectorSubcoreMesh(core_axis_name="c", subcore_axis_name="s")


def _sc_worker():
    return lax.axis_index("s") * SC_CORES + lax.axis_index("c")


def _sc_dispatch(src, idx, out_rows, planes, n_tok):
    n_win = n_tok // SC_WINDOW
    n_iter = -(-n_win // SC_WORKERS)

    @functools.partial(
        pl.kernel, mesh=_sc_mesh(), out_type=jax.ShapeDtypeStruct((out_rows, LANES), src.dtype),
        scratch_types=[pltpu.VMEM((planes * TOP_K, SC_WINDOW), I32),
                       pltpu.VMEM((planes, SC_WINDOW, LANES), src.dtype),
                       pltpu.SemaphoreType.DMA((planes,)), pltpu.SemaphoreType.DMA],
        name="sc_dispatch")
    def k(src_hbm, idx_hbm, out_hbm, idx_v, rows_v, load_sem, scat_sem):
        wid = _sc_worker()

        @pl.loop(0, n_iter)
        def _(i):
            win = i * SC_WORKERS + wid

            @pl.when(win < n_win)
            def _():
                t0 = win * SC_WINDOW
                loads = [pltpu.make_async_copy(
                    src_hbm.at[pl.ds(pl.multiple_of(j * n_tok + t0, SC_WINDOW), SC_WINDOW)],
                    rows_v.at[j], load_sem.at[j]) for j in range(planes)]
                for ld in loads:
                    ld.start()
                pltpu.sync_copy(idx_hbm.at[win], idx_v)
                scatters = []
                for j in range(planes):
                    loads[j].wait()
                    for kk in range(TOP_K):
                        cp = pltpu.make_async_copy(rows_v.at[j], out_hbm.at[idx_v.at[j * TOP_K + kk]], scat_sem)
                        cp.start()
                        scatters.append(cp)
                for cp in scatters:
                    cp.wait()

    return k(src, idx)


def _sc_gather(table, idx):
    n_chunks, win = idx.shape
    n_iter = n_chunks // SC_WORKERS
    assert n_chunks % SC_WORKERS == 0 and n_iter % 2 == 0

    @functools.partial(
        pl.kernel, mesh=_sc_mesh(), out_type=jax.ShapeDtypeStruct((n_chunks * win, LANES), table.dtype),
        scratch_types=[pltpu.VMEM((n_iter, win), I32), pltpu.VMEM((2, win, LANES), table.dtype),
                       pltpu.SemaphoreType.DMA((2,)), pltpu.SemaphoreType.DMA((2,))],
        name="sc_gather")
    def k(table_hbm, idx_hbm, out_hbm, idx_v, rows_v, gather_sem, write_sem):
        wid = _sc_worker()
        c0 = wid * n_iter
        pltpu.sync_copy(idx_hbm.at[wid], idx_v)

        def gather(c, b):
            return pltpu.make_async_copy(table_hbm.at[idx_v.at[c]], rows_v.at[b], gather_sem.at[b])

        def write(c, b):
            return pltpu.make_async_copy(rows_v.at[b], out_hbm.at[pl.ds(pl.multiple_of((c0 + c) * win, win), win)],
                                         write_sem.at[b])

        gather(0, 0).start()

        @pl.loop(0, n_iter, step=2)
        def _(i):
            for b in range(2):
                c = i + b
                other = 1 - b

                @pl.when(c >= 1)
                def _():
                    write(c - 1, other).wait()

                @pl.when(c + 1 < n_iter)
                def _():
                    gather(c + 1, other).start()

                gather(c, b).wait()
                write(c, b).start()

        write(n_iter - 1, (n_iter - 1) % 2).wait()

    return k(table, idx.reshape(SC_WORKERS, n_iter, win))


def _rope_tables(pos):
    half = ROPE_DIM // 2
    inv = ROPE_THETA ** (-jnp.arange(half, dtype=F32) / half)
    ang = pos.astype(F32)[:, None] * inv[None, :]
    cos = jnp.tile(jnp.cos(ang), (1, 2 * MLA_HEADS))
    sin = jnp.tile(jnp.sin(ang), (1, 2 * MLA_HEADS))
    return cos, sin


def _rot_cols(w):
    half = ROPE_DIM // 2
    return jnp.concatenate([-w[..., half:], w[..., :half]], axis=-1)


def _layer_weights(w_in, g_mix, g_q, g_kv, w_uq, w_uk, w_uv, s5_D, s5_w_glu, s5_b_glu, g_out_attn, g_out_s5,
                   w_out, g_ffn, w_router, b_router, w_sh_gate, w_sh_up, w_sh_down):
    D = w_in.shape[0]
    o1, o2, o3 = Q_LORA, Q_LORA + KV_LORA, Q_LORA + KV_LORA + ROPE_DIM
    s5w = w_in.shape[1] - o3
    w_rope = w_in[:, o2:o3]
    w_in_ext = jnp.concatenate([w_in[:, :o2], w_in[:, o3:], w_rope, _rot_cols(w_rope)], axis=1).astype(BF16)
    wq = w_uq.reshape(Q_LORA, MLA_HEADS, NOPE_DIM + ROPE_DIM)
    wq_rope = wq[:, :, NOPE_DIM:]
    wr_hi, wr_lo = _split(w_router.T)
    return {
        "s5w": s5w,
        "g_mix": g_mix.reshape(1, D), "w_in": w_in_ext,
        "g_q": g_q.reshape(1, Q_LORA), "g_kv": g_kv.reshape(1, KV_LORA),
        "wq_nope": wq[:, :, :NOPE_DIM].reshape(Q_LORA, MLA_HEADS * NOPE_DIM).astype(BF16),
        "wq_rope": wq_rope.reshape(Q_LORA, MLA_HEADS * ROPE_DIM).astype(BF16),
        "wq_rot": _rot_cols(wq_rope).reshape(Q_LORA, MLA_HEADS * ROPE_DIM).astype(BF16),
        "w_uk": w_uk.reshape(KV_LORA, MLA_HEADS * NOPE_DIM).astype(BF16),
        "w_uv": jnp.transpose(w_uv, (1, 0, 2)).astype(BF16),
        "s5_d": s5_D.reshape(1, s5w), "w_glu": s5_w_glu.astype(BF16), "b_glu": s5_b_glu.reshape(1, s5w),
        "g_out_attn": g_out_attn.reshape(1, -1), "g_out_s5": g_out_s5.reshape(1, s5w),
        "w_out": w_out.astype(BF16), "g_ffn": g_ffn.reshape(1, D),
        "wr_hi": wr_hi, "wr_lo": wr_lo, "b_router": b_router.reshape(N_EXPERTS, 1),
        "w_sg": w_sh_gate.astype(BF16), "w_su": w_sh_up.astype(BF16), "w_sd": w_sh_down.astype(BF16),
    }


def _state_in(s_re, s_im):
    B = s_re.shape[0]
    return jnp.stack([s_re.reshape(B, -1), s_im.reshape(B, -1)], axis=1).astype(F32)


def kernel(x_prompt, x_sample, c_prompt, c_sample, cache_ckv, cache_krope, state_s5_re, state_s5_im, w_ada, b_ada, g_mix, g_ffn, w_in, g_q, w_uq, g_kv, w_uk, w_uv, s5_A_re, s5_A_im, s5_B_re, s5_B_im, s5_C_re, s5_C_im, s5_D, s5_log_dt, s5_w_glu, s5_b_glu, g_out_attn, g_out_s5, w_out, w_router, b_router, w_exp_gate, w_exp_up, w_exp_down, w_sh_gate, w_sh_up, w_sh_down, g_final):
    Bp, Lp, D = x_prompt.shape
    Bs, Ls, _ = x_sample.shape
    depth = w_ada.shape[0]
    assert depth == 1, "single-layer step"
    past = cache_ckv.shape[2]
    G, N = s5_A_re.shape[1:]
    Ts = Bs * Ls
    Tp = Bp * Lp
    l = 0

    wts = _layer_weights(w_in[l], g_mix[l], g_q[l], g_kv[l], w_uq[l], w_uk[l], w_uv[l], s5_D[l], s5_w_glu[l],
                         s5_b_glu[l], g_out_attn[l], g_out_s5[l], w_out[l], g_ffn[l], w_router[l], b_router[l],
                         w_sh_gate[l], w_sh_up[l], w_sh_down[l])

    mod = _ada(jnp.concatenate([c_prompt, c_sample], axis=0), w_ada[l], b_ada[l])
    mod_p = [m.reshape(Bp, 1, D) for m in jnp.split(mod[:Bp], 6, axis=-1)]
    mod_s = [jnp.broadcast_to(m[:, None, :], (Bs, Ls, D)).reshape(1, Ts, D)
             for m in jnp.split(mod[Bp:], 6, axis=-1)]

    tm_p = 256
    cos_p, sin_p = _rope_tables(jnp.arange(Lp))
    cos_s, sin_s = _rope_tables(jnp.tile(past + jnp.arange(Ls), Bs))

    ckv_p, kr_p, u_p, kcat_p, v_p, q_p = _pre(x_prompt, mod_p[0], mod_p[1], cos_p, sin_p, wts, tm_p)
    olat_p = _attn_prompt(q_p, kcat_p, v_p, 512)
    tabs_p = _s5_tables(s5_A_re[l], s5_A_im[l], s5_B_re[l], s5_B_im[l], s5_C_re[l], s5_C_im[l], s5_log_dt[l], tm_p)
    os5_p, hl_p = _s5(u_p, jnp.zeros((Bp, 2, G * N), F32), tabs_p, wts, tm_p, precise=False)
    h2_p, base_p, eidx_p, gate_p = _post(x_prompt, olat_p, os5_p, [mod_p[2], mod_p[3], mod_p[4], mod_p[5]], wts, tm_p)

    xs_rows = x_sample.reshape(1, Ts, D)
    ckv_s, kr_s, u_s, kcat_s, v_s, q_s = _pre(xs_rows, mod_s[0], mod_s[1], cos_s, sin_s, wts, Ts)
    olat_s = _attn_sample(q_s, kcat_s, v_s, cache_ckv[l], cache_krope[l], wts["w_uk"])
    tabs_s = _s5_tables(s5_A_re[l], s5_A_im[l], s5_B_re[l], s5_B_im[l], s5_C_re[l], s5_C_im[l], s5_log_dt[l], Ls)
    os5_s, hl_s = _s5(u_s.reshape(Bs, Ls, -1), _state_in(state_s5_re[l], state_s5_im[l]), tabs_s, wts, Ls,
                      precise=True)
    h2_s, base_s, eidx_s, gate_s = _post(xs_rows, olat_s, os5_s.reshape(1, Ts, -1),
                                         [mod_s[2], mod_s[3], mod_s[4], mod_s[5]], wts, Ts)

    T_all = Tp + Ts
    eidx = jnp.concatenate([eidx_p, eidx_s], axis=1)
    gate = jnp.concatenate([gate_p, gate_s], axis=1)
    rank, counts = _rank(eidx, 256)
    counts = counts.reshape(N_EXPERTS).astype(I32)
    nblk = (counts + MOE_ROWS - 1) // MOE_ROWS
    blk_end = jnp.cumsum(nblk)
    pstart = (blk_end - nblk) * MOE_ROWS
    nb_max = (T_all * TOP_K) // MOE_ROWS + N_EXPERTS
    nused = blk_end[-1]
    blk_e = jnp.minimum(jnp.searchsorted(blk_end, jnp.arange(nb_max, dtype=I32), side="right"),
                        N_EXPERTS - 1).astype(I32)
    bidx = jnp.arange(nb_max, dtype=I32)
    blk_e = jnp.where(bidx < nused, blk_e, blk_e[jnp.maximum(nused - 1, 0)])
    nvalid = jnp.clip(counts[blk_e] - (bidx - (blk_end - nblk)[blk_e]) * MOE_ROWS, 0, MOE_ROWS).astype(I32)
    rows_pad = nb_max * MOE_ROWS
    planes = h2_p.shape[0]
    idx = _pair_rows(eidx, rank, pstart.astype(F32).reshape(N_EXPERTS, 1), planes, rows_pad, 256)
    n_win = T_all // SC_WINDOW
    h2_all = jnp.concatenate([h2_p, h2_s], axis=1).reshape(planes * T_all, LANES)
    xs_sorted = _sc_dispatch(h2_all, idx, planes * rows_pad, planes, T_all).reshape(planes, rows_pad, LANES)
    ys = _experts(xs_sorted, blk_e, nvalid, nused.reshape(1).astype(I32), w_exp_gate[l], w_exp_up[l], w_exp_down[l])
    ysg = _sc_gather(ys.reshape(planes * rows_pad, LANES), idx.reshape(n_win * planes * TOP_K, SC_WINDOW))
    ysg = ysg.reshape(n_win, planes, TOP_K, SC_WINDOW, LANES)
    gate_t = gate.T

    gfin = g_final.reshape(1, D)
    y_p = _final(base_p, ysg, gate_t, mod_p[5], gfin, tm_p, 0)
    y_s = _final(base_s, ysg, gate_t, mod_s[5], gfin, Ts, Tp).reshape(Bs, Ls, D)

    def state_out(hl, B):
        return hl[:, 0].reshape(1, B, G, N), hl[:, 1].reshape(1, B, G, N)

    sre_p, sim_p = state_out(hl_p, Bp)
    sre_s, sim_s = state_out(hl_s, Bs)
    return (y_p, y_s, ckv_p[None], kr_p[None], sre_p, sim_p,
            ckv_s.reshape(1, Bs, Ls, KV_LORA), kr_s.reshape(1, Bs, Ls, ROPE_DIM), sre_s, sim_s)
```

```python
import functools
import math

import numpy as np
import jax
import jax.numpy as jnp
from jax import lax
from jax.experimental import pallas as pl
from jax.experimental.pallas import tpu as pltpu
from jax.experimental.pallas import tpu_sc as plsc

F32 = jnp.float32
BF16 = jnp.bfloat16
I32 = jnp.int32
U32 = jnp.uint32

EPS = 1e-6
CHUNK = 64
MLA_HEADS = 4
NOPE_DIM = 128
ROPE_DIM = 64
V_DIM = 128
Q_LORA = 256
KV_LORA = 256
QK_HEAD = NOPE_DIM + ROPE_DIM
ROPE_THETA = 10000.0
S5_GROUP_CH = 16
S5_STATE = 64
N_EXPERTS = 256
TOP_K = 8
N_EXPERT_GROUPS = 8
TOPK_GROUPS = 4
ROUTED_SCALE = 2.5

S5_SUB = 16
MOE_ROWS = 512
LANES = 128
VMEM_LIMIT = 56 * 1024 * 1024
SC_CORES = 2
SC_SUBCORES = 16
SC_WORKERS = SC_CORES * SC_SUBCORES
SC_WINDOW = 128

NEG_INF = float("-inf")
BIG_I32 = 1 << 30


def _dot(a, b):
    return jnp.dot(a, b, preferred_element_type=F32)


def _dot_nt(a, b):
    return lax.dot_general(a, b, (((1,), (1,)), ((), ())), preferred_element_type=F32)


def _split(a):
    hi = a.astype(BF16)
    lo = (a - hi.astype(F32)).astype(BF16)
    return hi, lo


def _lane_tile(x, n):
    return jnp.concatenate([x] * n, axis=1)


def _rms(x, g):
    return x * lax.rsqrt(jnp.mean(x * x, axis=-1, keepdims=True) + EPS) * g


def _silu(x):
    return x * jax.nn.sigmoid(x)


def _pack_rows(x):
    half = x.shape[1] // 2
    hi = lax.bitcast_convert_type(x[:, :half].astype(BF16).astype(F32), U32)
    lo = lax.bitcast_convert_type(x[:, half:].astype(BF16).astype(F32), U32)
    w = hi | (lo >> 16)
    return [w[:, j * LANES:(j + 1) * LANES] for j in range(half // LANES)]


def _unpack_rows(planes):
    his = [lax.bitcast_convert_type(p & jnp.uint32(0xFFFF0000), F32).astype(BF16) for p in planes]
    los = [lax.bitcast_convert_type(p << 16, F32).astype(BF16) for p in planes]
    return jnp.concatenate(his + los, axis=1)


def _params(sem):
    return pltpu.CompilerParams(dimension_semantics=sem, vmem_limit_bytes=VMEM_LIMIT)


def _const_spec(shape):
    nd = len(shape)
    return pl.BlockSpec(shape, lambda *_: (0,) * nd)


def _ada_kernel(c_ref, whi_ref, wlo_ref, b_ref, o_ref):
    c = c_ref[...]
    s_hi, s_lo = _split(_silu(c))
    w_hi = whi_ref[...]
    o_ref[...] = _dot(s_hi, w_hi) + _dot(s_hi, wlo_ref[...]) + _dot(s_lo, w_hi) + b_ref[...]


def _ada(c, w_ada, b_ada):
    rows, d = c.shape
    n = w_ada.shape[1]
    tn = 512
    w_hi, w_lo = _split(w_ada)
    return pl.pallas_call(
        _ada_kernel,
        grid=(n // tn,),
        in_specs=[_const_spec((rows, d)),
                  pl.BlockSpec((d, tn), lambda j: (0, j)),
                  pl.BlockSpec((d, tn), lambda j: (0, j)),
                  pl.BlockSpec((1, tn), lambda j: (0, j))],
        out_specs=pl.BlockSpec((rows, tn), lambda j: (0, j)),
        out_shape=jax.ShapeDtypeStruct((rows, n), F32),
        compiler_params=_params(("arbitrary",)),
        name="ada",
    )(c, w_hi, w_lo, b_ada.reshape(1, n))


def _pre_kernel(x_ref, sh_ref, sc_ref, g_ref, win_ref, gq_ref, gkv_ref, wqn_ref, wqr_ref, wqt_ref,
                wuk_ref, cos_ref, sin_ref, ckv_ref, kr_ref, u_ref, kcat_ref, v_ref, q_ref, *, scale):
    x = x_ref[0]
    h = _rms(x, g_ref[...]) * (1.0 + sc_ref[0]) + sh_ref[0]
    z = _dot(h.astype(BF16), win_ref[...])
    cq = _rms(z[:, :Q_LORA], gq_ref[...])
    ckv = _rms(z[:, Q_LORA:Q_LORA + KV_LORA], gkv_ref[...])
    o_s5 = Q_LORA + KV_LORA
    s5w = u_ref.shape[-1]
    u_ref[0] = z[:, o_s5:o_s5 + s5w]
    o_r = o_s5 + s5w
    cos = cos_ref[...]
    sin = sin_ref[...]
    kr = z[:, o_r:o_r + ROPE_DIM] * cos[:, :ROPE_DIM] + z[:, o_r + ROPE_DIM:o_r + 2 * ROPE_DIM] * sin[:, :ROPE_DIM]
    ckv_ref[0] = ckv
    kr_ref[0] = kr
    ckvb = ckv.astype(BF16)
    krb = kr.astype(BF16)
    v_ref[0] = ckvb
    kn = _dot(ckvb, wuk_ref[...])
    cqb = cq.astype(BF16)
    qn = _dot(cqb, wqn_ref[...]) * scale
    qr = (_dot(cqb, wqr_ref[...]) * cos + _dot(cqb, wqt_ref[...]) * sin) * scale
    for hd in range(MLA_HEADS):
        kcat_ref[0, hd, :, :NOPE_DIM] = kn[:, hd * NOPE_DIM:(hd + 1) * NOPE_DIM].astype(BF16)
        kcat_ref[0, hd, :, NOPE_DIM:] = krb
        q_ref[0, hd, :, :NOPE_DIM] = qn[:, hd * NOPE_DIM:(hd + 1) * NOPE_DIM].astype(BF16)
        q_ref[0, hd, :, NOPE_DIM:] = qr[:, hd * ROPE_DIM:(hd + 1) * ROPE_DIM].astype(BF16)


def _pre(x, shift, scale_mod, cos_t, sin_t, wts, tm):
    B, L, D = x.shape
    nt = L // tm
    per_row = shift.shape[1] != 1
    mod_spec = (pl.BlockSpec((1, tm, D), lambda b, i: (b, i, 0)) if per_row
                else pl.BlockSpec((1, 1, D), lambda b, i: (b, 0, 0)))
    s5w = wts["s5w"]
    hr = MLA_HEADS * ROPE_DIM
    kern = functools.partial(_pre_kernel, scale=QK_HEAD ** -0.5 * math.log2(math.e))
    consts = [wts["g_mix"], wts["w_in"], wts["g_q"], wts["g_kv"], wts["wq_nope"], wts["wq_rope"],
              wts["wq_rot"], wts["w_uk"]]
    in_specs = [pl.BlockSpec((1, tm, D), lambda b, i: (b, i, 0)), mod_spec, mod_spec]
    in_specs += [_const_spec(c.shape) for c in consts]
    in_specs += [pl.BlockSpec((tm, hr), lambda b, i: (i, 0)), pl.BlockSpec((tm, hr), lambda b, i: (i, 0))]
    out_shape = (jax.ShapeDtypeStruct((B, L, KV_LORA), F32),
                 jax.ShapeDtypeStruct((B, L, ROPE_DIM), F32),
                 jax.ShapeDtypeStruct((B, L, s5w), F32),
                 jax.ShapeDtypeStruct((B, MLA_HEADS, L, QK_HEAD), BF16),
                 jax.ShapeDtypeStruct((B, L, KV_LORA), BF16),
                 jax.ShapeDtypeStruct((B, MLA_HEADS, L, QK_HEAD), BF16))
    out_specs = (pl.BlockSpec((1, tm, KV_LORA), lambda b, i: (b, i, 0)),
                 pl.BlockSpec((1, tm, ROPE_DIM), lambda b, i: (b, i, 0)),
                 pl.BlockSpec((1, tm, s5w), lambda b, i: (b, i, 0)),
                 pl.BlockSpec((1, MLA_HEADS, tm, QK_HEAD), lambda b, i: (b, 0, i, 0)),
                 pl.BlockSpec((1, tm, KV_LORA), lambda b, i: (b, i, 0)),
                 pl.BlockSpec((1, MLA_HEADS, tm, QK_HEAD), lambda b, i: (b, 0, i, 0)))
    return pl.pallas_call(
        kern, grid=(B, nt), in_specs=in_specs, out_specs=out_specs, out_shape=out_shape,
        compiler_params=_params(("arbitrary", "arbitrary")), name="pre",
    )(x, shift, scale_mod, *consts, cos_t, sin_t)


def _attn_kernel(q_ref, k_ref, v_ref, o_ref, m_scr, l_scr, acc_scr, *, t):
    i = pl.program_id(1)
    m_scr[...] = jnp.full(m_scr.shape, NEG_INF, F32)
    l_scr[...] = jnp.zeros(l_scr.shape, F32)
    acc_scr[...] = jnp.zeros(acc_scr.shape, F32)
    visible = (lax.broadcasted_iota(I32, (t, t), 1) // CHUNK) <= (lax.broadcasted_iota(I32, (t, t), 0) // CHUNK)

    def step(j0, masked):
        v = v_ref[0, pl.ds(j0, t), :]
        for hd in range(MLA_HEADS):
            rs = slice(hd * t, (hd + 1) * t)
            s = _dot_nt(q_ref[0, hd], k_ref[0, hd, pl.ds(j0, t), :])
            if masked:
                s = jnp.where(visible, s, NEG_INF)
            m_prev = m_scr[rs]
            m_next = jnp.maximum(m_prev, jnp.max(s, axis=1, keepdims=True))
            alpha = jnp.exp2(m_prev - m_next)
            p = jnp.exp2(s - _lane_tile(m_next, t // LANES))
            l_scr[rs] = alpha * l_scr[rs] + jnp.sum(p, axis=1, keepdims=True)
            m_scr[rs] = m_next
            acc_scr[rs] = acc_scr[rs] * _lane_tile(alpha, KV_LORA // LANES) + _dot(p.astype(BF16), v)

    def body(j, carry):
        step(pl.multiple_of(j * t, t), False)
        return carry

    lax.fori_loop(0, i, body, 0)
    step(pl.multiple_of(i * t, t), True)
    for hd in range(MLA_HEADS):
        rs = slice(hd * t, (hd + 1) * t)
        inv = 1.0 / l_scr[rs]
        o_ref[0, hd] = (acc_scr[rs] * _lane_tile(inv, KV_LORA // LANES)).astype(BF16)


def _attn_prompt(q, kcat, v, t):
    B, H, L, _ = q.shape
    assert L % t == 0 and t % CHUNK == 0
    rows = H * t
    kern = functools.partial(_attn_kernel, t=t)
    resident = pl.Buffered(1)
    return pl.pallas_call(
        kern, grid=(B, L // t),
        in_specs=[pl.BlockSpec((1, H, t, QK_HEAD), lambda b, i: (b, 0, i, 0)),
                  pl.BlockSpec((1, H, L, QK_HEAD), lambda b, i: (b, 0, 0, 0), pipeline_mode=resident),
                  pl.BlockSpec((1, L, KV_LORA), lambda b, i: (b, 0, 0), pipeline_mode=resident)],
        out_specs=pl.BlockSpec((1, H, t, KV_LORA), lambda b, i: (b, 0, i, 0)),
        out_shape=jax.ShapeDtypeStruct((B, H, L, KV_LORA), BF16),
        scratch_shapes=[pltpu.VMEM((rows, LANES), F32), pltpu.VMEM((rows, LANES), F32),
                        pltpu.VMEM((rows, KV_LORA), F32)],
        compiler_params=_params(("arbitrary", "arbitrary")), name="attn_prompt",
    )(q, kcat, v)


def _attn_sample_kernel(q_ref, kn_ref, vn_ref, pc_ref, pr_ref, wuk_ref, o_ref, *, past, lq):
    pc = pc_ref[0].astype(BF16)
    pr = pr_ref[0].astype(BF16)
    kp = _dot(pc, wuk_ref[...]).astype(BF16)
    vn = vn_ref[0]
    qchunk_p = (past + lax.broadcasted_iota(I32, (lq, past), 0)) // CHUNK
    vis_p = lax.broadcasted_iota(I32, (lq, past), 1) // CHUNK <= qchunk_p
    qchunk_n = (past + lax.broadcasted_iota(I32, (lq, lq), 0)) // CHUNK
    vis_n = (past + lax.broadcasted_iota(I32, (lq, lq), 1)) // CHUNK <= qchunk_n
    for hd in range(MLA_HEADS):
        q = q_ref[0, hd]
        s_p = (_dot_nt(q[:, :NOPE_DIM], kp[:, hd * NOPE_DIM:(hd + 1) * NOPE_DIM])
               + _dot_nt(q[:, NOPE_DIM:], pr))
        s_n = _dot_nt(q, kn_ref[0, hd])
        s_p = jnp.where(vis_p, s_p, NEG_INF)
        s_n = jnp.where(vis_n, s_n, NEG_INF)
        m = jnp.maximum(jnp.max(s_p, axis=1, keepdims=True), jnp.max(s_n, axis=1, keepdims=True))
        p_p = jnp.exp2(s_p - m)
        p_n = jnp.exp2(s_n - m)
        l = jnp.sum(p_p, axis=1, keepdims=True) + jnp.sum(p_n, axis=1, keepdims=True)
        o = _dot(p_p.astype(BF16), pc) + _dot(p_n.astype(BF16), vn)
        o_ref[0, hd] = (o / l).astype(BF16)


def _attn_sample(q, kcat, v, past_ckv, past_kr, w_uk):
    B, past, _ = past_ckv.shape
    H = MLA_HEADS
    lq = q.shape[2] // B
    kern = functools.partial(_attn_sample_kernel, past=past, lq=lq)
    return pl.pallas_call(
        kern, grid=(B,),
        in_specs=[pl.BlockSpec((1, H, lq, QK_HEAD), lambda b: (0, 0, b, 0)),
                  pl.BlockSpec((1, H, lq, QK_HEAD), lambda b: (0, 0, b, 0)),
                  pl.BlockSpec((1, lq, KV_LORA), lambda b: (0, b, 0)),
                  pl.BlockSpec((1, past, KV_LORA), lambda b: (b, 0, 0)),
                  pl.BlockSpec((1, past, ROPE_DIM), lambda b: (b, 0, 0)),
                  _const_spec(w_uk.shape)],
        out_specs=pl.BlockSpec((1, H, lq, KV_LORA), lambda b: (0, 0, b, 0)),
        out_shape=jax.ShapeDtypeStruct((1, H, B * lq, KV_LORA), BF16),
        compiler_params=_params(("arbitrary",)), name="attn_sample",
    )(q, kcat, v, past_ckv, past_kr, w_uk)


def _s5_kernel(u_ref, h0_ref, bre_ref, bim_ref, brel_ref, biml_ref, lt_ref, pinr_ref, pini_ref,
               pwr_ref, pwi_ref, a_ref, cre_ref, cim_ref, d_ref, wglu_ref, bglu_ref, gout_ref,
               o_ref, hl_ref, st_scr, cum_scr, hs_scr, *, tm, precise):
    i = pl.program_id(1)
    ns = st_scr.shape[1]
    half = ns // 2
    wh = u_ref.shape[-1] // 2

    @pl.when(i == 0)
    def _():
        st_scr[...] = h0_ref[0]

    u = u_ref[0]
    lt = lt_ref[...]
    for hf in range(2):
        uh = u[:, hf * wh:(hf + 1) * wh]
        if precise:
            u_hi, u_lo = _split(uh)
            bu_re = _dot(u_hi, bre_ref[hf]) + _dot(u_lo, bre_ref[hf]) + _dot(u_hi, brel_ref[hf])
            bu_im = _dot(u_hi, bim_ref[hf]) + _dot(u_lo, bim_ref[hf]) + _dot(u_hi, biml_ref[hf])
        else:
            u_hi = uh.astype(BF16)
            bu_re = _dot(u_hi, bre_ref[hf])
            bu_im = _dot(u_hi, bim_ref[hf])
        sl = slice(hf * half, (hf + 1) * half)
        pr = pinr_ref[:, sl]
        pi = pini_ref[:, sl]
        v_re = pr * bu_re - pi * bu_im
        v_im = pr * bu_im + pi * bu_re
        for part, v in ((0, v_re), (1, v_im)):
            if precise:
                v_hi, v_lo = _split(v)
                c = _dot(lt, v_hi) + _dot(lt, v_lo)
            else:
                c = _dot(lt, v.astype(BF16))
            cum_scr[:, part * ns + hf * half:part * ns + (hf + 1) * half] = c

    a_re = a_ref[0:1, :]
    a_im = a_ref[1:2, :]
    pw_re = pwr_ref[...]
    pw_im = pwi_ref[...]

    def chunk(c, carry):
        s_re, s_im = carry
        r0 = pl.multiple_of(c * S5_SUB, S5_SUB)
        t_re = cum_scr[pl.ds(r0, S5_SUB), 0:ns] + (a_re * s_re - a_im * s_im)
        t_im = cum_scr[pl.ds(r0, S5_SUB), ns:2 * ns] + (a_re * s_im + a_im * s_re)
        h_re = pw_re * t_re - pw_im * t_im
        h_im = pw_re * t_im + pw_im * t_re
        hs_scr[pl.ds(r0, S5_SUB), 0:ns] = h_re.astype(BF16)
        hs_scr[pl.ds(r0, S5_SUB), ns:2 * ns] = h_im.astype(BF16)
        return h_re[S5_SUB - 1:S5_SUB, :], h_im[S5_SUB - 1:S5_SUB, :]

    s_re, s_im = lax.fori_loop(0, tm // S5_SUB, chunk, (st_scr[0:1, :], st_scr[1:2, :]))
    st_scr[0:1, :] = s_re
    st_scr[1:2, :] = s_im
    hl_ref[0, 0:1, :] = s_re
    hl_ref[0, 1:2, :] = s_im

    ys = []
    for hf in range(2):
        hre = hs_scr[:, hf * half:(hf + 1) * half]
        him = hs_scr[:, ns + hf * half:ns + (hf + 1) * half]
        ys.append(_dot(hre, cre_ref[hf]) + _dot(him, cim_ref[hf]))
    y = jnp.concatenate(ys, axis=1) + d_ref[...] * u
    zg = jax.nn.gelu(y)
    gl = _dot(zg.astype(BF16), wglu_ref[...]) + bglu_ref[...]
    o = zg * jax.nn.sigmoid(gl)
    o_ref[0] = _rms(o, gout_ref[...]).astype(BF16)


def _s5(u, h0, tabs, wts, tm, precise):
    B, L, W = u.shape
    ns = h0.shape[-1]
    consts = [tabs["b_re"], tabs["b_im"], tabs["b_re_lo"], tabs["b_im_lo"], tabs["lt"], tabs["pin_re"],
              tabs["pin_im"], tabs["pw_re"], tabs["pw_im"], tabs["a"], tabs["c_re"], tabs["c_im"],
              wts["s5_d"], wts["w_glu"], wts["b_glu"], wts["g_out_s5"]]
    kern = functools.partial(_s5_kernel, tm=tm, precise=precise)
    return pl.pallas_call(
        kern, grid=(B, L // tm),
        in_specs=[pl.BlockSpec((1, tm, W), lambda b, i: (b, i, 0)),
                  pl.BlockSpec((1, 2, ns), lambda b, i: (b, 0, 0))] + [_const_spec(c.shape) for c in consts],
        out_specs=(pl.BlockSpec((1, tm, W), lambda b, i: (b, i, 0)),
                   pl.BlockSpec((1, 2, ns), lambda b, i: (b, 0, 0))),
        out_shape=(jax.ShapeDtypeStruct((B, L, W), BF16), jax.ShapeDtypeStruct((B, 2, ns), F32)),
        scratch_shapes=[pltpu.VMEM((2, ns), F32), pltpu.VMEM((tm, 2 * ns), F32), pltpu.VMEM((tm, 2 * ns), BF16)],
        compiler_params=_params(("arbitrary", "arbitrary")), name="s5",
    )(u, h0, *consts)


def _s5_tables(a_re_p, a_im_p, b_re_p, b_im_p, c_re_p, c_im_p, log_dt, tm):
    G, N = a_re_p.shape
    CH = b_re_p.shape[-1]
    dt = jnp.exp(log_dt.astype(F32))[:, None]
    lr = a_re_p.astype(F32) * dt
    li = a_im_p.astype(F32) * dt
    er = jnp.exp(lr)
    ab_re, ab_im = er * jnp.cos(li), er * jnp.sin(li)
    lam2 = a_re_p.astype(F32) ** 2 + a_im_p.astype(F32) ** 2
    nr, ni = ab_re - 1.0, ab_im
    f_re = (nr * a_re_p + ni * a_im_p) / lam2
    f_im = (ni * a_re_p - nr * a_im_p) / lam2
    bb_re = f_re[..., None] * b_re_p - f_im[..., None] * b_im_p
    bb_im = f_re[..., None] * b_im_p + f_im[..., None] * b_re_p
    gh = G // 2
    eye = jnp.eye(gh, dtype=F32)

    def blk_b(bb):
        t = bb.reshape(2, gh, N, CH)
        m = jnp.einsum("hgnc,gk->hgckn", t, eye)
        return m.reshape(2, gh * CH, gh * N)

    def blk_c(cc):
        t = cc.reshape(2, gh, CH, N)
        m = jnp.einsum("hgcn,gk->hgnkc", t, eye)
        return m.reshape(2, gh * N, gh * CH)

    b_re_m, b_im_m = blk_b(bb_re), blk_b(bb_im)
    b_re_hi, b_re_lo = _split(b_re_m)
    b_im_hi, b_im_lo = _split(b_im_m)
    s = jnp.arange(S5_SUB, dtype=F32)[:, None, None]

    def powers(sign):
        e = jnp.exp(sign * lr[None] * s)
        return ((e * jnp.cos(sign * li[None] * s)).reshape(S5_SUB, G * N),
                (e * jnp.sin(sign * li[None] * s)).reshape(S5_SUB, G * N))

    pin_re, pin_im = powers(-1.0)
    pw_re, pw_im = powers(1.0)
    reps = tm // S5_SUB
    r = np.arange(tm)
    lt = ((r[:, None] // S5_SUB == r[None, :] // S5_SUB) & (r[None, :] <= r[:, None])).astype(np.float32)
    return {
        "b_re": b_re_hi, "b_im": b_im_hi, "b_re_lo": b_re_lo, "b_im_lo": b_im_lo,
        "lt": jnp.asarray(lt, BF16),
        "pin_re": jnp.tile(pin_re, (reps, 1)), "pin_im": jnp.tile(pin_im, (reps, 1)),
        "pw_re": pw_re, "pw_im": pw_im,
        "a": jnp.stack([ab_re.reshape(G * N), ab_im.reshape(G * N)]),
        "c_re": blk_c(c_re_p.astype(F32)).astype(BF16), "c_im": (-blk_c(c_im_p.astype(F32))).astype(BF16),
    }


def _route(scores, sel):
    E, tm = scores.shape
    ge = E // N_EXPERT_GROUPS
    io_g = lax.broadcasted_iota(I32, (ge, tm), 0)
    gs_rows = []
    for g in range(N_EXPERT_GROUPS):
        sg = sel[g * ge:(g + 1) * ge, :]
        m1 = jnp.max(sg, axis=0, keepdims=True)
        i1 = jnp.min(jnp.where(sg == m1, io_g, BIG_I32), axis=0, keepdims=True)
        m2 = jnp.max(jnp.where(io_g == i1, NEG_INF, sg), axis=0, keepdims=True)
        gs_rows.append(m1 + m2)
    gs = jnp.concatenate(gs_rows, axis=0)
    gio = lax.broadcasted_iota(I32, gs.shape, 0)
    gsel = jnp.zeros(gs.shape, F32)
    for _ in range(TOPK_GROUPS):
        mx = jnp.max(gs, axis=0, keepdims=True)
        ix = jnp.min(jnp.where(gs == mx, gio, BIG_I32), axis=0, keepdims=True)
        hit = gio == ix
        gsel = jnp.where(hit, 1.0, gsel)
        gs = jnp.where(hit, NEG_INF, gs)
    emask = jnp.concatenate([jnp.broadcast_to(gsel[g:g + 1, :], (ge, tm)) for g in range(N_EXPERT_GROUPS)], axis=0)
    cand = jnp.where(emask > 0.0, sel, NEG_INF)
    eio = lax.broadcasted_iota(I32, (E, tm), 0)
    idxs, gates = [], []
    for _ in range(TOP_K):
        mx = jnp.max(cand, axis=0, keepdims=True)
        ix = jnp.min(jnp.where(cand == mx, eio, BIG_I32), axis=0, keepdims=True)
        hit = eio == ix
        gates.append(jnp.sum(jnp.where(hit, scores, 0.0), axis=0, keepdims=True))
        idxs.append(ix)
        cand = jnp.where(hit, NEG_INF, cand)
    eidx = jnp.concatenate(idxs, axis=0)
    gate = jnp.concatenate(gates, axis=0)
    gate = gate / jnp.sum(gate, axis=0, keepdims=True) * ROUTED_SCALE
    return eidx, gate


def _post_kernel(x_ref, ol_ref, os_ref, gtm_ref, shf_ref, scf_ref, gtf_ref, wuv_ref, goa_ref, wout_ref,
                 gffn_ref, wrh_ref, wrl_ref, br_ref, wsg_ref, wsu_ref, wsd_ref,
                 h2_ref, base_ref, eidx_ref, gate_ref):
    x = x_ref[0]
    oa = jnp.concatenate([_dot(ol_ref[0, hd], wuv_ref[hd]) for hd in range(MLA_HEADS)], axis=1)
    oan = _rms(oa, goa_ref[...]).astype(BF16)
    wa = oan.shape[1]
    mix = _dot(oan, wout_ref[:wa, :]) + _dot(os_ref[0], wout_ref[wa:, :])
    x1 = x + gtm_ref[0] * mix
    h2 = _rms(x1, gffn_ref[...]) * (1.0 + scf_ref[0]) + shf_ref[0]
    h2_hi, h2_lo = _split(h2)
    for j, plane in enumerate(_pack_rows(h2)):
        h2_ref[j] = plane
    sh =_dot((_silu(_dot(h2_hi, wsg_ref[...])) * _dot(h2_hi, wsu_ref[...])).astype(BF16), wsd_ref[...])
    base_ref[0] = x1 + gtf_ref[0] * sh
    wr_hi = wrh_ref[...]
    logits = _dot_nt(wr_hi, h2_hi) + _dot_nt(wrl_ref[...], h2_hi) + _dot_nt(wr_hi, h2_lo)
    scores = jax.nn.sigmoid(logits)
    eidx, gate = _route(scores, scores + br_ref[...])
    eidx_ref[...] = eidx
    gate_ref[...] = gate


def _post(x, o_lat, o_s5, mods, wts, tm, b0=0, nb=None):
    _, L, D = x.shape
    nb = x.shape[0] if nb is None else nb
    nt = L // tm
    per_row = mods[0].shape[1] != 1
    mod_spec = (pl.BlockSpec((1, tm, D), lambda b, i: (b0 + b, i, 0)) if per_row
                else pl.BlockSpec((1, 1, D), lambda b, i: (b0 + b, 0, 0)))
    W = o_s5.shape[-1]
    consts = [wts["w_uv"], wts["g_out_attn"], wts["w_out"], wts["g_ffn"], wts["wr_hi"], wts["wr_lo"],
              wts["b_router"], wts["w_sg"], wts["w_su"], wts["w_sd"]]
    in_specs = [pl.BlockSpec((1, tm, D), lambda b, i: (b0 + b, i, 0)),
                pl.BlockSpec((1, MLA_HEADS, tm, KV_LORA), lambda b, i: (b0 + b, 0, i, 0)),
                pl.BlockSpec((1, tm, W), lambda b, i: (b0 + b, i, 0)),
                mod_spec, mod_spec, mod_spec, mod_spec] + [_const_spec(c.shape) for c in consts]
    planes = D // (2 * LANES)
    out_shape = (jax.ShapeDtypeStruct((planes, nb * L, LANES), U32), jax.ShapeDtypeStruct((nb, L, D), F32),
                 jax.ShapeDtypeStruct((TOP_K, nb * L), I32), jax.ShapeDtypeStruct((TOP_K, nb * L), F32))
    out_specs = (pl.BlockSpec((planes, tm, LANES), lambda b, i: (0, b * nt + i, 0)),
                 pl.BlockSpec((1, tm, D), lambda b, i: (b, i, 0)),
                 pl.BlockSpec((TOP_K, tm), lambda b, i: (0, b * nt + i)),
                 pl.BlockSpec((TOP_K, tm), lambda b, i: (0, b * nt + i)))
    return pl.pallas_call(
        _post_kernel, grid=(nb, nt), in_specs=in_specs, out_specs=out_specs, out_shape=out_shape,
        compiler_params=_params(("arbitrary", "arbitrary")), name="post",
    )(x, o_lat, o_s5, *mods, *consts)


def _rank_kernel(eidx_ref, tri_ref, rank_ref, cnt_ref, carry_scr):
    i = pl.program_id(0)

    @pl.when(i == 0)
    def _():
        carry_scr[...] = jnp.zeros(carry_scr.shape, F32)

    eidx = eidx_ref[...]
    tt = eidx.shape[1]
    eio = lax.broadcasted_iota(I32, (N_EXPERTS, tt), 0)
    hits = [eio == eidx[k:k + 1, :] for k in range(TOP_K)]
    onehot = jnp.zeros((N_EXPERTS, tt), F32)
    for hit in hits:
        onehot = jnp.where(hit, 1.0, onehot)
    before = _dot(onehot.astype(BF16), tri_ref[...]) + carry_scr[...]
    ranks = [jnp.sum(jnp.where(hit, before, 0.0), axis=0, keepdims=True) for hit in hits]
    rank_ref[...] = jnp.concatenate(ranks, axis=0).astype(I32)
    carry = carry_scr[...] + jnp.sum(onehot, axis=1, keepdims=True)
    carry_scr[...] = carry
    cnt_ref[...] = carry


def _rank(eidx, tt):
    K, T = eidx.shape
    r = np.arange(tt)
    tri = jnp.asarray((r[:, None] < r[None, :]).astype(np.float32), BF16)
    return pl.pallas_call(
        _rank_kernel, grid=(T // tt,),
        in_specs=[pl.BlockSpec((K, tt), lambda i: (0, i)), _const_spec((tt, tt))],
        out_specs=(pl.BlockSpec((K, tt), lambda i: (0, i)), _const_spec((N_EXPERTS, 1))),
        out_shape=(jax.ShapeDtypeStruct((K, T), I32), jax.ShapeDtypeStruct((N_EXPERTS, 1), F32)),
        scratch_shapes=[pltpu.VMEM((N_EXPERTS, 1), F32)],
        compiler_params=_params(("arbitrary",)), name="rank",
    )(eidx, tri)


def _pos_kernel(eidx_ref, rank_ref, pstart_ref, idx_ref, *, rows_pad):
    eidx = eidx_ref[...]
    tt = eidx.shape[1]
    eio = lax.broadcasted_iota(I32, (N_EXPERTS, tt), 0)
    pstart = pstart_ref[...]
    starts = [jnp.sum(jnp.where(eio == eidx[k:k + 1, :], pstart, 0.0), axis=0, keepdims=True)
              for k in range(TOP_K)]
    pos = jnp.concatenate(starts, axis=0).astype(I32) + rank_ref[...]
    planes = idx_ref.shape[1] // TOP_K
    for w in range(idx_ref.shape[0]):
        for j in range(planes):
            idx_ref[w, j * TOP_K:(j + 1) * TOP_K, :] = pos[:, w * SC_WINDOW:(w + 1) * SC_WINDOW] + j * rows_pad


def _pair_rows(eidx, rank, pstart, planes, rows_pad, tt):
    K, T = eidx.shape
    wpt = tt // SC_WINDOW
    return pl.pallas_call(
        functools.partial(_pos_kernel, rows_pad=rows_pad), grid=(T // tt,),
        in_specs=[pl.BlockSpec((K, tt), lambda i: (0, i)), pl.BlockSpec((K, tt), lambda i: (0, i)),
                  _const_spec((N_EXPERTS, 1))],
        out_specs=pl.BlockSpec((wpt, planes * K, SC_WINDOW), lambda i: (i, 0, 0)),
        out_shape=jax.ShapeDtypeStruct((T // SC_WINDOW, planes * K, SC_WINDOW), I32),
        compiler_params=_params(("arbitrary",)), name="pair_rows",
    )(eidx, rank, pstart)


def _expert_rows(x_ref, o_ref, wg_ref, wu_ref, wd_ref, nvalid, rows):
    planes = x_ref.shape[0]
    x = _unpack_rows([x_ref[j, :rows, :] for j in range(planes)])
    live = lax.broadcasted_iota(I32, x.shape, 0) < nvalid
    x = jnp.where(live, x, jnp.zeros_like(x))
    g = _dot(x, wg_ref[0].astype(BF16))
    u = _dot(x, wu_ref[0].astype(BF16))
    y = _dot((_silu(g) * u).astype(BF16), wd_ref[0].astype(BF16))
    for j, plane in enumerate(_pack_rows(y)):
        o_ref[j, :rows, :] = plane


def _expert_kernel(blk_e_ref, nvalid_ref, nused_ref, x_ref, wg_ref, wu_ref, wd_ref, o_ref):
    b = pl.program_id(0)
    used = b < nused_ref[0]
    nvalid = nvalid_ref[b]
    half = MOE_ROWS // 2

    @pl.when(jnp.logical_and(used, nvalid > half))
    def _():
        _expert_rows(x_ref, o_ref, wg_ref, wu_ref, wd_ref, nvalid, MOE_ROWS)

    @pl.when(jnp.logical_and(used, nvalid <= half))
    def _():
        _expert_rows(x_ref, o_ref, wg_ref, wu_ref, wd_ref, nvalid, half)
        o_ref[:, half:, :] = jnp.zeros((o_ref.shape[0], MOE_ROWS - half, LANES), o_ref.dtype)


def _experts(xs, blk_e, nvalid, nused, wg, wu, wd):
    planes, rows, _ = xs.shape
    nb = rows // MOE_ROWS
    _, D, F = wg.shape

    def xmap(b, blk_e_ref, nvalid_ref, nused_ref):
        return (0, jnp.minimum(b, nused_ref[0] - 1), 0)

    def wmap(b, blk_e_ref, nvalid_ref, nused_ref):
        return (blk_e_ref[b], 0, 0)

    grid_spec = pltpu.PrefetchScalarGridSpec(
        num_scalar_prefetch=3, grid=(nb,),
        in_specs=[pl.BlockSpec((planes, MOE_ROWS, LANES), xmap),
                  pl.BlockSpec((1, D, F), wmap), pl.BlockSpec((1, D, F), wmap), pl.BlockSpec((1, F, D), wmap)],
        out_specs=pl.BlockSpec((planes, MOE_ROWS, LANES), xmap))
    return pl.pallas_call(
        _expert_kernel, grid_spec=grid_spec, out_shape=jax.ShapeDtypeStruct(xs.shape, U32),
        compiler_params=_params(("arbitrary",)), name="experts",
    )(blk_e, nvalid, nused, xs, wg, wu, wd)


def _final_kernel(base_ref, ys_ref, gate_ref, gtf_ref, gfin_ref, o_ref):
    nwin, planes = ys_ref.shape[:2]
    gate = gate_ref[...]
    routed = jnp.zeros(base_ref.shape[1:], F32)
    for k in range(TOP_K):
        rows = jnp.concatenate([_unpack_rows([ys_ref[w, j, k] for j in range(planes)]) for w in range(nwin)], axis=0)
        routed = routed + gate[:, k:k + 1] * rows.astype(F32)
    y = base_ref[0] + gtf_ref[0] * routed
    o_ref[0] = _rms(y, gfin_ref[...])


def _final_kernel_aliased(base_ref, ys_ref, gate_ref, gtf_ref, gfin_ref, prev_ref, o_ref):
    _final_kernel(base_ref, ys_ref, gate_ref, gtf_ref, gfin_ref, o_ref)


def _final(base, ysg, gate_t, gt_f, g_final, tm, row0, b0=0, out_batches=None, y_prev=None):
    nb, L, D = base.shape
    out_batches = nb if out_batches is None else out_batches
    planes = ysg.shape[1]
    wpt = tm // SC_WINDOW
    nt = L // tm
    off = row0 // tm
    per_row = gt_f.shape[1] != 1
    mod_spec = (pl.BlockSpec((1, tm, D), lambda b, i: (b0 + b, i, 0)) if per_row
                else pl.BlockSpec((1, 1, D), lambda b, i: (b0 + b, 0, 0)))
    in_specs = [pl.BlockSpec((1, tm, D), lambda b, i: (b, i, 0)),
                pl.BlockSpec((wpt, planes, TOP_K, SC_WINDOW, LANES), lambda b, i: (off + b * nt + i, 0, 0, 0, 0)),
                pl.BlockSpec((tm, TOP_K), lambda b, i: (off + b * nt + i, 0)),
                mod_spec, _const_spec((1, D))]
    args = [base, ysg, gate_t, gt_f, g_final]
    kern = _final_kernel
    aliases = {}
    if y_prev is not None:
        in_specs.append(pl.BlockSpec(memory_space=pl.ANY))
        args.append(y_prev)
        aliases = {len(args) - 1: 0}
        kern = _final_kernel_aliased
    return pl.pallas_call(
        kern, grid=(nb, nt), in_specs=in_specs,
        out_specs=pl.BlockSpec((1, tm, D), lambda b, i: (b0 + b, i, 0)),
        out_shape=jax.ShapeDtypeStruct((out_batches, L, D), F32),
        input_output_aliases=aliases,
        compiler_params=_params(("arbitrary", "arbitrary")), name="final",
    )(*args)


def _sc_mesh():
    return plsc.VectorSubcoreMesh(core_axis_name="c", subcore_axis_name="s")


def _sc_worker():
    return lax.axis_index("s") * SC_CORES + lax.axis_index("c")


def _sc_dispatch(src, idx, out_rows, planes, n_tok):
    n_win = n_tok // SC_WINDOW
    n_iter = -(-n_win // SC_WORKERS)

    @functools.partial(
        pl.kernel, mesh=_sc_mesh(), out_type=jax.ShapeDtypeStruct((out_rows, LANES), src.dtype),
        scratch_types=[pltpu.VMEM((planes * TOP_K, SC_WINDOW), I32),
                       pltpu.VMEM((planes, SC_WINDOW, LANES), src.dtype),
                       pltpu.SemaphoreType.DMA((planes,)), pltpu.SemaphoreType.DMA],
        name="sc_dispatch")
    def k(src_hbm, idx_hbm, out_hbm, idx_v, rows_v, load_sem, scat_sem):
        wid = _sc_worker()

        @pl.loop(0, n_iter)
        def _(i):
            win = i * SC_WORKERS + wid

            @pl.when(win < n_win)
            def _():
                t0 = win * SC_WINDOW
                loads = [pltpu.make_async_copy(
                    src_hbm.at[pl.ds(pl.multiple_of(j * n_tok + t0, SC_WINDOW), SC_WINDOW)],
                    rows_v.at[j], load_sem.at[j]) for j in range(planes)]
                for ld in loads:
                    ld.start()
                pltpu.sync_copy(idx_hbm.at[win], idx_v)
                scatters = []
                for j in range(planes):
                    loads[j].wait()
                    for kk in range(TOP_K):
                        cp = pltpu.make_async_copy(rows_v.at[j], out_hbm.at[idx_v.at[j * TOP_K + kk]], scat_sem)
                        cp.start()
                        scatters.append(cp)
                for cp in scatters:
                    cp.wait()

    return k(src, idx)


def _sc_gather(table, idx):
    n_chunks, win = idx.shape
    n_iter = n_chunks // SC_WORKERS
    assert n_chunks % SC_WORKERS == 0 and n_iter % 2 == 0

    @functools.partial(
        pl.kernel, mesh=_sc_mesh(), out_type=jax.ShapeDtypeStruct((n_chunks * win, LANES), table.dtype),
        scratch_types=[pltpu.VMEM((n_iter, win), I32), pltpu.VMEM((2, win, LANES), table.dtype),
                       pltpu.SemaphoreType.DMA((2,)), pltpu.SemaphoreType.DMA((2,))],
        name="sc_gather")
    def k(table_hbm, idx_hbm, out_hbm, idx_v, rows_v, gather_sem, write_sem):
        wid = _sc_worker()
        c0 = wid * n_iter
        pltpu.sync_copy(idx_hbm.at[wid], idx_v)

        def gather(c, b):
            return pltpu.make_async_copy(table_hbm.at[idx_v.at[c]], rows_v.at[b], gather_sem.at[b])

        def write(c, b):
            return pltpu.make_async_copy(rows_v.at[b], out_hbm.at[pl.ds(pl.multiple_of((c0 + c) * win, win), win)],
                                         write_sem.at[b])

        gather(0, 0).start()

        @pl.loop(0, n_iter, step=2)
        def _(i):
            for b in range(2):
                c = i + b
                other = 1 - b

                @pl.when(c >= 1)
                def _():
                    write(c - 1, other).wait()

                @pl.when(c + 1 < n_iter)
                def _():
                    gather(c + 1, other).start()

                gather(c, b).wait()
                write(c, b).start()

        write(n_iter - 1, (n_iter - 1) % 2).wait()

    return k(table, idx.reshape(SC_WORKERS, n_iter, win))


def _moe_plan(eidx, planes):
    T = eidx.shape[1]
    rank, counts = _rank(eidx, 256)
    counts = counts.reshape(N_EXPERTS).astype(I32)
    nblk = (counts + MOE_ROWS - 1) // MOE_ROWS
    blk_end = jnp.cumsum(nblk)
    blk_start = blk_end - nblk
    nb_max = (T * TOP_K) // MOE_ROWS + N_EXPERTS
    nused = blk_end[-1]
    bidx = jnp.arange(nb_max, dtype=I32)
    blk_e = jnp.minimum(jnp.searchsorted(blk_end, bidx, side="right"), N_EXPERTS - 1).astype(I32)
    blk_e = jnp.where(bidx < nused, blk_e, blk_e[jnp.maximum(nused - 1, 0)])
    nvalid = jnp.clip(counts[blk_e] - (bidx - blk_start[blk_e]) * MOE_ROWS, 0, MOE_ROWS).astype(I32)
    rows_pad = nb_max * MOE_ROWS
    pstart = (blk_start * MOE_ROWS).astype(F32).reshape(N_EXPERTS, 1)
    idx = _pair_rows(eidx, rank, pstart, planes, rows_pad, 256)
    return idx, blk_e, nvalid, nused.reshape(1).astype(I32), rows_pad


def _moe_group(h2p, eidx, w_gate, w_up, w_down):
    planes, T, _ = h2p.shape
    assert (T * TOP_K) % MOE_ROWS == 0 and T % 256 == 0
    idx, blk_e, nvalid, nused, rows_pad = _moe_plan(eidx, planes)
    n_win = T // SC_WINDOW
    xs = _sc_dispatch(h2p.reshape(planes * T, LANES), idx, planes * rows_pad, planes, T)
    ys = _experts(xs.reshape(planes, rows_pad, LANES), blk_e, nvalid, nused, w_gate, w_up, w_down)
    ysg = _sc_gather(ys.reshape(planes * rows_pad, LANES), idx.reshape(n_win * planes * TOP_K, SC_WINDOW))
    return ysg.reshape(n_win, planes, TOP_K, SC_WINDOW, LANES)


def _rope_tables(pos):
    half = ROPE_DIM // 2
    inv = ROPE_THETA ** (-jnp.arange(half, dtype=F32) / half)
    ang = pos.astype(F32)[:, None] * inv[None, :]
    cos = jnp.tile(jnp.cos(ang), (1, 2 * MLA_HEADS))
    sin = jnp.tile(jnp.sin(ang), (1, 2 * MLA_HEADS))
    return cos, sin


def _rot_cols(w):
    half = ROPE_DIM // 2
    return jnp.concatenate([-w[..., half:], w[..., :half]], axis=-1)


def _layer_weights(w_in, g_mix, g_q, g_kv, w_uq, w_uk, w_uv, s5_D, s5_w_glu, s5_b_glu, g_out_attn, g_out_s5,
                   w_out, g_ffn, w_router, b_router, w_sh_gate, w_sh_up, w_sh_down):
    D = w_in.shape[0]
    o1, o2, o3 = Q_LORA, Q_LORA + KV_LORA, Q_LORA + KV_LORA + ROPE_DIM
    s5w = w_in.shape[1] - o3
    w_rope = w_in[:, o2:o3]
    w_in_ext = jnp.concatenate([w_in[:, :o2], w_in[:, o3:], w_rope, _rot_cols(w_rope)], axis=1).astype(BF16)
    wq = w_uq.reshape(Q_LORA, MLA_HEADS, NOPE_DIM + ROPE_DIM)
    wq_rope = wq[:, :, NOPE_DIM:]
    wr_hi, wr_lo = _split(w_router.T)
    return {
        "s5w": s5w,
        "g_mix": g_mix.reshape(1, D), "w_in": w_in_ext,
        "g_q": g_q.reshape(1, Q_LORA), "g_kv": g_kv.reshape(1, KV_LORA),
        "wq_nope": wq[:, :, :NOPE_DIM].reshape(Q_LORA, MLA_HEADS * NOPE_DIM).astype(BF16),
        "wq_rope": wq_rope.reshape(Q_LORA, MLA_HEADS * ROPE_DIM).astype(BF16),
        "wq_rot": _rot_cols(wq_rope).reshape(Q_LORA, MLA_HEADS * ROPE_DIM).astype(BF16),
        "w_uk": w_uk.reshape(KV_LORA, MLA_HEADS * NOPE_DIM).astype(BF16),
        "w_uv": jnp.transpose(w_uv, (1, 0, 2)).astype(BF16),
        "s5_d": s5_D.reshape(1, s5w), "w_glu": s5_w_glu.astype(BF16), "b_glu": s5_b_glu.reshape(1, s5w),
        "g_out_attn": g_out_attn.reshape(1, -1), "g_out_s5": g_out_s5.reshape(1, s5w),
        "w_out": w_out.astype(BF16), "g_ffn": g_ffn.reshape(1, D),
        "wr_hi": wr_hi, "wr_lo": wr_lo, "b_router": b_router.reshape(N_EXPERTS, 1),
        "w_sg": w_sh_gate.astype(BF16), "w_su": w_sh_up.astype(BF16), "w_sd": w_sh_down.astype(BF16),
    }


def _state_in(s_re, s_im):
    B = s_re.shape[0]
    return jnp.stack([s_re.reshape(B, -1), s_im.reshape(B, -1)], axis=1).astype(F32)


def kernel(x_prompt, x_sample, c_prompt, c_sample, cache_ckv, cache_krope, state_s5_re, state_s5_im, w_ada, b_ada, g_mix, g_ffn, w_in, g_q, w_uq, g_kv, w_uk, w_uv, s5_A_re, s5_A_im, s5_B_re, s5_B_im, s5_C_re, s5_C_im, s5_D, s5_log_dt, s5_w_glu, s5_b_glu, g_out_attn, g_out_s5, w_out, w_router, b_router, w_exp_gate, w_exp_up, w_exp_down, w_sh_gate, w_sh_up, w_sh_down, g_final):
    Bp, Lp, D = x_prompt.shape
    Bs, Ls, _ = x_sample.shape
    depth = w_ada.shape[0]
    assert depth == 1, "single-layer step"
    past = cache_ckv.shape[2]
    G, N = s5_A_re.shape[1:]
    Ts = Bs * Ls
    Tp = Bp * Lp
    l = 0

    wts = _layer_weights(w_in[l], g_mix[l], g_q[l], g_kv[l], w_uq[l], w_uk[l], w_uv[l], s5_D[l], s5_w_glu[l],
                         s5_b_glu[l], g_out_attn[l], g_out_s5[l], w_out[l], g_ffn[l], w_router[l], b_router[l],
                         w_sh_gate[l], w_sh_up[l], w_sh_down[l])

    mod = _ada(jnp.concatenate([c_prompt, c_sample], axis=0), w_ada[l], b_ada[l])
    mod_p = [m.reshape(Bp, 1, D) for m in jnp.split(mod[:Bp], 6, axis=-1)]
    mod_s = [jnp.broadcast_to(m[:, None, :], (Bs, Ls, D)).reshape(1, Ts, D)
             for m in jnp.split(mod[Bp:], 6, axis=-1)]

    tm_p = 256
    cos_p, sin_p = _rope_tables(jnp.arange(Lp))
    cos_s, sin_s = _rope_tables(jnp.tile(past + jnp.arange(Ls), Bs))

    ckv_p, kr_p, u_p, kcat_p, v_p, q_p = _pre(x_prompt, mod_p[0], mod_p[1], cos_p, sin_p, wts, tm_p)
    olat_p = _attn_prompt(q_p, kcat_p, v_p, 512)
    tabs_p = _s5_tables(s5_A_re[l], s5_A_im[l], s5_B_re[l], s5_B_im[l], s5_C_re[l], s5_C_im[l], s5_log_dt[l], tm_p)
    os5_p, hl_p = _s5(u_p, jnp.zeros((Bp, 2, G * N), F32), tabs_p, wts, tm_p, precise=False)

    xs_rows = x_sample.reshape(1, Ts, D)
    ckv_s, kr_s, u_s, kcat_s, v_s, q_s = _pre(xs_rows, mod_s[0], mod_s[1], cos_s, sin_s, wts, Ts)
    olat_s = _attn_sample(q_s, kcat_s, v_s, cache_ckv[l], cache_krope[l], wts["w_uk"])
    tabs_s = _s5_tables(s5_A_re[l], s5_A_im[l], s5_B_re[l], s5_B_im[l], s5_C_re[l], s5_C_im[l], s5_log_dt[l], Ls)
    os5_s, hl_s = _s5(u_s.reshape(Bs, Ls, -1), _state_in(state_s5_re[l], state_s5_im[l]), tabs_s, wts, Ls,
                      precise=True)
    h2_s, base_s, eidx_s, gate_s = _post(xs_rows, olat_s, os5_s.reshape(1, Ts, -1),
                                         [mod_s[2], mod_s[3], mod_s[4], mod_s[5]], wts, Ts)

    half = Bp // 2
    post_mods = [mod_p[2], mod_p[3], mod_p[4], mod_p[5]]
    experts_w = (w_exp_gate[l], w_exp_up[l], w_exp_down[l])
    h2_a, base_a, eidx_a, gate_a = _post(x_prompt, olat_p, os5_p, post_mods, wts, tm_p, 0, half)
    ysg_a = _moe_group(h2_a, eidx_a, *experts_w)
    h2_b, base_b, eidx_b, gate_b = _post(x_prompt, olat_p, os5_p, post_mods, wts, tm_p, half, Bp - half)
    ysg_b = _moe_group(jnp.concatenate([h2_b, h2_s], axis=1), jnp.concatenate([eidx_b, eidx_s], axis=1), *experts_w)
    gate_bs = jnp.concatenate([gate_b, gate_s], axis=1).T

    gfin = g_final.reshape(1, D)
    y_p = _final(base_a, ysg_a, gate_a.T, mod_p[5], gfin, tm_p, 0, 0, Bp)
    y_p = _final(base_b, ysg_b, gate_bs, mod_p[5], gfin, tm_p, 0, half, Bp, y_p)
    y_s = _final(base_s, ysg_b, gate_bs, mod_s[5], gfin, Ts, (Bp - half) * Lp).reshape(Bs, Ls, D)

    def state_out(hl, B):
        return hl[:, 0].reshape(1, B, G, N), hl[:, 1].reshape(1, B, G, N)

    sre_p, sim_p = state_out(hl_p, Bp)
    sre_s, sim_s = state_out(hl_s, Bs)
    return (y_p, y_s, ckv_p[None], kr_p[None], sre_p, sim_p,
            ckv_s.reshape(1, Bs, Ls, KV_LORA), kr_s.reshape(1, Bs, Ls, ROPE_DIM), sre_s, sim_s)
```

```python
import functools
import math

import numpy as np
import jax
import jax.numpy as jnp
from jax import lax
from jax.experimental import pallas as pl
from jax.experimental.pallas import tpu as pltpu
from jax.experimental.pallas import tpu_sc as plsc

F32 = jnp.float32
BF16 = jnp.bfloat16
I32 = jnp.int32
U32 = jnp.uint32

EPS = 1e-6
CHUNK = 64
MLA_HEADS = 4
NOPE_DIM = 128
ROPE_DIM = 64
V_DIM = 128
Q_LORA = 256
KV_LORA = 256
QK_HEAD = NOPE_DIM + ROPE_DIM
ROPE_THETA = 10000.0
S5_GROUP_CH = 16
S5_STATE = 64
N_EXPERTS = 256
TOP_K = 8
N_EXPERT_GROUPS = 8
TOPK_GROUPS = 4
ROUTED_SCALE = 2.5

S5_SUB = 16
MOE_ROWS = 512
LANES = 128
VMEM_LIMIT = 56 * 1024 * 1024
SC_CORES = 2
SC_SUBCORES = 16
SC_WORKERS = SC_CORES * SC_SUBCORES
SC_WINDOW = 128

NEG_INF = float("-inf")
BIG_I32 = 1 << 30


def _dot(a, b):
    return jnp.dot(a, b, preferred_element_type=F32)


def _dot_nt(a, b):
    return lax.dot_general(a, b, (((1,), (1,)), ((), ())), preferred_element_type=F32)


def _split(a):
    hi = a.astype(BF16)
    lo = (a - hi.astype(F32)).astype(BF16)
    return hi, lo


def _lane_tile(x, n):
    return jnp.concatenate([x] * n, axis=1)


def _rms(x, g):
    return x * lax.rsqrt(jnp.mean(x * x, axis=-1, keepdims=True) + EPS) * g


def _silu(x):
    return x * jax.nn.sigmoid(x)


def _pack_rows(x):
    half = x.shape[1] // 2
    hi = lax.bitcast_convert_type(x[:, :half].astype(BF16).astype(F32), U32)
    lo = lax.bitcast_convert_type(x[:, half:].astype(BF16).astype(F32), U32)
    w = hi | (lo >> 16)
    return [w[:, j * LANES:(j + 1) * LANES] for j in range(half // LANES)]


def _unpack_rows(planes):
    his = [lax.bitcast_convert_type(p & jnp.uint32(0xFFFF0000), F32).astype(BF16) for p in planes]
    los = [lax.bitcast_convert_type(p << 16, F32).astype(BF16) for p in planes]
    return jnp.concatenate(his + los, axis=1)


def _params(sem):
    return pltpu.CompilerParams(dimension_semantics=sem, vmem_limit_bytes=VMEM_LIMIT)


def _const_spec(shape):
    nd = len(shape)
    return pl.BlockSpec(shape, lambda *_: (0,) * nd)


def _ada_kernel(c_ref, whi_ref, wlo_ref, b_ref, o_ref):
    c = c_ref[...]
    s_hi, s_lo = _split(_silu(c))
    w_hi = whi_ref[...]
    o_ref[...] = _dot(s_hi, w_hi) + _dot(s_hi, wlo_ref[...]) + _dot(s_lo, w_hi) + b_ref[...]


def _ada(c, w_ada, b_ada):
    rows, d = c.shape
    n = w_ada.shape[1]
    tn = 512
    w_hi, w_lo = _split(w_ada)
    return pl.pallas_call(
        _ada_kernel,
        grid=(n // tn,),
        in_specs=[_const_spec((rows, d)),
                  pl.BlockSpec((d, tn), lambda j: (0, j)),
                  pl.BlockSpec((d, tn), lambda j: (0, j)),
                  pl.BlockSpec((1, tn), lambda j: (0, j))],
        out_specs=pl.BlockSpec((rows, tn), lambda j: (0, j)),
        out_shape=jax.ShapeDtypeStruct((rows, n), F32),
        compiler_params=_params(("arbitrary",)),
        name="ada",
    )(c, w_hi, w_lo, b_ada.reshape(1, n))


def _pre_kernel(x_ref, sh_ref, sc_ref, g_ref, win_ref, gq_ref, gkv_ref, wqn_ref, wqr_ref, wqt_ref,
                wuk_ref, cos_ref, sin_ref, ckv_ref, kr_ref, u_ref, kcat_ref, v_ref, q_ref, *, scale):
    x = x_ref[0]
    h = _rms(x, g_ref[...]) * (1.0 + sc_ref[0]) + sh_ref[0]
    z = _dot(h.astype(BF16), win_ref[...])
    cq = _rms(z[:, :Q_LORA], gq_ref[...])
    ckv = _rms(z[:, Q_LORA:Q_LORA + KV_LORA], gkv_ref[...])
    o_s5 = Q_LORA + KV_LORA
    s5w = u_ref.shape[-1]
    u_ref[0] = z[:, o_s5:o_s5 + s5w]
    o_r = o_s5 + s5w
    cos = cos_ref[...]
    sin = sin_ref[...]
    kr = z[:, o_r:o_r + ROPE_DIM] * cos[:, :ROPE_DIM] + z[:, o_r + ROPE_DIM:o_r + 2 * ROPE_DIM] * sin[:, :ROPE_DIM]
    ckv_ref[0] = ckv
    kr_ref[0] = kr
    ckvb = ckv.astype(BF16)
    krb = kr.astype(BF16)
    v_ref[0] = ckvb
    kn = _dot(ckvb, wuk_ref[...])
    cqb = cq.astype(BF16)
    qn = _dot(cqb, wqn_ref[...]) * scale
    qr = (_dot(cqb, wqr_ref[...]) * cos + _dot(cqb, wqt_ref[...]) * sin) * scale
    for hd in range(MLA_HEADS):
        kcat_ref[0, hd, :, :NOPE_DIM] = kn[:, hd * NOPE_DIM:(hd + 1) * NOPE_DIM].astype(BF16)
        kcat_ref[0, hd, :, NOPE_DIM:] = krb
        q_ref[0, hd, :, :NOPE_DIM] = qn[:, hd * NOPE_DIM:(hd + 1) * NOPE_DIM].astype(BF16)
        q_ref[0, hd, :, NOPE_DIM:] = qr[:, hd * ROPE_DIM:(hd + 1) * ROPE_DIM].astype(BF16)


def _pre(x, shift, scale_mod, cos_t, sin_t, wts, tm):
    B, L, D = x.shape
    nt = L // tm
    per_row = shift.shape[1] != 1
    mod_spec = (pl.BlockSpec((1, tm, D), lambda b, i: (b, i, 0)) if per_row
                else pl.BlockSpec((1, 1, D), lambda b, i: (b, 0, 0)))
    s5w = wts["s5w"]
    hr = MLA_HEADS * ROPE_DIM
    kern = functools.partial(_pre_kernel, scale=QK_HEAD ** -0.5 * math.log2(math.e))
    consts = [wts["g_mix"], wts["w_in"], wts["g_q"], wts["g_kv"], wts["wq_nope"], wts["wq_rope"],
              wts["wq_rot"], wts["w_uk"]]
    in_specs = [pl.BlockSpec((1, tm, D), lambda b, i: (b, i, 0)), mod_spec, mod_spec]
    in_specs += [_const_spec(c.shape) for c in consts]
    in_specs += [pl.BlockSpec((tm, hr), lambda b, i: (i, 0)), pl.BlockSpec((tm, hr), lambda b, i: (i, 0))]
    out_shape = (jax.ShapeDtypeStruct((B, L, KV_LORA), F32),
                 jax.ShapeDtypeStruct((B, L, ROPE_DIM), F32),
                 jax.ShapeDtypeStruct((B, L, s5w), F32),
                 jax.ShapeDtypeStruct((B, MLA_HEADS, L, QK_HEAD), BF16),
                 jax.ShapeDtypeStruct((B, L, KV_LORA), BF16),
                 jax.ShapeDtypeStruct((B, MLA_HEADS, L, QK_HEAD), BF16))
    out_specs = (pl.BlockSpec((1, tm, KV_LORA), lambda b, i: (b, i, 0)),
                 pl.BlockSpec((1, tm, ROPE_DIM), lambda b, i: (b, i, 0)),
                 pl.BlockSpec((1, tm, s5w), lambda b, i: (b, i, 0)),
                 pl.BlockSpec((1, MLA_HEADS, tm, QK_HEAD), lambda b, i: (b, 0, i, 0)),
                 pl.BlockSpec((1, tm, KV_LORA), lambda b, i: (b, i, 0)),
                 pl.BlockSpec((1, MLA_HEADS, tm, QK_HEAD), lambda b, i: (b, 0, i, 0)))
    return pl.pallas_call(
        kern, grid=(B, nt), in_specs=in_specs, out_specs=out_specs, out_shape=out_shape,
        compiler_params=_params(("arbitrary", "arbitrary")), name="pre",
    )(x, shift, scale_mod, *consts, cos_t, sin_t)


def _attn_kernel(q_ref, k_ref, v_ref, o_ref, m_scr, l_scr, acc_scr, *, t):
    i = pl.program_id(1)
    m_scr[...] = jnp.full(m_scr.shape, NEG_INF, F32)
    l_scr[...] = jnp.zeros(l_scr.shape, F32)
    acc_scr[...] = jnp.zeros(acc_scr.shape, F32)
    visible = (lax.broadcasted_iota(I32, (t, t), 1) // CHUNK) <= (lax.broadcasted_iota(I32, (t, t), 0) // CHUNK)

    def step(j0, masked):
        v = v_ref[0, pl.ds(j0, t), :]
        for hd in range(MLA_HEADS):
            rs = slice(hd * t, (hd + 1) * t)
            s = _dot_nt(q_ref[0, hd], k_ref[0, hd, pl.ds(j0, t), :])
            if masked:
                s = jnp.where(visible, s, NEG_INF)
            m_prev = m_scr[rs]
            m_next = jnp.maximum(m_prev, jnp.max(s, axis=1, keepdims=True))
            alpha = jnp.exp2(m_prev - m_next)
            p = jnp.exp2(s - _lane_tile(m_next, t // LANES))
            l_scr[rs] = alpha * l_scr[rs] + jnp.sum(p, axis=1, keepdims=True)
            m_scr[rs] = m_next
            acc_scr[rs] = acc_scr[rs] * _lane_tile(alpha, KV_LORA // LANES) + _dot(p.astype(BF16), v)

    def body(j, carry):
        step(pl.multiple_of(j * t, t), False)
        return carry

    lax.fori_loop(0, i, body, 0)
    step(pl.multiple_of(i * t, t), True)
    for hd in range(MLA_HEADS):
        rs = slice(hd * t, (hd + 1) * t)
        inv = 1.0 / l_scr[rs]
        o_ref[0, hd] = (acc_scr[rs] * _lane_tile(inv, KV_LORA // LANES)).astype(BF16)


def _attn_prompt(q, kcat, v, t):
    B, H, L, _ = q.shape
    assert L % t == 0 and t % CHUNK == 0
    rows = H * t
    kern = functools.partial(_attn_kernel, t=t)
    resident = pl.Buffered(1)
    return pl.pallas_call(
        kern, grid=(B, L // t),
        in_specs=[pl.BlockSpec((1, H, t, QK_HEAD), lambda b, i: (b, 0, i, 0)),
                  pl.BlockSpec((1, H, L, QK_HEAD), lambda b, i: (b, 0, 0, 0), pipeline_mode=resident),
                  pl.BlockSpec((1, L, KV_LORA), lambda b, i: (b, 0, 0), pipeline_mode=resident)],
        out_specs=pl.BlockSpec((1, H, t, KV_LORA), lambda b, i: (b, 0, i, 0)),
        out_shape=jax.ShapeDtypeStruct((B, H, L, KV_LORA), BF16),
        scratch_shapes=[pltpu.VMEM((rows, LANES), F32), pltpu.VMEM((rows, LANES), F32),
                        pltpu.VMEM((rows, KV_LORA), F32)],
        compiler_params=_params(("arbitrary", "arbitrary")), name="attn_prompt",
    )(q, kcat, v)


def _attn_sample_kernel(q_ref, kn_ref, vn_ref, pc_ref, pr_ref, wuk_ref, o_ref, *, past, lq):
    pc = pc_ref[0].astype(BF16)
    pr = pr_ref[0].astype(BF16)
    kp = _dot(pc, wuk_ref[...]).astype(BF16)
    vn = vn_ref[0]
    qchunk_p = (past + lax.broadcasted_iota(I32, (lq, past), 0)) // CHUNK
    vis_p = lax.broadcasted_iota(I32, (lq, past), 1) // CHUNK <= qchunk_p
    qchunk_n = (past + lax.broadcasted_iota(I32, (lq, lq), 0)) // CHUNK
    vis_n = (past + lax.broadcasted_iota(I32, (lq, lq), 1)) // CHUNK <= qchunk_n
    for hd in range(MLA_HEADS):
        q = q_ref[0, hd]
        s_p = (_dot_nt(q[:, :NOPE_DIM], kp[:, hd * NOPE_DIM:(hd + 1) * NOPE_DIM])
               + _dot_nt(q[:, NOPE_DIM:], pr))
        s_n = _dot_nt(q, kn_ref[0, hd])
        s_p = jnp.where(vis_p, s_p, NEG_INF)
        s_n = jnp.where(vis_n, s_n, NEG_INF)
        m = jnp.maximum(jnp.max(s_p, axis=1, keepdims=True), jnp.max(s_n, axis=1, keepdims=True))
        p_p = jnp.exp2(s_p - m)
        p_n = jnp.exp2(s_n - m)
        l = jnp.sum(p_p, axis=1, keepdims=True) + jnp.sum(p_n, axis=1, keepdims=True)
        o = _dot(p_p.astype(BF16), pc) + _dot(p_n.astype(BF16), vn)
        o_ref[0, hd] = (o / l).astype(BF16)


def _attn_sample(q, kcat, v, past_ckv, past_kr, w_uk):
    B, past, _ = past_ckv.shape
    H = MLA_HEADS
    lq = q.shape[2] // B
    kern = functools.partial(_attn_sample_kernel, past=past, lq=lq)
    return pl.pallas_call(
        kern, grid=(B,),
        in_specs=[pl.BlockSpec((1, H, lq, QK_HEAD), lambda b: (0, 0, b, 0)),
                  pl.BlockSpec((1, H, lq, QK_HEAD), lambda b: (0, 0, b, 0)),
                  pl.BlockSpec((1, lq, KV_LORA), lambda b: (0, b, 0)),
                  pl.BlockSpec((1, past, KV_LORA), lambda b: (b, 0, 0)),
                  pl.BlockSpec((1, past, ROPE_DIM), lambda b: (b, 0, 0)),
                  _const_spec(w_uk.shape)],
        out_specs=pl.BlockSpec((1, H, lq, KV_LORA), lambda b: (0, 0, b, 0)),
        out_shape=jax.ShapeDtypeStruct((1, H, B * lq, KV_LORA), BF16),
        compiler_params=_params(("arbitrary",)), name="attn_sample",
    )(q, kcat, v, past_ckv, past_kr, w_uk)


def _s5_kernel(u_ref, h0_ref, bre_ref, bim_ref, brel_ref, biml_ref, lt_ref, pinr_ref, pini_ref,
               pwr_ref, pwi_ref, a_ref, cre_ref, cim_ref, d_ref, wglu_ref, bglu_ref, gout_ref,
               o_ref, hl_ref, st_scr, cum_scr, hs_scr, *, tm, precise):
    i = pl.program_id(1)
    ns = st_scr.shape[1]
    half = ns // 2
    wh = u_ref.shape[-1] // 2

    @pl.when(i == 0)
    def _():
        st_scr[...] = h0_ref[0]

    u = u_ref[0]
    lt = lt_ref[...]
    for hf in range(2):
        uh = u[:, hf * wh:(hf + 1) * wh]
        if precise:
            u_hi, u_lo = _split(uh)
            bu_re = _dot(u_hi, bre_ref[hf]) + _dot(u_lo, bre_ref[hf]) + _dot(u_hi, brel_ref[hf])
            bu_im = _dot(u_hi, bim_ref[hf]) + _dot(u_lo, bim_ref[hf]) + _dot(u_hi, biml_ref[hf])
        else:
            u_hi = uh.astype(BF16)
            bu_re = _dot(u_hi, bre_ref[hf])
            bu_im = _dot(u_hi, bim_ref[hf])
        sl = slice(hf * half, (hf + 1) * half)
        pr = pinr_ref[:, sl]
        pi = pini_ref[:, sl]
        v_re = pr * bu_re - pi * bu_im
        v_im = pr * bu_im + pi * bu_re
        for part, v in ((0, v_re), (1, v_im)):
            if precise:
                v_hi, v_lo = _split(v)
                c = _dot(lt, v_hi) + _dot(lt, v_lo)
            else:
                c = _dot(lt, v.astype(BF16))
            cum_scr[:, part * ns + hf * half:part * ns + (hf + 1) * half] = c

    a_re = a_ref[0:1, :]
    a_im = a_ref[1:2, :]
    pw_re = pwr_ref[...]
    pw_im = pwi_ref[...]

    def chunk(c, carry):
        s_re, s_im = carry
        r0 = pl.multiple_of(c * S5_SUB, S5_SUB)
        t_re = cum_scr[pl.ds(r0, S5_SUB), 0:ns] + (a_re * s_re - a_im * s_im)
        t_im = cum_scr[pl.ds(r0, S5_SUB), ns:2 * ns] + (a_re * s_im + a_im * s_re)
        h_re = pw_re * t_re - pw_im * t_im
        h_im = pw_re * t_im + pw_im * t_re
        hs_scr[pl.ds(r0, S5_SUB), 0:ns] = h_re.astype(BF16)
        hs_scr[pl.ds(r0, S5_SUB), ns:2 * ns] = h_im.astype(BF16)
        return h_re[S5_SUB - 1:S5_SUB, :], h_im[S5_SUB - 1:S5_SUB, :]

    s_re, s_im = lax.fori_loop(0, tm // S5_SUB, chunk, (st_scr[0:1, :], st_scr[1:2, :]))
    st_scr[0:1, :] = s_re
    st_scr[1:2, :] = s_im
    hl_ref[0, 0:1, :] = s_re
    hl_ref[0, 1:2, :] = s_im

    ys = []
    for hf in range(2):
        hre = hs_scr[:, hf * half:(hf + 1) * half]
        him = hs_scr[:, ns + hf * half:ns + (hf + 1) * half]
        ys.append(_dot(hre, cre_ref[hf]) + _dot(him, cim_ref[hf]))
    y = jnp.concatenate(ys, axis=1) + d_ref[...] * u
    zg = jax.nn.gelu(y)
    gl = _dot(zg.astype(BF16), wglu_ref[...]) + bglu_ref[...]
    o = zg * jax.nn.sigmoid(gl)
    o_ref[0] = _rms(o, gout_ref[...]).astype(BF16)


def _s5(u, h0, tabs, wts, tm, precise):
    B, L, W = u.shape
    ns = h0.shape[-1]
    consts = [tabs["b_re"], tabs["b_im"], tabs["b_re_lo"], tabs["b_im_lo"], tabs["lt"], tabs["pin_re"],
              tabs["pin_im"], tabs["pw_re"], tabs["pw_im"], tabs["a"], tabs["c_re"], tabs["c_im"],
              wts["s5_d"], wts["w_glu"], wts["b_glu"], wts["g_out_s5"]]
    kern = functools.partial(_s5_kernel, tm=tm, precise=precise)
    return pl.pallas_call(
        kern, grid=(B, L // tm),
        in_specs=[pl.BlockSpec((1, tm, W), lambda b, i: (b, i, 0)),
                  pl.BlockSpec((1, 2, ns), lambda b, i: (b, 0, 0))] + [_const_spec(c.shape) for c in consts],
        out_specs=(pl.BlockSpec((1, tm, W), lambda b, i: (b, i, 0)),
                   pl.BlockSpec((1, 2, ns), lambda b, i: (b, 0, 0))),
        out_shape=(jax.ShapeDtypeStruct((B, L, W), BF16), jax.ShapeDtypeStruct((B, 2, ns), F32)),
        scratch_shapes=[pltpu.VMEM((2, ns), F32), pltpu.VMEM((tm, 2 * ns), F32), pltpu.VMEM((tm, 2 * ns), BF16)],
        compiler_params=_params(("arbitrary", "arbitrary")), name="s5",
    )(u, h0, *consts)


def _s5_tables(a_re_p, a_im_p, b_re_p, b_im_p, c_re_p, c_im_p, log_dt, tm):
    G, N = a_re_p.shape
    CH = b_re_p.shape[-1]
    dt = jnp.exp(log_dt.astype(F32))[:, None]
    lr = a_re_p.astype(F32) * dt
    li = a_im_p.astype(F32) * dt
    er = jnp.exp(lr)
    ab_re, ab_im = er * jnp.cos(li), er * jnp.sin(li)
    lam2 = a_re_p.astype(F32) ** 2 + a_im_p.astype(F32) ** 2
    nr, ni = ab_re - 1.0, ab_im
    f_re = (nr * a_re_p + ni * a_im_p) / lam2
    f_im = (ni * a_re_p - nr * a_im_p) / lam2
    bb_re = f_re[..., None] * b_re_p - f_im[..., None] * b_im_p
    bb_im = f_re[..., None] * b_im_p + f_im[..., None] * b_re_p
    gh = G // 2
    eye = jnp.eye(gh, dtype=F32)

    def blk_b(bb):
        t = bb.reshape(2, gh, N, CH)
        m = jnp.einsum("hgnc,gk->hgckn", t, eye)
        return m.reshape(2, gh * CH, gh * N)

    def blk_c(cc):
        t = cc.reshape(2, gh, CH, N)
        m = jnp.einsum("hgcn,gk->hgnkc", t, eye)
        return m.reshape(2, gh * N, gh * CH)

    b_re_m, b_im_m = blk_b(bb_re), blk_b(bb_im)
    b_re_hi, b_re_lo = _split(b_re_m)
    b_im_hi, b_im_lo = _split(b_im_m)
    s = jnp.arange(S5_SUB, dtype=F32)[:, None, None]

    def powers(sign):
        e = jnp.exp(sign * lr[None] * s)
        return ((e * jnp.cos(sign * li[None] * s)).reshape(S5_SUB, G * N),
                (e * jnp.sin(sign * li[None] * s)).reshape(S5_SUB, G * N))

    pin_re, pin_im = powers(-1.0)
    pw_re, pw_im = powers(1.0)
    reps = tm // S5_SUB
    r = np.arange(tm)
    lt = ((r[:, None] // S5_SUB == r[None, :] // S5_SUB) & (r[None, :] <= r[:, None])).astype(np.float32)
    return {
        "b_re": b_re_hi, "b_im": b_im_hi, "b_re_lo": b_re_lo, "b_im_lo": b_im_lo,
        "lt": jnp.asarray(lt, BF16),
        "pin_re": jnp.tile(pin_re, (reps, 1)), "pin_im": jnp.tile(pin_im, (reps, 1)),
        "pw_re": pw_re, "pw_im": pw_im,
        "a": jnp.stack([ab_re.reshape(G * N), ab_im.reshape(G * N)]),
        "c_re": blk_c(c_re_p.astype(F32)).astype(BF16), "c_im": (-blk_c(c_im_p.astype(F32))).astype(BF16),
    }


def _route(scores, sel):
    E, tm = scores.shape
    ge = E // N_EXPERT_GROUPS
    io_g = lax.broadcasted_iota(I32, (ge, tm), 0)
    gs_rows = []
    for g in range(N_EXPERT_GROUPS):
        sg = sel[g * ge:(g + 1) * ge, :]
        m1 = jnp.max(sg, axis=0, keepdims=True)
        i1 = jnp.min(jnp.where(sg == m1, io_g, BIG_I32), axis=0, keepdims=True)
        m2 = jnp.max(jnp.where(io_g == i1, NEG_INF, sg), axis=0, keepdims=True)
        gs_rows.append(m1 + m2)
    gs = jnp.concatenate(gs_rows, axis=0)
    gio = lax.broadcasted_iota(I32, gs.shape, 0)
    gsel = jnp.zeros(gs.shape, F32)
    for _ in range(TOPK_GROUPS):
        mx = jnp.max(gs, axis=0, keepdims=True)
        ix = jnp.min(jnp.where(gs == mx, gio, BIG_I32), axis=0, keepdims=True)
        hit = gio == ix
        gsel = jnp.where(hit, 1.0, gsel)
        gs = jnp.where(hit, NEG_INF, gs)
    emask = jnp.concatenate([jnp.broadcast_to(gsel[g:g + 1, :], (ge, tm)) for g in range(N_EXPERT_GROUPS)], axis=0)
    cand = jnp.where(emask > 0.0, sel, NEG_INF)
    eio = lax.broadcasted_iota(I32, (E, tm), 0)
    idxs, gates = [], []
    for _ in range(TOP_K):
        mx = jnp.max(cand, axis=0, keepdims=True)
        ix = jnp.min(jnp.where(cand == mx, eio, BIG_I32), axis=0, keepdims=True)
        hit = eio == ix
        gates.append(jnp.sum(jnp.where(hit, scores, 0.0), axis=0, keepdims=True))
        idxs.append(ix)
        cand = jnp.where(hit, NEG_INF, cand)
    eidx = jnp.concatenate(idxs, axis=0)
    gate = jnp.concatenate(gates, axis=0)
    gate = gate / jnp.sum(gate, axis=0, keepdims=True) * ROUTED_SCALE
    return eidx, gate


def _post_kernel(x_ref, ol_ref, os_ref, gtm_ref, shf_ref, scf_ref, gtf_ref, wuv_ref, goa_ref, wout_ref,
                 gffn_ref, wrh_ref, wrl_ref, br_ref, wsg_ref, wsu_ref, wsd_ref,
                 h2_ref, base_ref, eidx_ref, gate_ref):
    x = x_ref[0]
    oa = jnp.concatenate([_dot(ol_ref[0, hd], wuv_ref[hd]) for hd in range(MLA_HEADS)], axis=1)
    oan = _rms(oa, goa_ref[...]).astype(BF16)
    wa = oan.shape[1]
    mix = _dot(oan, wout_ref[:wa, :]) + _dot(os_ref[0], wout_ref[wa:, :])
    x1 = x + gtm_ref[0] * mix
    h2 = _rms(x1, gffn_ref[...]) * (1.0 + scf_ref[0]) + shf_ref[0]
    h2_hi, h2_lo = _split(h2)
    for j, plane in enumerate(_pack_rows(h2)):
        h2_ref[j] = plane
    sh =_dot((_silu(_dot(h2_hi, wsg_ref[...])) * _dot(h2_hi, wsu_ref[...])).astype(BF16), wsd_ref[...])
    base_ref[0] = x1 + gtf_ref[0] * sh
    wr_hi = wrh_ref[...]
    logits = _dot_nt(wr_hi, h2_hi) + _dot_nt(wrl_ref[...], h2_hi) + _dot_nt(wr_hi, h2_lo)
    scores = jax.nn.sigmoid(logits)
    eidx, gate = _route(scores, scores + br_ref[...])
    eidx_ref[...] = eidx
    gate_ref[...] = gate


def _post(x, o_lat, o_s5, mods, wts, tm):
    nb, L, D = x.shape
    nt = L // tm
    per_row = mods[0].shape[1] != 1
    mod_spec = (pl.BlockSpec((1, tm, D), lambda b, i: (b, i, 0)) if per_row
                else pl.BlockSpec((1, 1, D), lambda b, i: (b, 0, 0)))
    W = o_s5.shape[-1]
    consts = [wts["w_uv"], wts["g_out_attn"], wts["w_out"], wts["g_ffn"], wts["wr_hi"], wts["wr_lo"],
              wts["b_router"], wts["w_sg"], wts["w_su"], wts["w_sd"]]
    in_specs = [pl.BlockSpec((1, tm, D), lambda b, i: (b, i, 0)),
                pl.BlockSpec((1, MLA_HEADS, tm, KV_LORA), lambda b, i: (b, 0, i, 0)),
                pl.BlockSpec((1, tm, W), lambda b, i: (b, i, 0)),
                mod_spec, mod_spec, mod_spec, mod_spec] + [_const_spec(c.shape) for c in consts]
    planes = D // (2 * LANES)
    out_shape = (jax.ShapeDtypeStruct((planes, nb * L, LANES), U32), jax.ShapeDtypeStruct((nb, L, D), F32),
                 jax.ShapeDtypeStruct((TOP_K, nb * L), I32), jax.ShapeDtypeStruct((TOP_K, nb * L), F32))
    out_specs = (pl.BlockSpec((planes, tm, LANES), lambda b, i: (0, b * nt + i, 0)),
                 pl.BlockSpec((1, tm, D), lambda b, i: (b, i, 0)),
                 pl.BlockSpec((TOP_K, tm), lambda b, i: (0, b * nt + i)),
                 pl.BlockSpec((TOP_K, tm), lambda b, i: (0, b * nt + i)))
    return pl.pallas_call(
        _post_kernel, grid=(nb, nt), in_specs=in_specs, out_specs=out_specs, out_shape=out_shape,
        compiler_params=_params(("arbitrary", "arbitrary")), name="post",
    )(x, o_lat, o_s5, *mods, *consts)


def _rank_kernel(eidx_ref, tri_ref, rank_ref, cnt_ref, carry_scr):
    i = pl.program_id(0)

    @pl.when(i == 0)
    def _():
        carry_scr[...] = jnp.zeros(carry_scr.shape, F32)

    eidx = eidx_ref[...]
    tt = eidx.shape[1]
    eio = lax.broadcasted_iota(I32, (N_EXPERTS, tt), 0)
    hits = [eio == eidx[k:k + 1, :] for k in range(TOP_K)]
    onehot = jnp.zeros((N_EXPERTS, tt), F32)
    for hit in hits:
        onehot = jnp.where(hit, 1.0, onehot)
    before = _dot(onehot.astype(BF16), tri_ref[...]) + carry_scr[...]
    ranks = [jnp.sum(jnp.where(hit, before, 0.0), axis=0, keepdims=True) for hit in hits]
    rank_ref[...] = jnp.concatenate(ranks, axis=0).astype(I32)
    carry = carry_scr[...] + jnp.sum(onehot, axis=1, keepdims=True)
    carry_scr[...] = carry
    cnt_ref[...] = carry


def _rank(eidx, tt):
    K, T = eidx.shape
    r = np.arange(tt)
    tri = jnp.asarray((r[:, None] < r[None, :]).astype(np.float32), BF16)
    return pl.pallas_call(
        _rank_kernel, grid=(T // tt,),
        in_specs=[pl.BlockSpec((K, tt), lambda i: (0, i)), _const_spec((tt, tt))],
        out_specs=(pl.BlockSpec((K, tt), lambda i: (0, i)), _const_spec((N_EXPERTS, 1))),
        out_shape=(jax.ShapeDtypeStruct((K, T), I32), jax.ShapeDtypeStruct((N_EXPERTS, 1), F32)),
        scratch_shapes=[pltpu.VMEM((N_EXPERTS, 1), F32)],
        compiler_params=_params(("arbitrary",)), name="rank",
    )(eidx, tri)


def _pos_kernel(eidx_ref, rank_ref, pstart_ref, idx_ref, *, rows_pad):
    eidx = eidx_ref[...]
    tt = eidx.shape[1]
    eio = lax.broadcasted_iota(I32, (N_EXPERTS, tt), 0)
    pstart = pstart_ref[...]
    starts = [jnp.sum(jnp.where(eio == eidx[k:k + 1, :], pstart, 0.0), axis=0, keepdims=True)
              for k in range(TOP_K)]
    pos = jnp.concatenate(starts, axis=0).astype(I32) + rank_ref[...]
    planes = idx_ref.shape[1] // TOP_K
    for w in range(idx_ref.shape[0]):
        for j in range(planes):
            idx_ref[w, j * TOP_K:(j + 1) * TOP_K, :] = pos[:, w * SC_WINDOW:(w + 1) * SC_WINDOW] + j * rows_pad


def _pair_rows(eidx, rank, pstart, planes, rows_pad, tt):
    K, T = eidx.shape
    wpt = tt // SC_WINDOW
    return pl.pallas_call(
        functools.partial(_pos_kernel, rows_pad=rows_pad), grid=(T // tt,),
        in_specs=[pl.BlockSpec((K, tt), lambda i: (0, i)), pl.BlockSpec((K, tt), lambda i: (0, i)),
                  _const_spec((N_EXPERTS, 1))],
        out_specs=pl.BlockSpec((wpt, planes * K, SC_WINDOW), lambda i: (i, 0, 0)),
        out_shape=jax.ShapeDtypeStruct((T // SC_WINDOW, planes * K, SC_WINDOW), I32),
        compiler_params=_params(("arbitrary",)), name="pair_rows",
    )(eidx, rank, pstart)


def _expert_rows(x_ref, o_ref, wg_ref, wu_ref, wd_ref, nvalid, rows):
    planes = x_ref.shape[0]
    x = _unpack_rows([x_ref[j, :rows, :] for j in range(planes)])
    live = lax.broadcasted_iota(I32, x.shape, 0) < nvalid
    x = jnp.where(live, x, jnp.zeros_like(x))
    g = _dot(x, wg_ref[0].astype(BF16))
    u = _dot(x, wu_ref[0].astype(BF16))
    y = _dot((_silu(g) * u).astype(BF16), wd_ref[0].astype(BF16))
    for j, plane in enumerate(_pack_rows(y)):
        o_ref[j, :rows, :] = plane


def _expert_kernel(blk_e_ref, nvalid_ref, nused_ref, x_ref, wg_ref, wu_ref, wd_ref, o_ref):
    b = pl.program_id(0)
    used = b < nused_ref[0]
    nvalid = nvalid_ref[b]
    half = MOE_ROWS // 2

    @pl.when(jnp.logical_and(used, nvalid > half))
    def _():
        _expert_rows(x_ref, o_ref, wg_ref, wu_ref, wd_ref, nvalid, MOE_ROWS)

    @pl.when(jnp.logical_and(used, nvalid <= half))
    def _():
        _expert_rows(x_ref, o_ref, wg_ref, wu_ref, wd_ref, nvalid, half)
        o_ref[:, half:, :] = jnp.zeros((o_ref.shape[0], MOE_ROWS - half, LANES), o_ref.dtype)


def _experts(xs, blk_e, nvalid, nused, wg, wu, wd):
    planes, rows, _ = xs.shape
    nb = rows // MOE_ROWS
    _, D, F = wg.shape

    def xmap(b, blk_e_ref, nvalid_ref, nused_ref):
        return (0, jnp.minimum(b, nused_ref[0] - 1), 0)

    def wmap(b, blk_e_ref, nvalid_ref, nused_ref):
        return (blk_e_ref[b], 0, 0)

    grid_spec = pltpu.PrefetchScalarGridSpec(
        num_scalar_prefetch=3, grid=(nb,),
        in_specs=[pl.BlockSpec((planes, MOE_ROWS, LANES), xmap),
                  pl.BlockSpec((1, D, F), wmap), pl.BlockSpec((1, D, F), wmap), pl.BlockSpec((1, F, D), wmap)],
        out_specs=pl.BlockSpec((planes, MOE_ROWS, LANES), xmap))
    return pl.pallas_call(
        _expert_kernel, grid_spec=grid_spec, out_shape=jax.ShapeDtypeStruct(xs.shape, U32),
        compiler_params=_params(("arbitrary",)), name="experts",
    )(blk_e, nvalid, nused, xs, wg, wu, wd)


def _final_kernel(base_ref, ys_ref, gate_ref, gtf_ref, gfin_ref, o_ref):
    nwin, planes = ys_ref.shape[:2]
    gate = gate_ref[...]
    routed = jnp.zeros(base_ref.shape[1:], F32)
    for k in range(TOP_K):
        rows = jnp.concatenate([_unpack_rows([ys_ref[w, j, k] for j in range(planes)]) for w in range(nwin)], axis=0)
        routed = routed + gate[:, k:k + 1] * rows.astype(F32)
    y = base_ref[0] + gtf_ref[0] * routed
    o_ref[0] = _rms(y, gfin_ref[...])


def _final(base, ysg, gate_t, gt_f, g_final, tm, row0):
    B, L, D = base.shape
    planes = ysg.shape[1]
    wpt = tm // SC_WINDOW
    nt = L // tm
    off = row0 // tm
    per_row = gt_f.shape[1] != 1
    mod_spec = (pl.BlockSpec((1, tm, D), lambda b, i: (b, i, 0)) if per_row
                else pl.BlockSpec((1, 1, D), lambda b, i: (b, 0, 0)))
    return pl.pallas_call(
        _final_kernel, grid=(B, nt),
        in_specs=[pl.BlockSpec((1, tm, D), lambda b, i: (b, i, 0)),
                  pl.BlockSpec((wpt, planes, TOP_K, SC_WINDOW, LANES), lambda b, i: (off + b * nt + i, 0, 0, 0, 0)),
                  pl.BlockSpec((tm, TOP_K), lambda b, i: (off + b * nt + i, 0)),
                  mod_spec, _const_spec((1, D))],
        out_specs=pl.BlockSpec((1, tm, D), lambda b, i: (b, i, 0)),
        out_shape=jax.ShapeDtypeStruct((B, L, D), F32),
        compiler_params=_params(("arbitrary", "arbitrary")), name="final",
    )(base, ysg, gate_t, gt_f, g_final)


def _sc_mesh():
    return plsc.VectorSubcoreMesh(core_axis_name="c", subcore_axis_name="s")


def _sc_worker():
    return lax.axis_index("s") * SC_CORES + lax.axis_index("c")


def _sc_dispatch(src, idx, out_rows, planes, n_tok):
    n_win = n_tok // SC_WINDOW
    n_iter = -(-n_win // SC_WORKERS)

    @functools.partial(
        pl.kernel, mesh=_sc_mesh(), out_type=jax.ShapeDtypeStruct((out_rows, LANES), src.dtype),
        scratch_types=[pltpu.VMEM((planes * TOP_K, SC_WINDOW), I32),
                       pltpu.VMEM((planes, SC_WINDOW, LANES), src.dtype),
                       pltpu.SemaphoreType.DMA((planes,)), pltpu.SemaphoreType.DMA],
        name="sc_dispatch")
    def k(src_hbm, idx_hbm, out_hbm, idx_v, rows_v, load_sem, scat_sem):
        wid = _sc_worker()

        @pl.loop(0, n_iter)
        def _(i):
            win = i * SC_WORKERS + wid

            @pl.when(win < n_win)
            def _():
                t0 = win * SC_WINDOW
                loads = [pltpu.make_async_copy(
                    src_hbm.at[pl.ds(pl.multiple_of(j * n_tok + t0, SC_WINDOW), SC_WINDOW)],
                    rows_v.at[j], load_sem.at[j]) for j in range(planes)]
                for ld in loads:
                    ld.start()
                pltpu.sync_copy(idx_hbm.at[win], idx_v)
                scatters = []
                for j in range(planes):
                    loads[j].wait()
                    for kk in range(TOP_K):
                        cp = pltpu.make_async_copy(rows_v.at[j], out_hbm.at[idx_v.at[j * TOP_K + kk]], scat_sem)
                        cp.start()
                        scatters.append(cp)
                for cp in scatters:
                    cp.wait()

    return k(src, idx)


def _sc_gather(table, idx):
    n_chunks, win = idx.shape
    n_iter = n_chunks // SC_WORKERS
    assert n_chunks % SC_WORKERS == 0 and n_iter % 2 == 0

    @functools.partial(
        pl.kernel, mesh=_sc_mesh(), out_type=jax.ShapeDtypeStruct((n_chunks * win, LANES), table.dtype),
        scratch_types=[pltpu.VMEM((n_iter, win), I32), pltpu.VMEM((2, win, LANES), table.dtype),
                       pltpu.SemaphoreType.DMA((2,)), pltpu.SemaphoreType.DMA((2,))],
        name="sc_gather")
    def k(table_hbm, idx_hbm, out_hbm, idx_v, rows_v, gather_sem, write_sem):
        wid = _sc_worker()
        c0 = wid * n_iter
        pltpu.sync_copy(idx_hbm.at[wid], idx_v)

        def gather(c, b):
            return pltpu.make_async_copy(table_hbm.at[idx_v.at[c]], rows_v.at[b], gather_sem.at[b])

        def write(c, b):
            return pltpu.make_async_copy(rows_v.at[b], out_hbm.at[pl.ds(pl.multiple_of((c0 + c) * win, win), win)],
                                         write_sem.at[b])

        gather(0, 0).start()

        @pl.loop(0, n_iter, step=2)
        def _(i):
            for b in range(2):
                c = i + b
                other = 1 - b

                @pl.when(c >= 1)
                def _():
                    write(c - 1, other).wait()

                @pl.when(c + 1 < n_iter)
                def _():
                    gather(c + 1, other).start()

                gather(c, b).wait()
                write(c, b).start()

        write(n_iter - 1, (n_iter - 1) % 2).wait()

    return k(table, idx.reshape(SC_WORKERS, n_iter, win))


def _moe_plan(eidx, planes):
    T = eidx.shape[1]
    rank, counts = _rank(eidx, 256)
    counts = counts.reshape(N_EXPERTS).astype(I32)
    nblk = (counts + MOE_ROWS - 1) // MOE_ROWS
    blk_end = jnp.cumsum(nblk)
    blk_start = blk_end - nblk
    nb_max = (T * TOP_K) // MOE_ROWS + N_EXPERTS
    nused = blk_end[-1]
    bidx = jnp.arange(nb_max, dtype=I32)
    last = jnp.minimum(bidx, nused - 1)
    blk_e = jnp.minimum(jnp.sum((blk_end[None, :] <= last[:, None]).astype(I32), axis=1), N_EXPERTS - 1)
    mine = blk_e[:, None] == jnp.arange(N_EXPERTS, dtype=I32)[None, :]
    cnt_b = jnp.sum(jnp.where(mine, counts[None, :], 0), axis=1)
    start_b = jnp.sum(jnp.where(mine, blk_start[None, :], 0), axis=1)
    nvalid = jnp.clip(cnt_b - (bidx - start_b) * MOE_ROWS, 0, MOE_ROWS).astype(I32)
    rows_pad = nb_max * MOE_ROWS
    pstart = (blk_start * MOE_ROWS).astype(F32).reshape(N_EXPERTS, 1)
    idx = _pair_rows(eidx, rank, pstart, planes, rows_pad, 256)
    return idx, blk_e, nvalid, nused.reshape(1).astype(I32), rows_pad


def _moe_group(h2p, eidx, w_gate, w_up, w_down):
    planes, T, _ = h2p.shape
    assert (T * TOP_K) % MOE_ROWS == 0 and T % 256 == 0
    idx, blk_e, nvalid, nused, rows_pad = _moe_plan(eidx, planes)
    n_win = T // SC_WINDOW
    xs = _sc_dispatch(h2p.reshape(planes * T, LANES), idx, planes * rows_pad, planes, T)
    ys = _experts(xs.reshape(planes, rows_pad, LANES), blk_e, nvalid, nused, w_gate, w_up, w_down)
    ysg = _sc_gather(ys.reshape(planes * rows_pad, LANES), idx.reshape(n_win * planes * TOP_K, SC_WINDOW))
    return ysg.reshape(n_win, planes, TOP_K, SC_WINDOW, LANES)


def _rope_tables(pos):
    half = ROPE_DIM // 2
    inv = ROPE_THETA ** (-jnp.arange(half, dtype=F32) / half)
    ang = pos.astype(F32)[:, None] * inv[None, :]
    cos = jnp.tile(jnp.cos(ang), (1, 2 * MLA_HEADS))
    sin = jnp.tile(jnp.sin(ang), (1, 2 * MLA_HEADS))
    return cos, sin


def _rot_cols(w):
    half = ROPE_DIM // 2
    return jnp.concatenate([-w[..., half:], w[..., :half]], axis=-1)


def _layer_weights(w_in, g_mix, g_q, g_kv, w_uq, w_uk, w_uv, s5_D, s5_w_glu, s5_b_glu, g_out_attn, g_out_s5,
                   w_out, g_ffn, w_router, b_router, w_sh_gate, w_sh_up, w_sh_down):
    D = w_in.shape[0]
    o1, o2, o3 = Q_LORA, Q_LORA + KV_LORA, Q_LORA + KV_LORA + ROPE_DIM
    s5w = w_in.shape[1] - o3
    w_rope = w_in[:, o2:o3]
    w_in_ext = jnp.concatenate([w_in[:, :o2], w_in[:, o3:], w_rope, _rot_cols(w_rope)], axis=1).astype(BF16)
    wq = w_uq.reshape(Q_LORA, MLA_HEADS, NOPE_DIM + ROPE_DIM)
    wq_rope = wq[:, :, NOPE_DIM:]
    wr_hi, wr_lo = _split(w_router.T)
    return {
        "s5w": s5w,
        "g_mix": g_mix.reshape(1, D), "w_in": w_in_ext,
        "g_q": g_q.reshape(1, Q_LORA), "g_kv": g_kv.reshape(1, KV_LORA),
        "wq_nope": wq[:, :, :NOPE_DIM].reshape(Q_LORA, MLA_HEADS * NOPE_DIM).astype(BF16),
        "wq_rope": wq_rope.reshape(Q_LORA, MLA_HEADS * ROPE_DIM).astype(BF16),
        "wq_rot": _rot_cols(wq_rope).reshape(Q_LORA, MLA_HEADS * ROPE_DIM).astype(BF16),
        "w_uk": w_uk.reshape(KV_LORA, MLA_HEADS * NOPE_DIM).astype(BF16),
        "w_uv": jnp.transpose(w_uv, (1, 0, 2)).astype(BF16),
        "s5_d": s5_D.reshape(1, s5w), "w_glu": s5_w_glu.astype(BF16), "b_glu": s5_b_glu.reshape(1, s5w),
        "g_out_attn": g_out_attn.reshape(1, -1), "g_out_s5": g_out_s5.reshape(1, s5w),
        "w_out": w_out.astype(BF16), "g_ffn": g_ffn.reshape(1, D),
        "wr_hi": wr_hi, "wr_lo": wr_lo, "b_router": b_router.reshape(N_EXPERTS, 1),
        "w_sg": w_sh_gate.astype(BF16), "w_su": w_sh_up.astype(BF16), "w_sd": w_sh_down.astype(BF16),
    }


def _state_in(s_re, s_im):
    B = s_re.shape[0]
    return jnp.stack([s_re.reshape(B, -1), s_im.reshape(B, -1)], axis=1).astype(F32)


def kernel(x_prompt, x_sample, c_prompt, c_sample, cache_ckv, cache_krope, state_s5_re, state_s5_im, w_ada, b_ada, g_mix, g_ffn, w_in, g_q, w_uq, g_kv, w_uk, w_uv, s5_A_re, s5_A_im, s5_B_re, s5_B_im, s5_C_re, s5_C_im, s5_D, s5_log_dt, s5_w_glu, s5_b_glu, g_out_attn, g_out_s5, w_out, w_router, b_router, w_exp_gate, w_exp_up, w_exp_down, w_sh_gate, w_sh_up, w_sh_down, g_final):
    Bp, Lp, D = x_prompt.shape
    Bs, Ls, _ = x_sample.shape
    depth = w_ada.shape[0]
    assert depth == 1, "single-layer step"
    past = cache_ckv.shape[2]
    G, N = s5_A_re.shape[1:]
    Ts = Bs * Ls
    Tp = Bp * Lp
    l = 0

    wts = _layer_weights(w_in[l], g_mix[l], g_q[l], g_kv[l], w_uq[l], w_uk[l], w_uv[l], s5_D[l], s5_w_glu[l],
                         s5_b_glu[l], g_out_attn[l], g_out_s5[l], w_out[l], g_ffn[l], w_router[l], b_router[l],
                         w_sh_gate[l], w_sh_up[l], w_sh_down[l])

    mod = _ada(jnp.concatenate([c_prompt, c_sample], axis=0), w_ada[l], b_ada[l])
    mod_p = [m.reshape(Bp, 1, D) for m in jnp.split(mod[:Bp], 6, axis=-1)]
    mod_s = [jnp.broadcast_to(m[:, None, :], (Bs, Ls, D)).reshape(1, Ts, D)
             for m in jnp.split(mod[Bp:], 6, axis=-1)]

    tm_p = 256
    tm_pre = 1024
    tm_post = 1024
    cos_p, sin_p = _rope_tables(jnp.arange(Lp))
    cos_s, sin_s = _rope_tables(jnp.tile(past + jnp.arange(Ls), Bs))

    ckv_p, kr_p, u_p, kcat_p, v_p, q_p = _pre(x_prompt, mod_p[0], mod_p[1], cos_p, sin_p, wts, tm_pre)
    olat_p = _attn_prompt(q_p, kcat_p, v_p, 512)
    tabs_p = _s5_tables(s5_A_re[l], s5_A_im[l], s5_B_re[l], s5_B_im[l], s5_C_re[l], s5_C_im[l], s5_log_dt[l], tm_p)
    os5_p, hl_p = _s5(u_p, jnp.zeros((Bp, 2, G * N), F32), tabs_p, wts, tm_p, precise=False)

    xs_rows = x_sample.reshape(1, Ts, D)
    ckv_s, kr_s, u_s, kcat_s, v_s, q_s = _pre(xs_rows, mod_s[0], mod_s[1], cos_s, sin_s, wts, Ts)
    olat_s = _attn_sample(q_s, kcat_s, v_s, cache_ckv[l], cache_krope[l], wts["w_uk"])
    tabs_s = _s5_tables(s5_A_re[l], s5_A_im[l], s5_B_re[l], s5_B_im[l], s5_C_re[l], s5_C_im[l], s5_log_dt[l], Ls)
    os5_s, hl_s = _s5(u_s.reshape(Bs, Ls, -1), _state_in(state_s5_re[l], state_s5_im[l]), tabs_s, wts, Ls,
                      precise=True)
    h2_s, base_s, eidx_s, gate_s = _post(xs_rows, olat_s, os5_s.reshape(1, Ts, -1),
                                         [mod_s[2], mod_s[3], mod_s[4], mod_s[5]], wts, Ts)

    h2_p, base_p, eidx_p, gate_p = _post(x_prompt, olat_p, os5_p, [mod_p[2], mod_p[3], mod_p[4], mod_p[5]], wts, tm_post)
    ysg = _moe_group(jnp.concatenate([h2_p, h2_s], axis=1), jnp.concatenate([eidx_p, eidx_s], axis=1),
                     w_exp_gate[l], w_exp_up[l], w_exp_down[l])
    gate_t = jnp.concatenate([gate_p, gate_s], axis=1).T

    gfin = g_final.reshape(1, D)
    y_p = _final(base_p, ysg, gate_t, mod_p[5], gfin, tm_p, 0)
    y_s = _final(base_s, ysg, gate_t, mod_s[5], gfin, Ts, Tp).reshape(Bs, Ls, D)

    def state_out(hl, B):
        return hl[:, 0].reshape(1, B, G, N), hl[:, 1].reshape(1, B, G, N)

    sre_p, sim_p = state_out(hl_p, Bp)
    sre_s, sim_s = state_out(hl_s, Bs)
    return (y_p, y_s, ckv_p[None], kr_p[None], sre_p, sim_p,
            ckv_s.reshape(1, Bs, Ls, KV_LORA), kr_s.reshape(1, Bs, Ls, ROPE_DIM), sre_s, sim_s)
```

```python
import functools
import math

import numpy as np
import jax
import jax.numpy as jnp
from jax import lax
from jax.experimental import pallas as pl
from jax.experimental.pallas import tpu as pltpu
from jax.experimental.pallas import tpu_sc as plsc

F32 = jnp.float32
BF16 = jnp.bfloat16
I32 = jnp.int32
U32 = jnp.uint32

EPS = 1e-6
CHUNK = 64
MLA_HEADS = 4
NOPE_DIM = 128
ROPE_DIM = 64
V_DIM = 128
Q_LORA = 256
KV_LORA = 256
QK_HEAD = NOPE_DIM + ROPE_DIM
ROPE_THETA = 10000.0
S5_GROUP_CH = 16
S5_STATE = 64
N_EXPERTS = 256
TOP_K = 8
N_EXPERT_GROUPS = 8
TOPK_GROUPS = 4
ROUTED_SCALE = 2.5

S5_SUB = 16
MOE_ROWS = 512
LANES = 128
VMEM_LIMIT = 56 * 1024 * 1024
SC_CORES = 2
SC_SUBCORES = 16
SC_WORKERS = SC_CORES * SC_SUBCORES
SC_WINDOW = 128

NEG_INF = float("-inf")
BIG_I32 = 1 << 30


def _dot(a, b):
    return jnp.dot(a, b, preferred_element_type=F32)


def _dot_nt(a, b):
    return lax.dot_general(a, b, (((1,), (1,)), ((), ())), preferred_element_type=F32)


def _split(a):
    hi = a.astype(BF16)
    lo = (a - hi.astype(F32)).astype(BF16)
    return hi, lo


def _lane_tile(x, n):
    return jnp.concatenate([x] * n, axis=1)


def _rms(x, g):
    return x * lax.rsqrt(jnp.mean(x * x, axis=-1, keepdims=True) + EPS) * g


def _silu(x):
    return x * jax.nn.sigmoid(x)


def _pack_rows(x):
    half = x.shape[1] // 2
    hi = lax.bitcast_convert_type(x[:, :half].astype(BF16).astype(F32), U32)
    lo = lax.bitcast_convert_type(x[:, half:].astype(BF16).astype(F32), U32)
    w = hi | (lo >> 16)
    return [w[:, j * LANES:(j + 1) * LANES] for j in range(half // LANES)]


def _unpack_rows(planes):
    his = [lax.bitcast_convert_type(p & jnp.uint32(0xFFFF0000), F32).astype(BF16) for p in planes]
    los = [lax.bitcast_convert_type(p << 16, F32).astype(BF16) for p in planes]
    return jnp.concatenate(his + los, axis=1)


def _params(sem):
    return pltpu.CompilerParams(dimension_semantics=sem, vmem_limit_bytes=VMEM_LIMIT)


def _const_spec(shape):
    nd = len(shape)
    return pl.BlockSpec(shape, lambda *_: (0,) * nd)


def _ada_kernel(c_ref, whi_ref, wlo_ref, b_ref, o_ref):
    c = c_ref[...]
    s_hi, s_lo = _split(_silu(c))
    w_hi = whi_ref[...]
    o_ref[...] = _dot(s_hi, w_hi) + _dot(s_hi, wlo_ref[...]) + _dot(s_lo, w_hi) + b_ref[...]


def _ada(c, w_ada, b_ada):
    rows, d = c.shape
    n = w_ada.shape[1]
    tn = 512
    w_hi, w_lo = _split(w_ada)
    return pl.pallas_call(
        _ada_kernel,
        grid=(n // tn,),
        in_specs=[_const_spec((rows, d)),
                  pl.BlockSpec((d, tn), lambda j: (0, j)),
                  pl.BlockSpec((d, tn), lambda j: (0, j)),
                  pl.BlockSpec((1, tn), lambda j: (0, j))],
        out_specs=pl.BlockSpec((rows, tn), lambda j: (0, j)),
        out_shape=jax.ShapeDtypeStruct((rows, n), F32),
        compiler_params=_params(("arbitrary",)),
        name="ada",
    )(c, w_hi, w_lo, b_ada.reshape(1, n))


def _pre_kernel(x_ref, sh_ref, sc_ref, g_ref, win_ref, gq_ref, gkv_ref, wqn_ref, wqr_ref, wqt_ref,
                wuk_ref, cos_ref, sin_ref, ckv_ref, kr_ref, u_ref, kcat_ref, v_ref, q_ref, *, scale):
    x = x_ref[0]
    h = _rms(x, g_ref[...]) * (1.0 + sc_ref[0]) + sh_ref[0]
    z = _dot(h.astype(BF16), win_ref[...])
    cq = _rms(z[:, :Q_LORA], gq_ref[...])
    ckv = _rms(z[:, Q_LORA:Q_LORA + KV_LORA], gkv_ref[...])
    o_s5 = Q_LORA + KV_LORA
    s5w = u_ref.shape[-1]
    u_ref[0] = z[:, o_s5:o_s5 + s5w]
    o_r = o_s5 + s5w
    cos = cos_ref[...]
    sin = sin_ref[...]
    kr = z[:, o_r:o_r + ROPE_DIM] * cos[:, :ROPE_DIM] + z[:, o_r + ROPE_DIM:o_r + 2 * ROPE_DIM] * sin[:, :ROPE_DIM]
    ckv_ref[0] = ckv
    kr_ref[0] = kr
    ckvb = ckv.astype(BF16)
    krb = kr.astype(BF16)
    v_ref[0] = ckvb
    kn = _dot(ckvb, wuk_ref[...])
    cqb = cq.astype(BF16)
    qn = _dot(cqb, wqn_ref[...]) * scale
    qr = (_dot(cqb, wqr_ref[...]) * cos + _dot(cqb, wqt_ref[...]) * sin) * scale
    for hd in range(MLA_HEADS):
        kcat_ref[0, hd, :, :NOPE_DIM] = kn[:, hd * NOPE_DIM:(hd + 1) * NOPE_DIM].astype(BF16)
        kcat_ref[0, hd, :, NOPE_DIM:] = krb
        q_ref[0, hd, :, :NOPE_DIM] = qn[:, hd * NOPE_DIM:(hd + 1) * NOPE_DIM].astype(BF16)
        q_ref[0, hd, :, NOPE_DIM:] = qr[:, hd * ROPE_DIM:(hd + 1) * ROPE_DIM].astype(BF16)


def _pre(x, shift, scale_mod, cos_t, sin_t, wts, tm):
    B, L, D = x.shape
    nt = L // tm
    per_row = shift.shape[1] != 1
    mod_spec = (pl.BlockSpec((1, tm, D), lambda b, i: (b, i, 0)) if per_row
                else pl.BlockSpec((1, 1, D), lambda b, i: (b, 0, 0)))
    s5w = wts["s5w"]
    hr = MLA_HEADS * ROPE_DIM
    kern = functools.partial(_pre_kernel, scale=QK_HEAD ** -0.5 * math.log2(math.e))
    consts = [wts["g_mix"], wts["w_in"], wts["g_q"], wts["g_kv"], wts["wq_nope"], wts["wq_rope"],
              wts["wq_rot"], wts["w_uk"]]
    in_specs = [pl.BlockSpec((1, tm, D), lambda b, i: (b, i, 0)), mod_spec, mod_spec]
    in_specs += [_const_spec(c.shape) for c in consts]
    in_specs += [pl.BlockSpec((tm, hr), lambda b, i: (i, 0)), pl.BlockSpec((tm, hr), lambda b, i: (i, 0))]
    out_shape = (jax.ShapeDtypeStruct((B, L, KV_LORA), F32),
                 jax.ShapeDtypeStruct((B, L, ROPE_DIM), F32),
                 jax.ShapeDtypeStruct((B, L, s5w), F32),
                 jax.ShapeDtypeStruct((B, MLA_HEADS, L, QK_HEAD), BF16),
                 jax.ShapeDtypeStruct((B, L, KV_LORA), BF16),
                 jax.ShapeDtypeStruct((B, MLA_HEADS, L, QK_HEAD), BF16))
    out_specs = (pl.BlockSpec((1, tm, KV_LORA), lambda b, i: (b, i, 0)),
                 pl.BlockSpec((1, tm, ROPE_DIM), lambda b, i: (b, i, 0)),
                 pl.BlockSpec((1, tm, s5w), lambda b, i: (b, i, 0)),
                 pl.BlockSpec((1, MLA_HEADS, tm, QK_HEAD), lambda b, i: (b, 0, i, 0)),
                 pl.BlockSpec((1, tm, KV_LORA), lambda b, i: (b, i, 0)),
                 pl.BlockSpec((1, MLA_HEADS, tm, QK_HEAD), lambda b, i: (b, 0, i, 0)))
    return pl.pallas_call(
        kern, grid=(B, nt), in_specs=in_specs, out_specs=out_specs, out_shape=out_shape,
        compiler_params=_params(("arbitrary", "arbitrary")), name="pre",
    )(x, shift, scale_mod, *consts, cos_t, sin_t)


def _attn_kernel(q_ref, k_ref, v_ref, o_ref, m_scr, l_scr, acc_scr, *, t):
    i = pl.program_id(1)
    m_scr[...] = jnp.full(m_scr.shape, NEG_INF, F32)
    l_scr[...] = jnp.zeros(l_scr.shape, F32)
    acc_scr[...] = jnp.zeros(acc_scr.shape, F32)
    visible = (lax.broadcasted_iota(I32, (t, t), 1) // CHUNK) <= (lax.broadcasted_iota(I32, (t, t), 0) // CHUNK)

    def step(j0, masked):
        v = v_ref[0, pl.ds(j0, t), :]
        for hd in range(MLA_HEADS):
            rs = slice(hd * t, (hd + 1) * t)
            s = _dot_nt(q_ref[0, hd], k_ref[0, hd, pl.ds(j0, t), :])
            if masked:
                s = jnp.where(visible, s, NEG_INF)
            m_prev = m_scr[rs]
            m_next = jnp.maximum(m_prev, jnp.max(s, axis=1, keepdims=True))
            alpha = jnp.exp2(m_prev - m_next)
            p = jnp.exp2(s - _lane_tile(m_next, t // LANES))
            l_scr[rs] = alpha * l_scr[rs] + jnp.sum(p, axis=1, keepdims=True)
            m_scr[rs] = m_next
            acc_scr[rs] = acc_scr[rs] * _lane_tile(alpha, KV_LORA // LANES) + _dot(p.astype(BF16), v)

    def body(j, carry):
        step(pl.multiple_of(j * t, t), False)
        return carry

    lax.fori_loop(0, i, body, 0)
    step(pl.multiple_of(i * t, t), True)
    for hd in range(MLA_HEADS):
        rs = slice(hd * t, (hd + 1) * t)
        inv = 1.0 / l_scr[rs]
        o_ref[0, hd] = (acc_scr[rs] * _lane_tile(inv, KV_LORA // LANES)).astype(BF16)


def _attn_prompt(q, kcat, v, t):
    B, H, L, _ = q.shape
    assert L % t == 0 and t % CHUNK == 0
    rows = H * t
    kern = functools.partial(_attn_kernel, t=t)
    resident = pl.Buffered(1)
    return pl.pallas_call(
        kern, grid=(B, L // t),
        in_specs=[pl.BlockSpec((1, H, t, QK_HEAD), lambda b, i: (b, 0, i, 0)),
                  pl.BlockSpec((1, H, L, QK_HEAD), lambda b, i: (b, 0, 0, 0), pipeline_mode=resident),
                  pl.BlockSpec((1, L, KV_LORA), lambda b, i: (b, 0, 0), pipeline_mode=resident)],
        out_specs=pl.BlockSpec((1, H, t, KV_LORA), lambda b, i: (b, 0, i, 0)),
        out_shape=jax.ShapeDtypeStruct((B, H, L, KV_LORA), BF16),
        scratch_shapes=[pltpu.VMEM((rows, LANES), F32), pltpu.VMEM((rows, LANES), F32),
                        pltpu.VMEM((rows, KV_LORA), F32)],
        compiler_params=_params(("arbitrary", "arbitrary")), name="attn_prompt",
    )(q, kcat, v)


def _attn_sample_kernel(q_ref, kn_ref, vn_ref, pc_ref, pr_ref, wuk_ref, o_ref, *, past, lq):
    pc = pc_ref[0].astype(BF16)
    pr = pr_ref[0].astype(BF16)
    kp = _dot(pc, wuk_ref[...]).astype(BF16)
    vn = vn_ref[0]
    qchunk_p = (past + lax.broadcasted_iota(I32, (lq, past), 0)) // CHUNK
    vis_p = lax.broadcasted_iota(I32, (lq, past), 1) // CHUNK <= qchunk_p
    qchunk_n = (past + lax.broadcasted_iota(I32, (lq, lq), 0)) // CHUNK
    vis_n = (past + lax.broadcasted_iota(I32, (lq, lq), 1)) // CHUNK <= qchunk_n
    for hd in range(MLA_HEADS):
        q = q_ref[0, hd]
        s_p = (_dot_nt(q[:, :NOPE_DIM], kp[:, hd * NOPE_DIM:(hd + 1) * NOPE_DIM])
               + _dot_nt(q[:, NOPE_DIM:], pr))
        s_n = _dot_nt(q, kn_ref[0, hd])
        s_p = jnp.where(vis_p, s_p, NEG_INF)
        s_n = jnp.where(vis_n, s_n, NEG_INF)
        m = jnp.maximum(jnp.max(s_p, axis=1, keepdims=True), jnp.max(s_n, axis=1, keepdims=True))
        p_p = jnp.exp2(s_p - m)
        p_n = jnp.exp2(s_n - m)
        l = jnp.sum(p_p, axis=1, keepdims=True) + jnp.sum(p_n, axis=1, keepdims=True)
        o = _dot(p_p.astype(BF16), pc) + _dot(p_n.astype(BF16), vn)
        o_ref[0, hd] = (o / l).astype(BF16)


def _attn_sample(q, kcat, v, past_ckv, past_kr, w_uk):
    B, past, _ = past_ckv.shape
    H = MLA_HEADS
    lq = q.shape[2] // B
    kern = functools.partial(_attn_sample_kernel, past=past, lq=lq)
    return pl.pallas_call(
        kern, grid=(B,),
        in_specs=[pl.BlockSpec((1, H, lq, QK_HEAD), lambda b: (0, 0, b, 0)),
                  pl.BlockSpec((1, H, lq, QK_HEAD), lambda b: (0, 0, b, 0)),
                  pl.BlockSpec((1, lq, KV_LORA), lambda b: (0, b, 0)),
                  pl.BlockSpec((1, past, KV_LORA), lambda b: (b, 0, 0)),
                  pl.BlockSpec((1, past, ROPE_DIM), lambda b: (b, 0, 0)),
                  _const_spec(w_uk.shape)],
        out_specs=pl.BlockSpec((1, H, lq, KV_LORA), lambda b: (0, 0, b, 0)),
        out_shape=jax.ShapeDtypeStruct((1, H, B * lq, KV_LORA), BF16),
        compiler_params=_params(("arbitrary",)), name="attn_sample",
    )(q, kcat, v, past_ckv, past_kr, w_uk)


def _s5_kernel(u_ref, h0_ref, bre_ref, bim_ref, brel_ref, biml_ref, lt_ref, pinr_ref, pini_ref,
               pwr_ref, pwi_ref, a_ref, cre_ref, cim_ref, d_ref, wglu_ref, bglu_ref, gout_ref,
               o_ref, hl_ref, st_scr, cum_scr, hs_scr, *, tm, precise):
    i = pl.program_id(1)
    ns = st_scr.shape[1]
    half = ns // 2
    wh = u_ref.shape[-1] // 2

    @pl.when(i == 0)
    def _():
        st_scr[...] = h0_ref[0]

    u = u_ref[0]
    lt = lt_ref[...]
    for hf in range(2):
        uh = u[:, hf * wh:(hf + 1) * wh]
        if precise:
            u_hi, u_lo = _split(uh)
            bu_re = _dot(u_hi, bre_ref[hf]) + _dot(u_lo, bre_ref[hf]) + _dot(u_hi, brel_ref[hf])
            bu_im = _dot(u_hi, bim_ref[hf]) + _dot(u_lo, bim_ref[hf]) + _dot(u_hi, biml_ref[hf])
        else:
            u_hi = uh.astype(BF16)
            bu_re = _dot(u_hi, bre_ref[hf])
            bu_im = _dot(u_hi, bim_ref[hf])
        sl = slice(hf * half, (hf + 1) * half)
        pr = pinr_ref[:, sl]
        pi = pini_ref[:, sl]
        v_re = pr * bu_re - pi * bu_im
        v_im = pr * bu_im + pi * bu_re
        for part, v in ((0, v_re), (1, v_im)):
            if precise:
                v_hi, v_lo = _split(v)
                c = _dot(lt, v_hi) + _dot(lt, v_lo)
            else:
                c = _dot(lt, v.astype(BF16))
            cum_scr[:, part * ns + hf * half:part * ns + (hf + 1) * half] = c

    a_re = a_ref[0:1, :]
    a_im = a_ref[1:2, :]
    pw_re = pwr_ref[...]
    pw_im = pwi_ref[...]

    def chunk(c, carry):
        s_re, s_im = carry
        r0 = pl.multiple_of(c * S5_SUB, S5_SUB)
        t_re = cum_scr[pl.ds(r0, S5_SUB), 0:ns] + (a_re * s_re - a_im * s_im)
        t_im = cum_scr[pl.ds(r0, S5_SUB), ns:2 * ns] + (a_re * s_im + a_im * s_re)
        h_re = pw_re * t_re - pw_im * t_im
        h_im = pw_re * t_im + pw_im * t_re
        hs_scr[pl.ds(r0, S5_SUB), 0:ns] = h_re.astype(BF16)
        hs_scr[pl.ds(r0, S5_SUB), ns:2 * ns] = h_im.astype(BF16)
        return h_re[S5_SUB - 1:S5_SUB, :], h_im[S5_SUB - 1:S5_SUB, :]

    s_re, s_im = lax.fori_loop(0, tm // S5_SUB, chunk, (st_scr[0:1, :], st_scr[1:2, :]))
    st_scr[0:1, :] = s_re
    st_scr[1:2, :] = s_im
    hl_ref[0, 0:1, :] = s_re
    hl_ref[0, 1:2, :] = s_im

    ys = []
    for hf in range(2):
        hre = hs_scr[:, hf * half:(hf + 1) * half]
        him = hs_scr[:, ns + hf * half:ns + (hf + 1) * half]
        ys.append(_dot(hre, cre_ref[hf]) + _dot(him, cim_ref[hf]))
    y = jnp.concatenate(ys, axis=1) + d_ref[...] * u
    zg = jax.nn.gelu(y)
    gl = _dot(zg.astype(BF16), wglu_ref[...]) + bglu_ref[...]
    o = zg * jax.nn.sigmoid(gl)
    o_ref[0] = _rms(o, gout_ref[...]).astype(BF16)


def _s5(u, h0, tabs, wts, tm, precise):
    B, L, W = u.shape
    ns = h0.shape[-1]
    consts = [tabs["b_re"], tabs["b_im"], tabs["b_re_lo"], tabs["b_im_lo"], tabs["lt"], tabs["pin_re"],
              tabs["pin_im"], tabs["pw_re"], tabs["pw_im"], tabs["a"], tabs["c_re"], tabs["c_im"],
              wts["s5_d"], wts["w_glu"], wts["b_glu"], wts["g_out_s5"]]
    kern = functools.partial(_s5_kernel, tm=tm, precise=precise)
    return pl.pallas_call(
        kern, grid=(B, L // tm),
        in_specs=[pl.BlockSpec((1, tm, W), lambda b, i: (b, i, 0)),
                  pl.BlockSpec((1, 2, ns), lambda b, i: (b, 0, 0))] + [_const_spec(c.shape) for c in consts],
        out_specs=(pl.BlockSpec((1, tm, W), lambda b, i: (b, i, 0)),
                   pl.BlockSpec((1, 2, ns), lambda b, i: (b, 0, 0))),
        out_shape=(jax.ShapeDtypeStruct((B, L, W), BF16), jax.ShapeDtypeStruct((B, 2, ns), F32)),
        scratch_shapes=[pltpu.VMEM((2, ns), F32), pltpu.VMEM((tm, 2 * ns), F32), pltpu.VMEM((tm, 2 * ns), BF16)],
        compiler_params=_params(("arbitrary", "arbitrary")), name="s5",
    )(u, h0, *consts)


def _s5_tables(a_re_p, a_im_p, b_re_p, b_im_p, c_re_p, c_im_p, log_dt, tm):
    G, N = a_re_p.shape
    CH = b_re_p.shape[-1]
    dt = jnp.exp(log_dt.astype(F32))[:, None]
    lr = a_re_p.astype(F32) * dt
    li = a_im_p.astype(F32) * dt
    er = jnp.exp(lr)
    ab_re, ab_im = er * jnp.cos(li), er * jnp.sin(li)
    lam2 = a_re_p.astype(F32) ** 2 + a_im_p.astype(F32) ** 2
    nr, ni = ab_re - 1.0, ab_im
    f_re = (nr * a_re_p + ni * a_im_p) / lam2
    f_im = (ni * a_re_p - nr * a_im_p) / lam2
    bb_re = f_re[..., None] * b_re_p - f_im[..., None] * b_im_p
    bb_im = f_re[..., None] * b_im_p + f_im[..., None] * b_re_p
    gh = G // 2
    eye = jnp.eye(gh, dtype=F32)

    def blk_b(bb):
        t = bb.reshape(2, gh, N, CH)
        m = jnp.einsum("hgnc,gk->hgckn", t, eye)
        return m.reshape(2, gh * CH, gh * N)

    def blk_c(cc):
        t = cc.reshape(2, gh, CH, N)
        m = jnp.einsum("hgcn,gk->hgnkc", t, eye)
        return m.reshape(2, gh * N, gh * CH)

    b_re_m, b_im_m = blk_b(bb_re), blk_b(bb_im)
    b_re_hi, b_re_lo = _split(b_re_m)
    b_im_hi, b_im_lo = _split(b_im_m)
    s = jnp.arange(S5_SUB, dtype=F32)[:, None, None]

    def powers(sign):
        e = jnp.exp(sign * lr[None] * s)
        return ((e * jnp.cos(sign * li[None] * s)).reshape(S5_SUB, G * N),
                (e * jnp.sin(sign * li[None] * s)).reshape(S5_SUB, G * N))

    pin_re, pin_im = powers(-1.0)
    pw_re, pw_im = powers(1.0)
    reps = tm // S5_SUB
    r = np.arange(tm)
    lt = ((r[:, None] // S5_SUB == r[None, :] // S5_SUB) & (r[None, :] <= r[:, None])).astype(np.float32)
    return {
        "b_re": b_re_hi, "b_im": b_im_hi, "b_re_lo": b_re_lo, "b_im_lo": b_im_lo,
        "lt": jnp.asarray(lt, BF16),
        "pin_re": jnp.tile(pin_re, (reps, 1)), "pin_im": jnp.tile(pin_im, (reps, 1)),
        "pw_re": pw_re, "pw_im": pw_im,
        "a": jnp.stack([ab_re.reshape(G * N), ab_im.reshape(G * N)]),
        "c_re": blk_c(c_re_p.astype(F32)).astype(BF16), "c_im": (-blk_c(c_im_p.astype(F32))).astype(BF16),
    }


def _route(scores, sel):
    E, tm = scores.shape
    ge = E // N_EXPERT_GROUPS
    io_g = lax.broadcasted_iota(I32, (ge, tm), 0)
    gs_rows = []
    for g in range(N_EXPERT_GROUPS):
        sg = sel[g * ge:(g + 1) * ge, :]
        m1 = jnp.max(sg, axis=0, keepdims=True)
        i1 = jnp.min(jnp.where(sg == m1, io_g, BIG_I32), axis=0, keepdims=True)
        m2 = jnp.max(jnp.where(io_g == i1, NEG_INF, sg), axis=0, keepdims=True)
        gs_rows.append(m1 + m2)
    gs = jnp.concatenate(gs_rows, axis=0)
    gio = lax.broadcasted_iota(I32, gs.shape, 0)
    gsel = jnp.zeros(gs.shape, F32)
    for _ in range(TOPK_GROUPS):
        mx = jnp.max(gs, axis=0, keepdims=True)
        ix = jnp.min(jnp.where(gs == mx, gio, BIG_I32), axis=0, keepdims=True)
        hit = gio == ix
        gsel = jnp.where(hit, 1.0, gsel)
        gs = jnp.where(hit, NEG_INF, gs)
    emask = jnp.concatenate([jnp.broadcast_to(gsel[g:g + 1, :], (ge, tm)) for g in range(N_EXPERT_GROUPS)], axis=0)
    cand = jnp.where(emask > 0.0, sel, NEG_INF)
    eio = lax.broadcasted_iota(I32, (E, tm), 0)
    idxs, gates = [], []
    for _ in range(TOP_K):
        mx = jnp.max(cand, axis=0, keepdims=True)
        ix = jnp.min(jnp.where(cand == mx, eio, BIG_I32), axis=0, keepdims=True)
        hit = eio == ix
        gates.append(jnp.sum(jnp.where(hit, scores, 0.0), axis=0, keepdims=True))
        idxs.append(ix)
        cand = jnp.where(hit, NEG_INF, cand)
    eidx = jnp.concatenate(idxs, axis=0)
    gate = jnp.concatenate(gates, axis=0)
    gate = gate / jnp.sum(gate, axis=0, keepdims=True) * ROUTED_SCALE
    return eidx, gate


def _post_kernel(x_ref, ol_ref, os_ref, gtm_ref, shf_ref, scf_ref, gtf_ref, wuv_ref, goa_ref, wout_ref,
                 gffn_ref, wrh_ref, wrl_ref, br_ref, wsg_ref, wsu_ref, wsd_ref,
                 h2_ref, base_ref, eidx_ref, gate_ref):
    x = x_ref[0]
    oa = jnp.concatenate([_dot(ol_ref[0, hd], wuv_ref[hd]) for hd in range(MLA_HEADS)], axis=1)
    oan = _rms(oa, goa_ref[...]).astype(BF16)
    wa = oan.shape[1]
    mix = _dot(oan, wout_ref[:wa, :]) + _dot(os_ref[0], wout_ref[wa:, :])
    x1 = x + gtm_ref[0] * mix
    h2 = _rms(x1, gffn_ref[...]) * (1.0 + scf_ref[0]) + shf_ref[0]
    h2_hi, h2_lo = _split(h2)
    for j, plane in enumerate(_pack_rows(h2)):
        h2_ref[j] = plane
    sh =_dot((_silu(_dot(h2_hi, wsg_ref[...])) * _dot(h2_hi, wsu_ref[...])).astype(BF16), wsd_ref[...])
    base_ref[0] = x1 + gtf_ref[0] * sh
    wr_hi = wrh_ref[...]
    logits = _dot_nt(wr_hi, h2_hi) + _dot_nt(wrl_ref[...], h2_hi) + _dot_nt(wr_hi, h2_lo)
    scores = jax.nn.sigmoid(logits)
    eidx, gate = _route(scores, scores + br_ref[...])
    eidx_ref[...] = eidx
    gate_ref[...] = gate


def _post(x, o_lat, o_s5, mods, wts, tm):
    nb, L, D = x.shape
    nt = L // tm
    per_row = mods[0].shape[1] != 1
    mod_spec = (pl.BlockSpec((1, tm, D), lambda b, i: (b, i, 0)) if per_row
                else pl.BlockSpec((1, 1, D), lambda b, i: (b, 0, 0)))
    W = o_s5.shape[-1]
    consts = [wts["w_uv"], wts["g_out_attn"], wts["w_out"], wts["g_ffn"], wts["wr_hi"], wts["wr_lo"],
              wts["b_router"], wts["w_sg"], wts["w_su"], wts["w_sd"]]
    in_specs = [pl.BlockSpec((1, tm, D), lambda b, i: (b, i, 0)),
                pl.BlockSpec((1, MLA_HEADS, tm, KV_LORA), lambda b, i: (b, 0, i, 0)),
                pl.BlockSpec((1, tm, W), lambda b, i: (b, i, 0)),
                mod_spec, mod_spec, mod_spec, mod_spec] + [_const_spec(c.shape) for c in consts]
    planes = D // (2 * LANES)
    out_shape = (jax.ShapeDtypeStruct((planes, nb * L, LANES), U32), jax.ShapeDtypeStruct((nb, L, D), F32),
                 jax.ShapeDtypeStruct((TOP_K, nb * L), I32), jax.ShapeDtypeStruct((TOP_K, nb * L), F32))
    out_specs = (pl.BlockSpec((planes, tm, LANES), lambda b, i: (0, b * nt + i, 0)),
                 pl.BlockSpec((1, tm, D), lambda b, i: (b, i, 0)),
                 pl.BlockSpec((TOP_K, tm), lambda b, i: (0, b * nt + i)),
                 pl.BlockSpec((TOP_K, tm), lambda b, i: (0, b * nt + i)))
    return pl.pallas_call(
        _post_kernel, grid=(nb, nt), in_specs=in_specs, out_specs=out_specs, out_shape=out_shape,
        compiler_params=_params(("arbitrary", "arbitrary")), name="post",
    )(x, o_lat, o_s5, *mods, *consts)


def _rank_kernel(eidx_ref, tri_ref, rank_ref, cnt_ref, carry_scr):
    i = pl.program_id(0)

    @pl.when(i == 0)
    def _():
        carry_scr[...] = jnp.zeros(carry_scr.shape, F32)

    eidx = eidx_ref[...]
    tt = eidx.shape[1]
    eio = lax.broadcasted_iota(I32, (N_EXPERTS, tt), 0)
    hits = [eio == eidx[k:k + 1, :] for k in range(TOP_K)]
    onehot = jnp.zeros((N_EXPERTS, tt), F32)
    for hit in hits:
        onehot = jnp.where(hit, 1.0, onehot)
    before = _dot(onehot.astype(BF16), tri_ref[...]) + carry_scr[...]
    ranks = [jnp.sum(jnp.where(hit, before, 0.0), axis=0, keepdims=True) for hit in hits]
    rank_ref[...] = jnp.concatenate(ranks, axis=0).astype(I32)
    carry = carry_scr[...] + jnp.sum(onehot, axis=1, keepdims=True)
    carry_scr[...] = carry
    cnt_ref[...] = carry


def _rank(eidx, tt):
    K, T = eidx.shape
    r = np.arange(tt)
    tri = jnp.asarray((r[:, None] < r[None, :]).astype(np.float32), BF16)
    return pl.pallas_call(
        _rank_kernel, grid=(T // tt,),
        in_specs=[pl.BlockSpec((K, tt), lambda i: (0, i)), _const_spec((tt, tt))],
        out_specs=(pl.BlockSpec((K, tt), lambda i: (0, i)), _const_spec((N_EXPERTS, 1))),
        out_shape=(jax.ShapeDtypeStruct((K, T), I32), jax.ShapeDtypeStruct((N_EXPERTS, 1), F32)),
        scratch_shapes=[pltpu.VMEM((N_EXPERTS, 1), F32)],
        compiler_params=_params(("arbitrary",)), name="rank",
    )(eidx, tri)


def _pos_kernel(eidx_ref, rank_ref, pstart_ref, idx_ref, *, rows_pad):
    eidx = eidx_ref[...]
    tt = eidx.shape[1]
    eio = lax.broadcasted_iota(I32, (N_EXPERTS, tt), 0)
    pstart = pstart_ref[...]
    starts = [jnp.sum(jnp.where(eio == eidx[k:k + 1, :], pstart, 0.0), axis=0, keepdims=True)
              for k in range(TOP_K)]
    pos = jnp.concatenate(starts, axis=0).astype(I32) + rank_ref[...]
    planes = idx_ref.shape[1] // TOP_K
    for w in range(idx_ref.shape[0]):
        for j in range(planes):
            idx_ref[w, j * TOP_K:(j + 1) * TOP_K, :] = pos[:, w * SC_WINDOW:(w + 1) * SC_WINDOW] + j * rows_pad


def _pair_rows(eidx, rank, pstart, planes, rows_pad, tt):
    K, T = eidx.shape
    wpt = tt // SC_WINDOW
    return pl.pallas_call(
        functools.partial(_pos_kernel, rows_pad=rows_pad), grid=(T // tt,),
        in_specs=[pl.BlockSpec((K, tt), lambda i: (0, i)), pl.BlockSpec((K, tt), lambda i: (0, i)),
                  _const_spec((N_EXPERTS, 1))],
        out_specs=pl.BlockSpec((wpt, planes * K, SC_WINDOW), lambda i: (i, 0, 0)),
        out_shape=jax.ShapeDtypeStruct((T // SC_WINDOW, planes * K, SC_WINDOW), I32),
        compiler_params=_params(("arbitrary",)), name="pair_rows",
    )(eidx, rank, pstart)


def _expert_rows(x_ref, o_ref, wg, wu, wd, nvalid, rows):
    planes = x_ref.shape[0]
    x = _unpack_rows([x_ref[j, :rows, :] for j in range(planes)])
    live = lax.broadcasted_iota(I32, x.shape, 0) < nvalid
    x = jnp.where(live, x, jnp.zeros_like(x))
    g = _dot(x, wg.astype(BF16))
    u = _dot(x, wu.astype(BF16))
    y = _dot((_silu(g) * u).astype(BF16), wd.astype(BF16))
    for j, plane in enumerate(_pack_rows(y)):
        o_ref[j, :rows, :] = plane


def _expert_kernel(blk_e_ref, nvalid_ref, nused_ref, first_ref, next_ref, slot_ref,
                   x_ref, wg_hbm, wu_hbm, wd_hbm, o_ref, wg_buf, wu_buf, wd_buf, sem):
    b = pl.program_id(0)
    used = b < nused_ref[0]
    e = blk_e_ref[b]
    slot = slot_ref[e]
    nvalid = nvalid_ref[b]
    half = MOE_ROWS // 2

    def fetch(expert, s):
        return (pltpu.make_async_copy(wg_hbm.at[expert], wg_buf.at[s], sem.at[s, 0]),
                pltpu.make_async_copy(wu_hbm.at[expert], wu_buf.at[s], sem.at[s, 1]),
                pltpu.make_async_copy(wd_hbm.at[expert], wd_buf.at[s], sem.at[s, 2]))

    @pl.when(jnp.logical_and(used, first_ref[b] == 1))
    def _():
        @pl.when(b == 0)
        def _():
            for cp in fetch(e, slot):
                cp.start()

        nxt = next_ref[e]

        @pl.when(nxt >= 0)
        def _():
            for cp in fetch(nxt, 1 - slot):
                cp.start()

        for cp in fetch(e, slot):
            cp.wait()

    @pl.when(jnp.logical_and(used, nvalid > half))
    def _():
        _expert_rows(x_ref, o_ref, wg_buf[slot], wu_buf[slot], wd_buf[slot], nvalid, MOE_ROWS)

    @pl.when(jnp.logical_and(used, nvalid <= half))
    def _():
        _expert_rows(x_ref, o_ref, wg_buf[slot], wu_buf[slot], wd_buf[slot], nvalid, half)
        o_ref[:, half:, :] = jnp.zeros((o_ref.shape[0], MOE_ROWS - half, LANES), o_ref.dtype)


def _experts(xs, plan, wg, wu, wd):
    planes, rows, _ = xs.shape
    nb = rows // MOE_ROWS
    _, D, F = wg.shape

    def xmap(b, blk_e_ref, nvalid_ref, nused_ref, *_):
        return (0, jnp.minimum(b, nused_ref[0] - 1), 0)

    hbm = pl.BlockSpec(memory_space=pl.ANY)
    grid_spec = pltpu.PrefetchScalarGridSpec(
        num_scalar_prefetch=len(plan), grid=(nb,),
        in_specs=[pl.BlockSpec((planes, MOE_ROWS, LANES), xmap), hbm, hbm, hbm],
        out_specs=pl.BlockSpec((planes, MOE_ROWS, LANES), xmap),
        scratch_shapes=[pltpu.VMEM((2, D, F), wg.dtype), pltpu.VMEM((2, D, F), wu.dtype),
                        pltpu.VMEM((2, F, D), wd.dtype), pltpu.SemaphoreType.DMA((2, 3))])
    return pl.pallas_call(
        _expert_kernel, grid_spec=grid_spec, out_shape=jax.ShapeDtypeStruct(xs.shape, U32),
        compiler_params=_params(("arbitrary",)), name="experts",
    )(*plan, xs, wg, wu, wd)


def _final_kernel(base_ref, ys_ref, gate_ref, gtf_ref, gfin_ref, o_ref):
    nwin, planes = ys_ref.shape[:2]
    gate = gate_ref[...]
    routed = jnp.zeros(base_ref.shape[1:], F32)
    for k in range(TOP_K):
        rows = jnp.concatenate([_unpack_rows([ys_ref[w, j, k] for j in range(planes)]) for w in range(nwin)], axis=0)
        routed = routed + gate[:, k:k + 1] * rows.astype(F32)
    y = base_ref[0] + gtf_ref[0] * routed
    o_ref[0] = _rms(y, gfin_ref[...])


def _final(base, ysg, gate_t, gt_f, g_final, tm, row0):
    B, L, D = base.shape
    planes = ysg.shape[1]
    wpt = tm // SC_WINDOW
    nt = L // tm
    off = row0 // tm
    per_row = gt_f.shape[1] != 1
    mod_spec = (pl.BlockSpec((1, tm, D), lambda b, i: (b, i, 0)) if per_row
                else pl.BlockSpec((1, 1, D), lambda b, i: (b, 0, 0)))
    return pl.pallas_call(
        _final_kernel, grid=(B, nt),
        in_specs=[pl.BlockSpec((1, tm, D), lambda b, i: (b, i, 0)),
                  pl.BlockSpec((wpt, planes, TOP_K, SC_WINDOW, LANES), lambda b, i: (off + b * nt + i, 0, 0, 0, 0)),
                  pl.BlockSpec((tm, TOP_K), lambda b, i: (off + b * nt + i, 0)),
                  mod_spec, _const_spec((1, D))],
        out_specs=pl.BlockSpec((1, tm, D), lambda b, i: (b, i, 0)),
        out_shape=jax.ShapeDtypeStruct((B, L, D), F32),
        compiler_params=_params(("arbitrary", "arbitrary")), name="final",
    )(base, ysg, gate_t, gt_f, g_final)


def _sc_mesh():
    return plsc.VectorSubcoreMesh(core_axis_name="c", subcore_axis_name="s")


def _sc_worker():
    return lax.axis_index("s") * SC_CORES + lax.axis_index("c")


def _sc_dispatch(src, idx, out_rows, planes, n_tok):
    n_win = n_tok // SC_WINDOW
    n_iter = -(-n_win // SC_WORKERS)

    @functools.partial(
        pl.kernel, mesh=_sc_mesh(), out_type=jax.ShapeDtypeStruct((out_rows, LANES), src.dtype),
        scratch_types=[pltpu.VMEM((planes * TOP_K, SC_WINDOW), I32),
                       pltpu.VMEM((planes, SC_WINDOW, LANES), src.dtype),
                       pltpu.SemaphoreType.DMA((planes,)), pltpu.SemaphoreType.DMA],
        name="sc_dispatch")
    def k(src_hbm, idx_hbm, out_hbm, idx_v, rows_v, load_sem, scat_sem):
        wid = _sc_worker()

        @pl.loop(0, n_iter)
        def _(i):
            win = i * SC_WORKERS + wid

            @pl.when(win < n_win)
            def _():
                t0 = win * SC_WINDOW
                loads = [pltpu.make_async_copy(
                    src_hbm.at[pl.ds(pl.multiple_of(j * n_tok + t0, SC_WINDOW), SC_WINDOW)],
                    rows_v.at[j], load_sem.at[j]) for j in range(planes)]
                for ld in loads:
                    ld.start()
                pltpu.sync_copy(idx_hbm.at[win], idx_v)
                scatters = []
                for j in range(planes):
                    loads[j].wait()
                    for kk in range(TOP_K):
                        cp = pltpu.make_async_copy(rows_v.at[j], out_hbm.at[idx_v.at[j * TOP_K + kk]], scat_sem)
                        cp.start()
                        scatters.append(cp)
                for cp in scatters:
                    cp.wait()

    return k(src, idx)


def _sc_gather(table, idx):
    n_chunks, win = idx.shape
    n_iter = n_chunks // SC_WORKERS
    assert n_chunks % SC_WORKERS == 0 and n_iter % 2 == 0

    @functools.partial(
        pl.kernel, mesh=_sc_mesh(), out_type=jax.ShapeDtypeStruct((n_chunks * win, LANES), table.dtype),
        scratch_types=[pltpu.VMEM((n_iter, win), I32), pltpu.VMEM((2, win, LANES), table.dtype),
                       pltpu.SemaphoreType.DMA((2,)), pltpu.SemaphoreType.DMA((2,))],
        name="sc_gather")
    def k(table_hbm, idx_hbm, out_hbm, idx_v, rows_v, gather_sem, write_sem):
        wid = _sc_worker()
        c0 = wid * n_iter
        pltpu.sync_copy(idx_hbm.at[wid], idx_v)

        def gather(c, b):
            return pltpu.make_async_copy(table_hbm.at[idx_v.at[c]], rows_v.at[b], gather_sem.at[b])

        def write(c, b):
            return pltpu.make_async_copy(rows_v.at[b], out_hbm.at[pl.ds(pl.multiple_of((c0 + c) * win, win), win)],
                                         write_sem.at[b])

        gather(0, 0).start()

        @pl.loop(0, n_iter, step=2)
        def _(i):
            for b in range(2):
                c = i + b
                other = 1 - b

                @pl.when(c >= 1)
                def _():
                    write(c - 1, other).wait()

                @pl.when(c + 1 < n_iter)
                def _():
                    gather(c + 1, other).start()

                gather(c, b).wait()
                write(c, b).start()

        write(n_iter - 1, (n_iter - 1) % 2).wait()

    return k(table, idx.reshape(SC_WORKERS, n_iter, win))


def _moe_plan(eidx, planes):
    T = eidx.shape[1]
    rank, counts = _rank(eidx, 256)
    counts = counts.reshape(N_EXPERTS).astype(I32)
    nblk = (counts + MOE_ROWS - 1) // MOE_ROWS
    blk_end = jnp.cumsum(nblk)
    blk_start = blk_end - nblk
    nb_max = (T * TOP_K) // MOE_ROWS + N_EXPERTS
    nused = blk_end[-1]
    bidx = jnp.arange(nb_max, dtype=I32)
    last = jnp.minimum(bidx, nused - 1)
    blk_e = jnp.minimum(jnp.sum((blk_end[None, :] <= last[:, None]).astype(I32), axis=1), N_EXPERTS - 1)
    mine = blk_e[:, None] == jnp.arange(N_EXPERTS, dtype=I32)[None, :]
    cnt_b = jnp.sum(jnp.where(mine, counts[None, :], 0), axis=1)
    start_b = jnp.sum(jnp.where(mine, blk_start[None, :], 0), axis=1)
    nvalid = jnp.clip(cnt_b - (bidx - start_b) * MOE_ROWS, 0, MOE_ROWS).astype(I32)
    first = jnp.logical_and(bidx < nused, jnp.concatenate([jnp.ones((1,), bool), blk_e[1:] != blk_e[:-1]]))
    has = counts > 0
    eids = jnp.arange(N_EXPERTS, dtype=I32)
    nxt_or_self = lax.cummin(jnp.where(has, eids, N_EXPERTS), axis=0, reverse=True)
    next_used = jnp.concatenate([nxt_or_self[1:], jnp.full((1,), N_EXPERTS, I32)])
    next_used = jnp.where(next_used < N_EXPERTS, next_used, -1).astype(I32)
    slot = ((jnp.cumsum(has.astype(I32)) - 1) % 2).astype(I32)
    rows_pad = nb_max * MOE_ROWS
    pstart = (blk_start * MOE_ROWS).astype(F32).reshape(N_EXPERTS, 1)
    idx = _pair_rows(eidx, rank, pstart, planes, rows_pad, 256)
    plan = (blk_e.astype(I32), nvalid, nused.reshape(1).astype(I32), first.astype(I32), next_used, slot)
    return idx, plan, rows_pad


def _moe_group(h2p, eidx, w_gate, w_up, w_down):
    planes, T, _ = h2p.shape
    assert (T * TOP_K) % MOE_ROWS == 0 and T % 256 == 0
    idx, plan, rows_pad = _moe_plan(eidx, planes)
    n_win = T // SC_WINDOW
    xs = _sc_dispatch(h2p.reshape(planes * T, LANES), idx, planes * rows_pad, planes, T)
    ys = _experts(xs.reshape(planes, rows_pad, LANES), plan, w_gate, w_up, w_down)
    ysg = _sc_gather(ys.reshape(planes * rows_pad, LANES), idx.reshape(n_win * planes * TOP_K, SC_WINDOW))
    return ysg.reshape(n_win, planes, TOP_K, SC_WINDOW, LANES)


def _rope_tables(pos):
    half = ROPE_DIM // 2
    inv = ROPE_THETA ** (-jnp.arange(half, dtype=F32) / half)
    ang = pos.astype(F32)[:, None] * inv[None, :]
    cos = jnp.tile(jnp.cos(ang), (1, 2 * MLA_HEADS))
    sin = jnp.tile(jnp.sin(ang), (1, 2 * MLA_HEADS))
    return cos, sin


def _rot_cols(w):
    half = ROPE_DIM // 2
    return jnp.concatenate([-w[..., half:], w[..., :half]], axis=-1)


def _layer_weights(w_in, g_mix, g_q, g_kv, w_uq, w_uk, w_uv, s5_D, s5_w_glu, s5_b_glu, g_out_attn, g_out_s5,
                   w_out, g_ffn, w_router, b_router, w_sh_gate, w_sh_up, w_sh_down):
    D = w_in.shape[0]
    o1, o2, o3 = Q_LORA, Q_LORA + KV_LORA, Q_LORA + KV_LORA + ROPE_DIM
    s5w = w_in.shape[1] - o3
    w_rope = w_in[:, o2:o3]
    w_in_ext = jnp.concatenate([w_in[:, :o2], w_in[:, o3:], w_rope, _rot_cols(w_rope)], axis=1).astype(BF16)
    wq = w_uq.reshape(Q_LORA, MLA_HEADS, NOPE_DIM + ROPE_DIM)
    wq_rope = wq[:, :, NOPE_DIM:]
    wr_hi, wr_lo = _split(w_router.T)
    return {
        "s5w": s5w,
        "g_mix": g_mix.reshape(1, D), "w_in": w_in_ext,
        "g_q": g_q.reshape(1, Q_LORA), "g_kv": g_kv.reshape(1, KV_LORA),
        "wq_nope": wq[:, :, :NOPE_DIM].reshape(Q_LORA, MLA_HEADS * NOPE_DIM).astype(BF16),
        "wq_rope": wq_rope.reshape(Q_LORA, MLA_HEADS * ROPE_DIM).astype(BF16),
        "wq_rot": _rot_cols(wq_rope).reshape(Q_LORA, MLA_HEADS * ROPE_DIM).astype(BF16),
        "w_uk": w_uk.reshape(KV_LORA, MLA_HEADS * NOPE_DIM).astype(BF16),
        "w_uv": jnp.transpose(w_uv, (1, 0, 2)).astype(BF16),
        "s5_d": s5_D.reshape(1, s5w), "w_glu": s5_w_glu.astype(BF16), "b_glu": s5_b_glu.reshape(1, s5w),
        "g_out_attn": g_out_attn.reshape(1, -1), "g_out_s5": g_out_s5.reshape(1, s5w),
        "w_out": w_out.astype(BF16), "g_ffn": g_ffn.reshape(1, D),
        "wr_hi": wr_hi, "wr_lo": wr_lo, "b_router": b_router.reshape(N_EXPERTS, 1),
        "w_sg": w_sh_gate.astype(BF16), "w_su": w_sh_up.astype(BF16), "w_sd": w_sh_down.astype(BF16),
    }


def _state_in(s_re, s_im):
    B = s_re.shape[0]
    return jnp.stack([s_re.reshape(B, -1), s_im.reshape(B, -1)], axis=1).astype(F32)


def kernel(x_prompt, x_sample, c_prompt, c_sample, cache_ckv, cache_krope, state_s5_re, state_s5_im, w_ada, b_ada, g_mix, g_ffn, w_in, g_q, w_uq, g_kv, w_uk, w_uv, s5_A_re, s5_A_im, s5_B_re, s5_B_im, s5_C_re, s5_C_im, s5_D, s5_log_dt, s5_w_glu, s5_b_glu, g_out_attn, g_out_s5, w_out, w_router, b_router, w_exp_gate, w_exp_up, w_exp_down, w_sh_gate, w_sh_up, w_sh_down, g_final):
    Bp, Lp, D = x_prompt.shape
    Bs, Ls, _ = x_sample.shape
    depth = w_ada.shape[0]
    assert depth == 1, "single-layer step"
    past = cache_ckv.shape[2]
    G, N = s5_A_re.shape[1:]
    Ts = Bs * Ls
    Tp = Bp * Lp
    l = 0

    wts = _layer_weights(w_in[l], g_mix[l], g_q[l], g_kv[l], w_uq[l], w_uk[l], w_uv[l], s5_D[l], s5_w_glu[l],
                         s5_b_glu[l], g_out_attn[l], g_out_s5[l], w_out[l], g_ffn[l], w_router[l], b_router[l],
                         w_sh_gate[l], w_sh_up[l], w_sh_down[l])

    mod = _ada(jnp.concatenate([c_prompt, c_sample], axis=0), w_ada[l], b_ada[l])
    mod_p = [m.reshape(Bp, 1, D) for m in jnp.split(mod[:Bp], 6, axis=-1)]
    mod_s = [jnp.broadcast_to(m[:, None, :], (Bs, Ls, D)).reshape(1, Ts, D)
             for m in jnp.split(mod[Bp:], 6, axis=-1)]

    tm_p = 256
    tm_pre = 1024
    tm_post = 1024
    cos_p, sin_p = _rope_tables(jnp.arange(Lp))
    cos_s, sin_s = _rope_tables(jnp.tile(past + jnp.arange(Ls), Bs))

    ckv_p, kr_p, u_p, kcat_p, v_p, q_p = _pre(x_prompt, mod_p[0], mod_p[1], cos_p, sin_p, wts, tm_pre)
    olat_p = _attn_prompt(q_p, kcat_p, v_p, 512)
    tabs_p = _s5_tables(s5_A_re[l], s5_A_im[l], s5_B_re[l], s5_B_im[l], s5_C_re[l], s5_C_im[l], s5_log_dt[l], tm_p)
    os5_p, hl_p = _s5(u_p, jnp.zeros((Bp, 2, G * N), F32), tabs_p, wts, tm_p, precise=False)

    xs_rows = x_sample.reshape(1, Ts, D)
    ckv_s, kr_s, u_s, kcat_s, v_s, q_s = _pre(xs_rows, mod_s[0], mod_s[1], cos_s, sin_s, wts, Ts)
    olat_s = _attn_sample(q_s, kcat_s, v_s, cache_ckv[l], cache_krope[l], wts["w_uk"])
    tabs_s = _s5_tables(s5_A_re[l], s5_A_im[l], s5_B_re[l], s5_B_im[l], s5_C_re[l], s5_C_im[l], s5_log_dt[l], Ls)
    os5_s, hl_s = _s5(u_s.reshape(Bs, Ls, -1), _state_in(state_s5_re[l], state_s5_im[l]), tabs_s, wts, Ls,
                      precise=True)
    h2_s, base_s, eidx_s, gate_s = _post(xs_rows, olat_s, os5_s.reshape(1, Ts, -1),
                                         [mod_s[2], mod_s[3], mod_s[4], mod_s[5]], wts, Ts)

    h2_p, base_p, eidx_p, gate_p = _post(x_prompt, olat_p, os5_p, [mod_p[2], mod_p[3], mod_p[4], mod_p[5]], wts, tm_post)
    ysg = _moe_group(jnp.concatenate([h2_p, h2_s], axis=1), jnp.concatenate([eidx_p, eidx_s], axis=1),
                     w_exp_gate[l], w_exp_up[l], w_exp_down[l])
    gate_t = jnp.concatenate([gate_p, gate_s], axis=1).T

    gfin = g_final.reshape(1, D)
    y_p = _final(base_p, ysg, gate_t, mod_p[5], gfin, tm_p, 0)
    y_s = _final(base_s, ysg, gate_t, mod_s[5], gfin, Ts, Tp).reshape(Bs, Ls, D)

    def state_out(hl, B):
        return hl[:, 0].reshape(1, B, G, N), hl[:, 1].reshape(1, B, G, N)

    sre_p, sim_p = state_out(hl_p, Bp)
    sre_s, sim_s = state_out(hl_s, Bs)
    return (y_p, y_s, ckv_p[None], kr_p[None], sre_p, sim_p,
            ckv_s.reshape(1, Bs, Ls, KV_LORA), kr_s.reshape(1, Bs, Ls, ROPE_DIM), sre_s, sim_s)
```

```python
import functools
import math

import numpy as np
import jax
import jax.numpy as jnp
from jax import lax
from jax.experimental import pallas as pl
from jax.experimental.pallas import tpu as pltpu
from jax.experimental.pallas import tpu_sc as plsc

F32 = jnp.float32
BF16 = jnp.bfloat16
I32 = jnp.int32
U32 = jnp.uint32

EPS = 1e-6
CHUNK = 64
MLA_HEADS = 4
NOPE_DIM = 128
ROPE_DIM = 64
V_DIM = 128
Q_LORA = 256
KV_LORA = 256
QK_HEAD = NOPE_DIM + ROPE_DIM
ROPE_THETA = 10000.0
S5_GROUP_CH = 16
S5_STATE = 64
N_EXPERTS = 256
TOP_K = 8
N_EXPERT_GROUPS = 8
TOPK_GROUPS = 4
ROUTED_SCALE = 2.5

S5_SUB = 16
MOE_ROWS = 512
LANES = 128
VMEM_LIMIT = 56 * 1024 * 1024
SC_CORES = 2
SC_SUBCORES = 16
SC_WORKERS = SC_CORES * SC_SUBCORES
SC_WINDOW = 128

NEG_INF = float("-inf")
BIG_I32 = 1 << 30


def _dot(a, b):
    return jnp.dot(a, b, preferred_element_type=F32)


def _dot_nt(a, b):
    return lax.dot_general(a, b, (((1,), (1,)), ((), ())), preferred_element_type=F32)


def _split(a):
    hi = a.astype(BF16)
    lo = (a - hi.astype(F32)).astype(BF16)
    return hi, lo


def _lane_tile(x, n):
    return jnp.concatenate([x] * n, axis=1)


def _rms(x, g):
    return x * lax.rsqrt(jnp.mean(x * x, axis=-1, keepdims=True) + EPS) * g


def _silu(x):
    return x * jax.nn.sigmoid(x)


def _pack_rows(x):
    half = x.shape[1] // 2
    hi = lax.bitcast_convert_type(x[:, :half].astype(BF16).astype(F32), U32)
    lo = lax.bitcast_convert_type(x[:, half:].astype(BF16).astype(F32), U32)
    w = hi | (lo >> 16)
    return [w[:, j * LANES:(j + 1) * LANES] for j in range(half // LANES)]


def _unpack_rows(planes):
    his = [lax.bitcast_convert_type(p & jnp.uint32(0xFFFF0000), F32).astype(BF16) for p in planes]
    los = [lax.bitcast_convert_type(p << 16, F32).astype(BF16) for p in planes]
    return jnp.concatenate(his + los, axis=1)


def _params(sem):
    return pltpu.CompilerParams(dimension_semantics=sem, vmem_limit_bytes=VMEM_LIMIT)


def _const_spec(shape):
    nd = len(shape)
    return pl.BlockSpec(shape, lambda *_: (0,) * nd)


def _ada_kernel(c_ref, whi_ref, wlo_ref, b_ref, o_ref):
    c = c_ref[...]
    s_hi, s_lo = _split(_silu(c))
    w_hi = whi_ref[...]
    o_ref[...] = _dot(s_hi, w_hi) + _dot(s_hi, wlo_ref[...]) + _dot(s_lo, w_hi) + b_ref[...]


def _ada(c, w_ada, b_ada):
    rows, d = c.shape
    n = w_ada.shape[1]
    tn = 512
    w_hi, w_lo = _split(w_ada)
    return pl.pallas_call(
        _ada_kernel,
        grid=(n // tn,),
        in_specs=[_const_spec((rows, d)),
                  pl.BlockSpec((d, tn), lambda j: (0, j)),
                  pl.BlockSpec((d, tn), lambda j: (0, j)),
                  pl.BlockSpec((1, tn), lambda j: (0, j))],
        out_specs=pl.BlockSpec((rows, tn), lambda j: (0, j)),
        out_shape=jax.ShapeDtypeStruct((rows, n), F32),
        compiler_params=_params(("arbitrary",)),
        name="ada",
    )(c, w_hi, w_lo, b_ada.reshape(1, n))


def _pre_kernel(x_ref, sh_ref, sc_ref, g_ref, win_ref, gq_ref, gkv_ref, wqn_ref, wqr_ref, wqt_ref,
                wuk_ref, cos_ref, sin_ref, ckv_ref, kr_ref, u_ref, kcat_ref, v_ref, q_ref, *, scale):
    x = x_ref[0]
    h = _rms(x, g_ref[...]) * (1.0 + sc_ref[0]) + sh_ref[0]
    z = _dot(h.astype(BF16), win_ref[...])
    cq = _rms(z[:, :Q_LORA], gq_ref[...])
    ckv = _rms(z[:, Q_LORA:Q_LORA + KV_LORA], gkv_ref[...])
    o_s5 = Q_LORA + KV_LORA
    s5w = u_ref.shape[-1]
    u_ref[0] = z[:, o_s5:o_s5 + s5w]
    o_r = o_s5 + s5w
    cos = cos_ref[...]
    sin = sin_ref[...]
    kr = z[:, o_r:o_r + ROPE_DIM] * cos[:, :ROPE_DIM] + z[:, o_r + ROPE_DIM:o_r + 2 * ROPE_DIM] * sin[:, :ROPE_DIM]
    ckv_ref[0] = ckv
    kr_ref[0] = kr
    ckvb = ckv.astype(BF16)
    krb = kr.astype(BF16)
    v_ref[0] = ckvb
    kn = _dot(ckvb, wuk_ref[...])
    cqb = cq.astype(BF16)
    qn = _dot(cqb, wqn_ref[...]) * scale
    qr = (_dot(cqb, wqr_ref[...]) * cos + _dot(cqb, wqt_ref[...]) * sin) * scale
    for hd in range(MLA_HEADS):
        kcat_ref[0, hd, :, :NOPE_DIM] = kn[:, hd * NOPE_DIM:(hd + 1) * NOPE_DIM].astype(BF16)
        kcat_ref[0, hd, :, NOPE_DIM:] = krb
        q_ref[0, hd, :, :NOPE_DIM] = qn[:, hd * NOPE_DIM:(hd + 1) * NOPE_DIM].astype(BF16)
        q_ref[0, hd, :, NOPE_DIM:] = qr[:, hd * ROPE_DIM:(hd + 1) * ROPE_DIM].astype(BF16)


def _pre(x, shift, scale_mod, cos_t, sin_t, wts, tm):
    B, L, D = x.shape
    nt = L // tm
    per_row = shift.shape[1] != 1
    mod_spec = (pl.BlockSpec((1, tm, D), lambda b, i: (b, i, 0)) if per_row
                else pl.BlockSpec((1, 1, D), lambda b, i: (b, 0, 0)))
    s5w = wts["s5w"]
    hr = MLA_HEADS * ROPE_DIM
    kern = functools.partial(_pre_kernel, scale=QK_HEAD ** -0.5 * math.log2(math.e))
    consts = [wts["g_mix"], wts["w_in"], wts["g_q"], wts["g_kv"], wts["wq_nope"], wts["wq_rope"],
              wts["wq_rot"], wts["w_uk"]]
    in_specs = [pl.BlockSpec((1, tm, D), lambda b, i: (b, i, 0)), mod_spec, mod_spec]
    in_specs += [_const_spec(c.shape) for c in consts]
    in_specs += [pl.BlockSpec((tm, hr), lambda b, i: (i, 0)), pl.BlockSpec((tm, hr), lambda b, i: (i, 0))]
    out_shape = (jax.ShapeDtypeStruct((B, L, KV_LORA), F32),
                 jax.ShapeDtypeStruct((B, L, ROPE_DIM), F32),
                 jax.ShapeDtypeStruct((B, L, s5w), F32),
                 jax.ShapeDtypeStruct((B, MLA_HEADS, L, QK_HEAD), BF16),
                 jax.ShapeDtypeStruct((B, L, KV_LORA), BF16),
                 jax.ShapeDtypeStruct((B, MLA_HEADS, L, QK_HEAD), BF16))
    out_specs = (pl.BlockSpec((1, tm, KV_LORA), lambda b, i: (b, i, 0)),
                 pl.BlockSpec((1, tm, ROPE_DIM), lambda b, i: (b, i, 0)),
                 pl.BlockSpec((1, tm, s5w), lambda b, i: (b, i, 0)),
                 pl.BlockSpec((1, MLA_HEADS, tm, QK_HEAD), lambda b, i: (b, 0, i, 0)),
                 pl.BlockSpec((1, tm, KV_LORA), lambda b, i: (b, i, 0)),
                 pl.BlockSpec((1, MLA_HEADS, tm, QK_HEAD), lambda b, i: (b, 0, i, 0)))
    return pl.pallas_call(
        kern, grid=(B, nt), in_specs=in_specs, out_specs=out_specs, out_shape=out_shape,
        compiler_params=_params(("arbitrary", "arbitrary")), name="pre",
    )(x, shift, scale_mod, *consts, cos_t, sin_t)


def _attn_kernel(q_ref, k_ref, v_ref, o_ref, m_scr, l_scr, acc_scr, *, t):
    i = pl.program_id(1)
    m_scr[...] = jnp.full(m_scr.shape, NEG_INF, F32)
    l_scr[...] = jnp.zeros(l_scr.shape, F32)
    acc_scr[...] = jnp.zeros(acc_scr.shape, F32)
    visible = (lax.broadcasted_iota(I32, (t, t), 1) // CHUNK) <= (lax.broadcasted_iota(I32, (t, t), 0) // CHUNK)

    def step(j0, masked):
        v = v_ref[0, pl.ds(j0, t), :]
        for hd in range(MLA_HEADS):
            rs = slice(hd * t, (hd + 1) * t)
            s = _dot_nt(q_ref[0, hd], k_ref[0, hd, pl.ds(j0, t), :])
            if masked:
                s = jnp.where(visible, s, NEG_INF)
            m_prev = m_scr[rs]
            m_next = jnp.maximum(m_prev, jnp.max(s, axis=1, keepdims=True))
            alpha = jnp.exp2(m_prev - m_next)
            p = jnp.exp2(s - _lane_tile(m_next, t // LANES))
            l_scr[rs] = alpha * l_scr[rs] + jnp.sum(p, axis=1, keepdims=True)
            m_scr[rs] = m_next
            acc_scr[rs] = acc_scr[rs] * _lane_tile(alpha, KV_LORA // LANES) + _dot(p.astype(BF16), v)

    def body(j, carry):
        step(pl.multiple_of(j * t, t), False)
        return carry

    lax.fori_loop(0, i, body, 0)
    step(pl.multiple_of(i * t, t), True)
    for hd in range(MLA_HEADS):
        rs = slice(hd * t, (hd + 1) * t)
        inv = 1.0 / l_scr[rs]
        o_ref[0, hd] = (acc_scr[rs] * _lane_tile(inv, KV_LORA // LANES)).astype(BF16)


def _attn_prompt(q, kcat, v, t):
    B, H, L, _ = q.shape
    assert L % t == 0 and t % CHUNK == 0
    rows = H * t
    kern = functools.partial(_attn_kernel, t=t)
    resident = pl.Buffered(1)
    return pl.pallas_call(
        kern, grid=(B, L // t),
        in_specs=[pl.BlockSpec((1, H, t, QK_HEAD), lambda b, i: (b, 0, i, 0)),
                  pl.BlockSpec((1, H, L, QK_HEAD), lambda b, i: (b, 0, 0, 0), pipeline_mode=resident),
                  pl.BlockSpec((1, L, KV_LORA), lambda b, i: (b, 0, 0), pipeline_mode=resident)],
        out_specs=pl.BlockSpec((1, H, t, KV_LORA), lambda b, i: (b, 0, i, 0)),
        out_shape=jax.ShapeDtypeStruct((B, H, L, KV_LORA), BF16),
        scratch_shapes=[pltpu.VMEM((rows, LANES), F32), pltpu.VMEM((rows, LANES), F32),
                        pltpu.VMEM((rows, KV_LORA), F32)],
        compiler_params=_params(("arbitrary", "arbitrary")), name="attn_prompt",
    )(q, kcat, v)


def _attn_sample_kernel(q_ref, kn_ref, vn_ref, pc_ref, pr_ref, wuk_ref, o_ref, *, past, lq):
    pc = pc_ref[0].astype(BF16)
    pr = pr_ref[0].astype(BF16)
    kp = _dot(pc, wuk_ref[...]).astype(BF16)
    vn = vn_ref[0]
    qchunk_p = (past + lax.broadcasted_iota(I32, (lq, past), 0)) // CHUNK
    vis_p = lax.broadcasted_iota(I32, (lq, past), 1) // CHUNK <= qchunk_p
    qchunk_n = (past + lax.broadcasted_iota(I32, (lq, lq), 0)) // CHUNK
    vis_n = (past + lax.broadcasted_iota(I32, (lq, lq), 1)) // CHUNK <= qchunk_n
    for hd in range(MLA_HEADS):
        q = q_ref[0, hd]
        s_p = (_dot_nt(q[:, :NOPE_DIM], kp[:, hd * NOPE_DIM:(hd + 1) * NOPE_DIM])
               + _dot_nt(q[:, NOPE_DIM:], pr))
        s_n = _dot_nt(q, kn_ref[0, hd])
        s_p = jnp.where(vis_p, s_p, NEG_INF)
        s_n = jnp.where(vis_n, s_n, NEG_INF)
        m = jnp.maximum(jnp.max(s_p, axis=1, keepdims=True), jnp.max(s_n, axis=1, keepdims=True))
        p_p = jnp.exp2(s_p - m)
        p_n = jnp.exp2(s_n - m)
        l = jnp.sum(p_p, axis=1, keepdims=True) + jnp.sum(p_n, axis=1, keepdims=True)
        o = _dot(p_p.astype(BF16), pc) + _dot(p_n.astype(BF16), vn)
        o_ref[0, hd] = (o / l).astype(BF16)


def _attn_sample(q, kcat, v, past_ckv, past_kr, w_uk):
    B, past, _ = past_ckv.shape
    H = MLA_HEADS
    lq = q.shape[2] // B
    kern = functools.partial(_attn_sample_kernel, past=past, lq=lq)
    return pl.pallas_call(
        kern, grid=(B,),
        in_specs=[pl.BlockSpec((1, H, lq, QK_HEAD), lambda b: (0, 0, b, 0)),
                  pl.BlockSpec((1, H, lq, QK_HEAD), lambda b: (0, 0, b, 0)),
                  pl.BlockSpec((1, lq, KV_LORA), lambda b: (0, b, 0)),
                  pl.BlockSpec((1, past, KV_LORA), lambda b: (b, 0, 0)),
                  pl.BlockSpec((1, past, ROPE_DIM), lambda b: (b, 0, 0)),
                  _const_spec(w_uk.shape)],
        out_specs=pl.BlockSpec((1, H, lq, KV_LORA), lambda b: (0, 0, b, 0)),
        out_shape=jax.ShapeDtypeStruct((1, H, B * lq, KV_LORA), BF16),
        compiler_params=_params(("arbitrary",)), name="attn_sample",
    )(q, kcat, v, past_ckv, past_kr, w_uk)


def _s5_kernel(u_ref, h0_ref, bre_ref, bim_ref, brel_ref, biml_ref, lt_ref, pinr_ref, pini_ref,
               pwr_ref, pwi_ref, a_ref, cre_ref, cim_ref, d_ref, wglu_ref, bglu_ref, gout_ref,
               o_ref, hl_ref, st_scr, cum_scr, hs_scr, *, tm, precise):
    i = pl.program_id(1)
    ns = st_scr.shape[1]
    half = ns // 2
    wh = u_ref.shape[-1] // 2

    @pl.when(i == 0)
    def _():
        st_scr[...] = h0_ref[0]

    u = u_ref[0]
    lt = lt_ref[...]
    ys = []
    for hf in range(2):
        uh = u[:, hf * wh:(hf + 1) * wh]
        if precise:
            u_hi, u_lo = _split(uh)
            bu_re = _dot(u_hi, bre_ref[hf]) + _dot(u_lo, bre_ref[hf]) + _dot(u_hi, brel_ref[hf])
            bu_im = _dot(u_hi, bim_ref[hf]) + _dot(u_lo, bim_ref[hf]) + _dot(u_hi, biml_ref[hf])
        else:
            u_hi = uh.astype(BF16)
            bu_re = _dot(u_hi, bre_ref[hf])
            bu_im = _dot(u_hi, bim_ref[hf])
        sl = slice(hf * half, (hf + 1) * half)
        pr = pinr_ref[:, sl]
        pi = pini_ref[:, sl]
        v_re = pr * bu_re - pi * bu_im
        v_im = pr * bu_im + pi * bu_re
        for part, v in ((0, v_re), (1, v_im)):
            if precise:
                v_hi, v_lo = _split(v)
                c = _dot(lt, v_hi) + _dot(lt, v_lo)
            else:
                c = _dot(lt, v.astype(BF16))
            cum_scr[:, part * ns + hf * half:part * ns + (hf + 1) * half] = c

        re_cols = slice(hf * half, (hf + 1) * half)
        im_cols = slice(ns + hf * half, ns + (hf + 1) * half)
        a_re = a_ref[0:1, sl]
        a_im = a_ref[1:2, sl]
        pw_re = pwr_ref[:, sl]
        pw_im = pwi_ref[:, sl]
        s_re = st_scr[0:1, sl]
        s_im = st_scr[1:2, sl]
        for c in range(tm // S5_SUB):
            rows = slice(c * S5_SUB, (c + 1) * S5_SUB)
            t_re = cum_scr[rows, re_cols] + (a_re * s_re - a_im * s_im)
            t_im = cum_scr[rows, im_cols] + (a_re * s_im + a_im * s_re)
            h_re = pw_re * t_re - pw_im * t_im
            h_im = pw_re * t_im + pw_im * t_re
            hs_scr[rows, re_cols] = h_re.astype(BF16)
            hs_scr[rows, im_cols] = h_im.astype(BF16)
            s_re = h_re[S5_SUB - 1:S5_SUB, :]
            s_im = h_im[S5_SUB - 1:S5_SUB, :]
        st_scr[0:1, sl] = s_re
        st_scr[1:2, sl] = s_im
        hl_ref[0, 0:1, sl] = s_re
        hl_ref[0, 1:2, sl] = s_im
        ys.append(_dot(hs_scr[:, re_cols], cre_ref[hf]) + _dot(hs_scr[:, im_cols], cim_ref[hf]))
    y = jnp.concatenate(ys, axis=1) + d_ref[...] * u
    zg = jax.nn.gelu(y)
    gl = _dot(zg.astype(BF16), wglu_ref[...]) + bglu_ref[...]
    o = zg * jax.nn.sigmoid(gl)
    o_ref[0] = _rms(o, gout_ref[...]).astype(BF16)


def _s5(u, h0, tabs, wts, tm, precise):
    B, L, W = u.shape
    ns = h0.shape[-1]
    consts = [tabs["b_re"], tabs["b_im"], tabs["b_re_lo"], tabs["b_im_lo"], tabs["lt"], tabs["pin_re"],
              tabs["pin_im"], tabs["pw_re"], tabs["pw_im"], tabs["a"], tabs["c_re"], tabs["c_im"],
              wts["s5_d"], wts["w_glu"], wts["b_glu"], wts["g_out_s5"]]
    kern = functools.partial(_s5_kernel, tm=tm, precise=precise)
    return pl.pallas_call(
        kern, grid=(B, L // tm),
        in_specs=[pl.BlockSpec((1, tm, W), lambda b, i: (b, i, 0)),
                  pl.BlockSpec((1, 2, ns), lambda b, i: (b, 0, 0))] + [_const_spec(c.shape) for c in consts],
        out_specs=(pl.BlockSpec((1, tm, W), lambda b, i: (b, i, 0)),
                   pl.BlockSpec((1, 2, ns), lambda b, i: (b, 0, 0))),
        out_shape=(jax.ShapeDtypeStruct((B, L, W), BF16), jax.ShapeDtypeStruct((B, 2, ns), F32)),
        scratch_shapes=[pltpu.VMEM((2, ns), F32), pltpu.VMEM((tm, 2 * ns), F32), pltpu.VMEM((tm, 2 * ns), BF16)],
        compiler_params=_params(("arbitrary", "arbitrary")), name="s5",
    )(u, h0, *consts)


def _s5_tables(a_re_p, a_im_p, b_re_p, b_im_p, c_re_p, c_im_p, log_dt, tm):
    G, N = a_re_p.shape
    CH = b_re_p.shape[-1]
    dt = jnp.exp(log_dt.astype(F32))[:, None]
    lr = a_re_p.astype(F32) * dt
    li = a_im_p.astype(F32) * dt
    er = jnp.exp(lr)
    ab_re, ab_im = er * jnp.cos(li), er * jnp.sin(li)
    lam2 = a_re_p.astype(F32) ** 2 + a_im_p.astype(F32) ** 2
    nr, ni = ab_re - 1.0, ab_im
    f_re = (nr * a_re_p + ni * a_im_p) / lam2
    f_im = (ni * a_re_p - nr * a_im_p) / lam2
    bb_re = f_re[..., None] * b_re_p - f_im[..., None] * b_im_p
    bb_im = f_re[..., None] * b_im_p + f_im[..., None] * b_re_p
    gh = G // 2
    eye = jnp.eye(gh, dtype=F32)

    def blk_b(bb):
        t = bb.reshape(2, gh, N, CH)
        m = jnp.einsum("hgnc,gk->hgckn", t, eye)
        return m.reshape(2, gh * CH, gh * N)

    def blk_c(cc):
        t = cc.reshape(2, gh, CH, N)
        m = jnp.einsum("hgcn,gk->hgnkc", t, eye)
        return m.reshape(2, gh * N, gh * CH)

    b_re_m, b_im_m = blk_b(bb_re), blk_b(bb_im)
    b_re_hi, b_re_lo = _split(b_re_m)
    b_im_hi, b_im_lo = _split(b_im_m)
    s = jnp.arange(S5_SUB, dtype=F32)[:, None, None]

    def powers(sign):
        e = jnp.exp(sign * lr[None] * s)
        return ((e * jnp.cos(sign * li[None] * s)).reshape(S5_SUB, G * N),
                (e * jnp.sin(sign * li[None] * s)).reshape(S5_SUB, G * N))

    pin_re, pin_im = powers(-1.0)
    pw_re, pw_im = powers(1.0)
    reps = tm // S5_SUB
    r = np.arange(tm)
    lt = ((r[:, None] // S5_SUB == r[None, :] // S5_SUB) & (r[None, :] <= r[:, None])).astype(np.float32)
    return {
        "b_re": b_re_hi, "b_im": b_im_hi, "b_re_lo": b_re_lo, "b_im_lo": b_im_lo,
        "lt": jnp.asarray(lt, BF16),
        "pin_re": jnp.tile(pin_re, (reps, 1)), "pin_im": jnp.tile(pin_im, (reps, 1)),
        "pw_re": pw_re, "pw_im": pw_im,
        "a": jnp.stack([ab_re.reshape(G * N), ab_im.reshape(G * N)]),
        "c_re": blk_c(c_re_p.astype(F32)).astype(BF16), "c_im": (-blk_c(c_im_p.astype(F32))).astype(BF16),
    }


def _route(scores, sel):
    E, tm = scores.shape
    ge = E // N_EXPERT_GROUPS
    io_g = lax.broadcasted_iota(I32, (ge, tm), 0)
    gs_rows = []
    for g in range(N_EXPERT_GROUPS):
        sg = sel[g * ge:(g + 1) * ge, :]
        m1 = jnp.max(sg, axis=0, keepdims=True)
        i1 = jnp.min(jnp.where(sg == m1, io_g, BIG_I32), axis=0, keepdims=True)
        m2 = jnp.max(jnp.where(io_g == i1, NEG_INF, sg), axis=0, keepdims=True)
        gs_rows.append(m1 + m2)
    gs = jnp.concatenate(gs_rows, axis=0)
    gio = lax.broadcasted_iota(I32, gs.shape, 0)
    gsel = jnp.zeros(gs.shape, F32)
    for _ in range(TOPK_GROUPS):
        mx = jnp.max(gs, axis=0, keepdims=True)
        ix = jnp.min(jnp.where(gs == mx, gio, BIG_I32), axis=0, keepdims=True)
        hit = gio == ix
        gsel = jnp.where(hit, 1.0, gsel)
        gs = jnp.where(hit, NEG_INF, gs)
    emask = jnp.concatenate([jnp.broadcast_to(gsel[g:g + 1, :], (ge, tm)) for g in range(N_EXPERT_GROUPS)], axis=0)
    cand = jnp.where(emask > 0.0, sel, NEG_INF)
    eio = lax.broadcasted_iota(I32, (E, tm), 0)
    idxs, gates = [], []
    for _ in range(TOP_K):
        mx = jnp.max(cand, axis=0, keepdims=True)
        ix = jnp.min(jnp.where(cand == mx, eio, BIG_I32), axis=0, keepdims=True)
        hit = eio == ix
        gates.append(jnp.sum(jnp.where(hit, scores, 0.0), axis=0, keepdims=True))
        idxs.append(ix)
        cand = jnp.where(hit, NEG_INF, cand)
    eidx = jnp.concatenate(idxs, axis=0)
    gate = jnp.concatenate(gates, axis=0)
    gate = gate / jnp.sum(gate, axis=0, keepdims=True) * ROUTED_SCALE
    return eidx, gate


def _post_kernel(x_ref, ol_ref, os_ref, gtm_ref, shf_ref, scf_ref, gtf_ref, wuv_ref, goa_ref, wout_ref,
                 gffn_ref, wrh_ref, wrl_ref, br_ref, wsg_ref, wsu_ref, wsd_ref, tri_ref, cnt0_ref,
                 h2_ref, base_ref, eidx_ref, gate_ref, rank_ref, cnt_ref, cnt_scr):
    @pl.when(jnp.logical_and(pl.program_id(0) == 0, pl.program_id(1) == 0))
    def _():
        cnt_scr[...] = cnt0_ref[...]

    x = x_ref[0]
    oa = jnp.concatenate([_dot(ol_ref[0, hd], wuv_ref[hd]) for hd in range(MLA_HEADS)], axis=1)
    oan = _rms(oa, goa_ref[...]).astype(BF16)
    wa = oan.shape[1]
    mix = _dot(oan, wout_ref[:wa, :]) + _dot(os_ref[0], wout_ref[wa:, :])
    x1 = x + gtm_ref[0] * mix
    h2 = _rms(x1, gffn_ref[...]) * (1.0 + scf_ref[0]) + shf_ref[0]
    h2_hi, h2_lo = _split(h2)
    for j, plane in enumerate(_pack_rows(h2)):
        h2_ref[j] = plane
    sh = _dot((_silu(_dot(h2_hi, wsg_ref[...])) * _dot(h2_hi, wsu_ref[...])).astype(BF16), wsd_ref[...])
    base_ref[0] = x1 + gtf_ref[0] * sh
    wr_hi = wrh_ref[...]
    logits = _dot_nt(wr_hi, h2_hi) + _dot_nt(wrl_ref[...], h2_hi) + _dot_nt(wr_hi, h2_lo)
    scores = jax.nn.sigmoid(logits)
    eidx, gate = _route(scores, scores + br_ref[...])
    eidx_ref[...] = eidx
    gate_ref[...] = gate

    eio = lax.broadcasted_iota(I32, scores.shape, 0)
    onehot = jnp.zeros(scores.shape, F32)
    for k in range(TOP_K):
        onehot = jnp.where(eio == eidx[k:k + 1, :], 1.0, onehot)
    before = _dot(onehot.astype(BF16), tri_ref[...]) + cnt_scr[...]
    ranks = [jnp.sum(jnp.where(eio == eidx[k:k + 1, :], before, 0.0), axis=0, keepdims=True) for k in range(TOP_K)]
    rank_ref[...] = jnp.concatenate(ranks, axis=0).astype(I32)
    counts = cnt_scr[...] + jnp.sum(onehot, axis=1, keepdims=True)
    cnt_scr[...] = counts
    cnt_ref[...] = counts


def _post(x, o_lat, o_s5, mods, wts, tm, counts0):
    nb, L, D = x.shape
    nt = L // tm
    per_row = mods[0].shape[1] != 1
    mod_spec = (pl.BlockSpec((1, tm, D), lambda b, i: (b, i, 0)) if per_row
                else pl.BlockSpec((1, 1, D), lambda b, i: (b, 0, 0)))
    W = o_s5.shape[-1]
    r = np.arange(tm)
    tri = jnp.asarray((r[:, None] < r[None, :]).astype(np.float32), BF16)
    consts = [wts["w_uv"], wts["g_out_attn"], wts["w_out"], wts["g_ffn"], wts["wr_hi"], wts["wr_lo"],
              wts["b_router"], wts["w_sg"], wts["w_su"], wts["w_sd"], tri, counts0]
    in_specs = [pl.BlockSpec((1, tm, D), lambda b, i: (b, i, 0)),
                pl.BlockSpec((1, MLA_HEADS, tm, KV_LORA), lambda b, i: (b, 0, i, 0)),
                pl.BlockSpec((1, tm, W), lambda b, i: (b, i, 0)),
                mod_spec, mod_spec, mod_spec, mod_spec] + [_const_spec(c.shape) for c in consts]
    planes = D // (2 * LANES)
    out_shape = (jax.ShapeDtypeStruct((planes, nb * L, LANES), U32), jax.ShapeDtypeStruct((nb, L, D), F32),
                 jax.ShapeDtypeStruct((TOP_K, nb * L), I32), jax.ShapeDtypeStruct((TOP_K, nb * L), F32),
                 jax.ShapeDtypeStruct((TOP_K, nb * L), I32), jax.ShapeDtypeStruct((N_EXPERTS, 1), F32))
    out_specs = (pl.BlockSpec((planes, tm, LANES), lambda b, i: (0, b * nt + i, 0)),
                 pl.BlockSpec((1, tm, D), lambda b, i: (b, i, 0)),
                 pl.BlockSpec((TOP_K, tm), lambda b, i: (0, b * nt + i)),
                 pl.BlockSpec((TOP_K, tm), lambda b, i: (0, b * nt + i)),
                 pl.BlockSpec((TOP_K, tm), lambda b, i: (0, b * nt + i)),
                 _const_spec((N_EXPERTS, 1)))
    return pl.pallas_call(
        _post_kernel, grid=(nb, nt), in_specs=in_specs, out_specs=out_specs, out_shape=out_shape,
        scratch_shapes=[pltpu.VMEM((N_EXPERTS, 1), F32)],
        compiler_params=_params(("arbitrary", "arbitrary")), name="post",
    )(x, o_lat, o_s5, *mods, *consts)


def _pos_kernel(eidx_ref, rank_ref, pstart_ref, idx_ref, *, rows_pad):
    eidx = eidx_ref[...]
    tt = eidx.shape[1]
    eio = lax.broadcasted_iota(I32, (N_EXPERTS, tt), 0)
    pstart = pstart_ref[...]
    starts = [jnp.sum(jnp.where(eio == eidx[k:k + 1, :], pstart, 0.0), axis=0, keepdims=True)
              for k in range(TOP_K)]
    pos = jnp.concatenate(starts, axis=0).astype(I32) + rank_ref[...]
    planes = idx_ref.shape[1] // TOP_K
    for w in range(idx_ref.shape[0]):
        for j in range(planes):
            idx_ref[w, j * TOP_K:(j + 1) * TOP_K, :] = pos[:, w * SC_WINDOW:(w + 1) * SC_WINDOW] + j * rows_pad


def _pair_rows(eidx, rank, pstart, planes, rows_pad, tt):
    K, T = eidx.shape
    wpt = tt // SC_WINDOW
    return pl.pallas_call(
        functools.partial(_pos_kernel, rows_pad=rows_pad), grid=(T // tt,),
        in_specs=[pl.BlockSpec((K, tt), lambda i: (0, i)), pl.BlockSpec((K, tt), lambda i: (0, i)),
                  _const_spec((N_EXPERTS, 1))],
        out_specs=pl.BlockSpec((wpt, planes * K, SC_WINDOW), lambda i: (i, 0, 0)),
        out_shape=jax.ShapeDtypeStruct((T // SC_WINDOW, planes * K, SC_WINDOW), I32),
        compiler_params=_params(("arbitrary",)), name="pair_rows",
    )(eidx, rank, pstart)


def _expert_rows(x_ref, o_ref, wg, wu, wd, nvalid, rows):
    planes = x_ref.shape[0]
    x = _unpack_rows([x_ref[j, :rows, :] for j in range(planes)])
    live = lax.broadcasted_iota(I32, x.shape, 0) < nvalid
    x = jnp.where(live, x, jnp.zeros_like(x))
    g = _dot(x, wg.astype(BF16))
    u = _dot(x, wu.astype(BF16))
    y = _dot((_silu(g) * u).astype(BF16), wd.astype(BF16))
    for j, plane in enumerate(_pack_rows(y)):
        o_ref[j, :rows, :] = plane


def _expert_kernel(blk_e_ref, nvalid_ref, nused_ref, first_ref, next_ref, slot_ref,
                   x_ref, wg_hbm, wu_hbm, wd_hbm, o_ref, wg_buf, wu_buf, wd_buf, sem):
    b = pl.program_id(0)
    used = b < nused_ref[0]
    e = blk_e_ref[b]
    slot = slot_ref[e]
    nvalid = nvalid_ref[b]
    half = MOE_ROWS // 2

    def fetch(expert, s):
        return (pltpu.make_async_copy(wg_hbm.at[expert], wg_buf.at[s], sem.at[s, 0]),
                pltpu.make_async_copy(wu_hbm.at[expert], wu_buf.at[s], sem.at[s, 1]),
                pltpu.make_async_copy(wd_hbm.at[expert], wd_buf.at[s], sem.at[s, 2]))

    @pl.when(jnp.logical_and(used, first_ref[b] == 1))
    def _():
        @pl.when(b == 0)
        def _():
            for cp in fetch(e, slot):
                cp.start()

        nxt = next_ref[e]

        @pl.when(nxt >= 0)
        def _():
            for cp in fetch(nxt, 1 - slot):
                cp.start()

        for cp in fetch(e, slot):
            cp.wait()

    @pl.when(jnp.logical_and(used, nvalid > half))
    def _():
        _expert_rows(x_ref, o_ref, wg_buf[slot], wu_buf[slot], wd_buf[slot], nvalid, MOE_ROWS)

    @pl.when(jnp.logical_and(used, nvalid <= half))
    def _():
        _expert_rows(x_ref, o_ref, wg_buf[slot], wu_buf[slot], wd_buf[slot], nvalid, half)
        o_ref[:, half:, :] = jnp.zeros((o_ref.shape[0], MOE_ROWS - half, LANES), o_ref.dtype)


def _experts(xs, plan, wg, wu, wd):
    planes, rows, _ = xs.shape
    nb = rows // MOE_ROWS
    _, D, F = wg.shape

    def xmap(b, blk_e_ref, nvalid_ref, nused_ref, *_):
        return (0, jnp.minimum(b, nused_ref[0] - 1), 0)

    hbm = pl.BlockSpec(memory_space=pl.ANY)
    grid_spec = pltpu.PrefetchScalarGridSpec(
        num_scalar_prefetch=len(plan), grid=(nb,),
        in_specs=[pl.BlockSpec((planes, MOE_ROWS, LANES), xmap), hbm, hbm, hbm],
        out_specs=pl.BlockSpec((planes, MOE_ROWS, LANES), xmap),
        scratch_shapes=[pltpu.VMEM((2, D, F), wg.dtype), pltpu.VMEM((2, D, F), wu.dtype),
                        pltpu.VMEM((2, F, D), wd.dtype), pltpu.SemaphoreType.DMA((2, 3))])
    return pl.pallas_call(
        _expert_kernel, grid_spec=grid_spec, out_shape=jax.ShapeDtypeStruct(xs.shape, U32),
        compiler_params=_params(("arbitrary",)), name="experts",
    )(*plan, xs, wg, wu, wd)


def _final_kernel(base_ref, ys_ref, gate_ref, gtf_ref, gfin_ref, o_ref):
    nwin, planes = ys_ref.shape[:2]
    gate = gate_ref[...]
    routed = jnp.zeros(base_ref.shape[1:], F32)
    for k in range(TOP_K):
        rows = jnp.concatenate([_unpack_rows([ys_ref[w, j, k] for j in range(planes)]) for w in range(nwin)], axis=0)
        routed = routed + gate[:, k:k + 1] * rows.astype(F32)
    y = base_ref[0] + gtf_ref[0] * routed
    o_ref[0] = _rms(y, gfin_ref[...])


def _final(base, ysg, gate_t, gt_f, g_final, tm, row0):
    B, L, D = base.shape
    planes = ysg.shape[1]
    wpt = tm // SC_WINDOW
    nt = L // tm
    off = row0 // tm
    per_row = gt_f.shape[1] != 1
    mod_spec = (pl.BlockSpec((1, tm, D), lambda b, i: (b, i, 0)) if per_row
                else pl.BlockSpec((1, 1, D), lambda b, i: (b, 0, 0)))
    return pl.pallas_call(
        _final_kernel, grid=(B, nt),
        in_specs=[pl.BlockSpec((1, tm, D), lambda b, i: (b, i, 0)),
                  pl.BlockSpec((wpt, planes, TOP_K, SC_WINDOW, LANES), lambda b, i: (off + b * nt + i, 0, 0, 0, 0)),
                  pl.BlockSpec((tm, TOP_K), lambda b, i: (off + b * nt + i, 0)),
                  mod_spec, _const_spec((1, D))],
        out_specs=pl.BlockSpec((1, tm, D), lambda b, i: (b, i, 0)),
        out_shape=jax.ShapeDtypeStruct((B, L, D), F32),
        compiler_params=_params(("arbitrary", "arbitrary")), name="final",
    )(base, ysg, gate_t, gt_f, g_final)


def _sc_mesh():
    return plsc.VectorSubcoreMesh(core_axis_name="c", subcore_axis_name="s")


def _sc_worker():
    return lax.axis_index("s") * SC_CORES + lax.axis_index("c")


def _sc_dispatch(srcs, idx, out_rows):
    planes = srcs[0].shape[0]
    wins = [s.shape[1] // SC_WINDOW for s in srcs]
    n_win = sum(wins)
    n_iter = -(-n_win // SC_WORKERS)
    flat = [s.reshape(planes * s.shape[1], LANES) for s in srcs]

    @functools.partial(
        pl.kernel, mesh=_sc_mesh(), out_type=jax.ShapeDtypeStruct((out_rows, LANES), srcs[0].dtype),
        scratch_types=[pltpu.VMEM((planes * TOP_K, SC_WINDOW), I32),
                       pltpu.VMEM((planes, SC_WINDOW, LANES), srcs[0].dtype),
                       pltpu.SemaphoreType.DMA((planes,)), pltpu.SemaphoreType.DMA],
        name="sc_dispatch")
    def k(*refs):
        src_hbms, (idx_hbm, out_hbm, idx_v, rows_v, load_sem, scat_sem) = refs[:len(srcs)], refs[len(srcs):]
        wid = _sc_worker()

        def window(src_hbm, n_tok, win, local_win):
            t0 = local_win * SC_WINDOW
            loads = [pltpu.make_async_copy(
                src_hbm.at[pl.ds(pl.multiple_of(j * n_tok + t0, SC_WINDOW), SC_WINDOW)],
                rows_v.at[j], load_sem.at[j]) for j in range(planes)]
            for ld in loads:
                ld.start()
            pltpu.sync_copy(idx_hbm.at[win], idx_v)
            scatters = []
            for j in range(planes):
                loads[j].wait()
                for kk in range(TOP_K):
                    cp = pltpu.make_async_copy(rows_v.at[j], out_hbm.at[idx_v.at[j * TOP_K + kk]], scat_sem)
                    cp.start()
                    scatters.append(cp)
            for cp in scatters:
                cp.wait()

        @pl.loop(0, n_iter)
        def _(i):
            win = i * SC_WORKERS + wid
            first = 0
            for src_hbm, src, nw in zip(src_hbms, srcs, wins):
                @pl.when(jnp.logical_and(win >= first, win < first + nw))
                def _(src_hbm=src_hbm, n_tok=src.shape[1], first=first):
                    window(src_hbm, n_tok, win, win - first)
                first += nw

    return k(*flat, idx)


def _sc_gather(table, idx):
    n_chunks, win = idx.shape
    n_iter = n_chunks // SC_WORKERS
    assert n_chunks % SC_WORKERS == 0 and n_iter % 2 == 0

    @functools.partial(
        pl.kernel, mesh=_sc_mesh(), out_type=jax.ShapeDtypeStruct((n_chunks * win, LANES), table.dtype),
        scratch_types=[pltpu.VMEM((n_iter, win), I32), pltpu.VMEM((2, win, LANES), table.dtype),
                       pltpu.SemaphoreType.DMA((2,)), pltpu.SemaphoreType.DMA((2,))],
        name="sc_gather")
    def k(table_hbm, idx_hbm, out_hbm, idx_v, rows_v, gather_sem, write_sem):
        wid = _sc_worker()
        c0 = wid * n_iter
        pltpu.sync_copy(idx_hbm.at[wid], idx_v)

        def gather(c, b):
            return pltpu.make_async_copy(table_hbm.at[idx_v.at[c]], rows_v.at[b], gather_sem.at[b])

        def write(c, b):
            return pltpu.make_async_copy(rows_v.at[b], out_hbm.at[pl.ds(pl.multiple_of((c0 + c) * win, win), win)],
                                         write_sem.at[b])

        gather(0, 0).start()

        @pl.loop(0, n_iter, step=2)
        def _(i):
            for b in range(2):
                c = i + b
                other = 1 - b

                @pl.when(c >= 1)
                def _():
                    write(c - 1, other).wait()

                @pl.when(c + 1 < n_iter)
                def _():
                    gather(c + 1, other).start()

                gather(c, b).wait()
                write(c, b).start()

        write(n_iter - 1, (n_iter - 1) % 2).wait()

    return k(table, idx.reshape(SC_WORKERS, n_iter, win))


def _moe_plan(eidx, rank, counts, planes):
    T = eidx.shape[1]
    counts = counts.reshape(N_EXPERTS).astype(I32)
    nblk = (counts + MOE_ROWS - 1) // MOE_ROWS
    blk_end = jnp.cumsum(nblk)
    blk_start = blk_end - nblk
    nb_max = (T * TOP_K) // MOE_ROWS + N_EXPERTS
    nused = blk_end[-1]
    bidx = jnp.arange(nb_max, dtype=I32)
    last = jnp.minimum(bidx, nused - 1)
    blk_e = jnp.minimum(jnp.sum((blk_end[None, :] <= last[:, None]).astype(I32), axis=1), N_EXPERTS - 1)
    mine = blk_e[:, None] == jnp.arange(N_EXPERTS, dtype=I32)[None, :]
    cnt_b = jnp.sum(jnp.where(mine, counts[None, :], 0), axis=1)
    start_b = jnp.sum(jnp.where(mine, blk_start[None, :], 0), axis=1)
    nvalid = jnp.clip(cnt_b - (bidx - start_b) * MOE_ROWS, 0, MOE_ROWS).astype(I32)
    first = jnp.logical_and(bidx < nused, jnp.concatenate([jnp.ones((1,), bool), blk_e[1:] != blk_e[:-1]]))
    has = counts > 0
    eids = jnp.arange(N_EXPERTS, dtype=I32)
    nxt_or_self = lax.cummin(jnp.where(has, eids, N_EXPERTS), axis=0, reverse=True)
    next_used = jnp.concatenate([nxt_or_self[1:], jnp.full((1,), N_EXPERTS, I32)])
    next_used = jnp.where(next_used < N_EXPERTS, next_used, -1).astype(I32)
    slot = ((jnp.cumsum(has.astype(I32)) - 1) % 2).astype(I32)
    rows_pad = nb_max * MOE_ROWS
    pstart = (blk_start * MOE_ROWS).astype(F32).reshape(N_EXPERTS, 1)
    idx = _pair_rows(eidx, rank, pstart, planes, rows_pad, 256)
    plan = (blk_e.astype(I32), nvalid, nused.reshape(1).astype(I32), first.astype(I32), next_used, slot)
    return idx, plan, rows_pad


def _moe_group(h2ps, eidx, rank, counts, w_gate, w_up, w_down):
    planes = h2ps[0].shape[0]
    T = eidx.shape[1]
    assert (T * TOP_K) % MOE_ROWS == 0 and T % 256 == 0
    idx, plan, rows_pad = _moe_plan(eidx, rank, counts, planes)
    n_win = T // SC_WINDOW
    xs = _sc_dispatch(h2ps, idx, planes * rows_pad)
    ys = _experts(xs.reshape(planes, rows_pad, LANES), plan, w_gate, w_up, w_down)
    ysg = _sc_gather(ys.reshape(planes * rows_pad, LANES), idx.reshape(n_win * planes * TOP_K, SC_WINDOW))
    return ysg.reshape(n_win, planes, TOP_K, SC_WINDOW, LANES)


def _rope_tables(pos):
    half = ROPE_DIM // 2
    inv = ROPE_THETA ** (-jnp.arange(half, dtype=F32) / half)
    ang = pos.astype(F32)[:, None] * inv[None, :]
    cos = jnp.tile(jnp.cos(ang), (1, 2 * MLA_HEADS))
    sin = jnp.tile(jnp.sin(ang), (1, 2 * MLA_HEADS))
    return cos, sin


def _rot_cols(w):
    half = ROPE_DIM // 2
    return jnp.concatenate([-w[..., half:], w[..., :half]], axis=-1)


def _layer_weights(w_in, g_mix, g_q, g_kv, w_uq, w_uk, w_uv, s5_D, s5_w_glu, s5_b_glu, g_out_attn, g_out_s5,
                   w_out, g_ffn, w_router, b_router, w_sh_gate, w_sh_up, w_sh_down):
    D = w_in.shape[0]
    o1, o2, o3 = Q_LORA, Q_LORA + KV_LORA, Q_LORA + KV_LORA + ROPE_DIM
    s5w = w_in.shape[1] - o3
    w_rope = w_in[:, o2:o3]
    w_in_ext = jnp.concatenate([w_in[:, :o2], w_in[:, o3:], w_rope, _rot_cols(w_rope)], axis=1).astype(BF16)
    wq = w_uq.reshape(Q_LORA, MLA_HEADS, NOPE_DIM + ROPE_DIM)
    wq_rope = wq[:, :, NOPE_DIM:]
    wr_hi, wr_lo = _split(w_router.T)
    return {
        "s5w": s5w,
        "g_mix": g_mix.reshape(1, D), "w_in": w_in_ext,
        "g_q": g_q.reshape(1, Q_LORA), "g_kv": g_kv.reshape(1, KV_LORA),
        "wq_nope": wq[:, :, :NOPE_DIM].reshape(Q_LORA, MLA_HEADS * NOPE_DIM).astype(BF16),
        "wq_rope": wq_rope.reshape(Q_LORA, MLA_HEADS * ROPE_DIM).astype(BF16),
        "wq_rot": _rot_cols(wq_rope).reshape(Q_LORA, MLA_HEADS * ROPE_DIM).astype(BF16),
        "w_uk": w_uk.reshape(KV_LORA, MLA_HEADS * NOPE_DIM).astype(BF16),
        "w_uv": jnp.transpose(w_uv, (1, 0, 2)).astype(BF16),
        "s5_d": s5_D.reshape(1, s5w), "w_glu": s5_w_glu.astype(BF16), "b_glu": s5_b_glu.reshape(1, s5w),
        "g_out_attn": g_out_attn.reshape(1, -1), "g_out_s5": g_out_s5.reshape(1, s5w),
        "w_out": w_out.astype(BF16), "g_ffn": g_ffn.reshape(1, D),
        "wr_hi": wr_hi, "wr_lo": wr_lo, "b_router": b_router.reshape(N_EXPERTS, 1),
        "w_sg": w_sh_gate.astype(BF16), "w_su": w_sh_up.astype(BF16), "w_sd": w_sh_down.astype(BF16),
    }


def _state_in(s_re, s_im):
    B = s_re.shape[0]
    return jnp.stack([s_re.reshape(B, -1), s_im.reshape(B, -1)], axis=1).astype(F32)


def kernel(x_prompt, x_sample, c_prompt, c_sample, cache_ckv, cache_krope, state_s5_re, state_s5_im, w_ada, b_ada, g_mix, g_ffn, w_in, g_q, w_uq, g_kv, w_uk, w_uv, s5_A_re, s5_A_im, s5_B_re, s5_B_im, s5_C_re, s5_C_im, s5_D, s5_log_dt, s5_w_glu, s5_b_glu, g_out_attn, g_out_s5, w_out, w_router, b_router, w_exp_gate, w_exp_up, w_exp_down, w_sh_gate, w_sh_up, w_sh_down, g_final):
    Bp, Lp, D = x_prompt.shape
    Bs, Ls, _ = x_sample.shape
    depth = w_ada.shape[0]
    assert depth == 1, "single-layer step"
    past = cache_ckv.shape[2]
    G, N = s5_A_re.shape[1:]
    Ts = Bs * Ls
    Tp = Bp * Lp
    l = 0

    wts = _layer_weights(w_in[l], g_mix[l], g_q[l], g_kv[l], w_uq[l], w_uk[l], w_uv[l], s5_D[l], s5_w_glu[l],
                         s5_b_glu[l], g_out_attn[l], g_out_s5[l], w_out[l], g_ffn[l], w_router[l], b_router[l],
                         w_sh_gate[l], w_sh_up[l], w_sh_down[l])

    mod = _ada(jnp.concatenate([c_prompt, c_sample], axis=0), w_ada[l], b_ada[l])
    mod_p = [m.reshape(Bp, 1, D) for m in jnp.split(mod[:Bp], 6, axis=-1)]
    mod_s = [jnp.broadcast_to(m[:, None, :], (Bs, Ls, D)).reshape(1, Ts, D)
             for m in jnp.split(mod[Bp:], 6, axis=-1)]

    tm_p = 256
    tm_pre = 1024
    tm_post = 1024
    cos_p, sin_p = _rope_tables(jnp.arange(Lp))
    cos_s, sin_s = _rope_tables(jnp.tile(past + jnp.arange(Ls), Bs))

    ckv_p, kr_p, u_p, kcat_p, v_p, q_p = _pre(x_prompt, mod_p[0], mod_p[1], cos_p, sin_p, wts, tm_pre)
    olat_p = _attn_prompt(q_p, kcat_p, v_p, 512)
    tabs_p = _s5_tables(s5_A_re[l], s5_A_im[l], s5_B_re[l], s5_B_im[l], s5_C_re[l], s5_C_im[l], s5_log_dt[l], tm_p)
    os5_p, hl_p = _s5(u_p, jnp.zeros((Bp, 2, G * N), F32), tabs_p, wts, tm_p, precise=False)

    xs_rows = x_sample.reshape(1, Ts, D)
    ckv_s, kr_s, u_s, kcat_s, v_s, q_s = _pre(xs_rows, mod_s[0], mod_s[1], cos_s, sin_s, wts, Ts)
    olat_s = _attn_sample(q_s, kcat_s, v_s, cache_ckv[l], cache_krope[l], wts["w_uk"])
    tabs_s = _s5_tables(s5_A_re[l], s5_A_im[l], s5_B_re[l], s5_B_im[l], s5_C_re[l], s5_C_im[l], s5_log_dt[l], Ls)
    os5_s, hl_s = _s5(u_s.reshape(Bs, Ls, -1), _state_in(state_s5_re[l], state_s5_im[l]), tabs_s, wts, Ls,
                      precise=True)
    h2_p, base_p, eidx_p, gate_p, rank_p, counts_p = _post(
        x_prompt, olat_p, os5_p, [mod_p[2], mod_p[3], mod_p[4], mod_p[5]], wts, tm_post,
        jnp.zeros((N_EXPERTS, 1), F32))
    h2_s, base_s, eidx_s, gate_s, rank_s, counts = _post(
        xs_rows, olat_s, os5_s.reshape(1, Ts, -1), [mod_s[2], mod_s[3], mod_s[4], mod_s[5]], wts, Ts, counts_p)
    ysg = _moe_group([h2_p, h2_s], jnp.concatenate([eidx_p, eidx_s], axis=1),
                     jnp.concatenate([rank_p, rank_s], axis=1), counts, w_exp_gate[l], w_exp_up[l], w_exp_down[l])
    gate_t = jnp.concatenate([gate_p, gate_s], axis=1).T

    gfin = g_final.reshape(1, D)
    y_p = _final(base_p, ysg, gate_t, mod_p[5], gfin, tm_p, 0)
    y_s = _final(base_s, ysg, gate_t, mod_s[5], gfin, Ts, Tp).reshape(Bs, Ls, D)

    def state_out(hl, B):
        return hl[:, 0].reshape(1, B, G, N), hl[:, 1].reshape(1, B, G, N)

    sre_p, sim_p = state_out(hl_p, Bp)
    sre_s, sim_s = state_out(hl_s, Bs)
    return (y_p, y_s, ckv_p[None], kr_p[None], sre_p, sim_p,
            ckv_s.reshape(1, Bs, Ls, KV_LORA), kr_s.reshape(1, Bs, Ls, ROPE_DIM), sre_s, sim_s)
```

```python
import functools
import math

import numpy as np
import jax
import jax.numpy as jnp
from jax import lax
from jax.experimental import pallas as pl
from jax.experimental.pallas import tpu as pltpu
from jax.experimental.pallas import tpu_sc as plsc

F32 = jnp.float32
BF16 = jnp.bfloat16
I32 = jnp.int32
U32 = jnp.uint32

EPS = 1e-6
CHUNK = 64
MLA_HEADS = 4
NOPE_DIM = 128
ROPE_DIM = 64
V_DIM = 128
Q_LORA = 256
KV_LORA = 256
QK_HEAD = NOPE_DIM + ROPE_DIM
ROPE_THETA = 10000.0
S5_GROUP_CH = 16
S5_STATE = 64
N_EXPERTS = 256
TOP_K = 8
N_EXPERT_GROUPS = 8
TOPK_GROUPS = 4
ROUTED_SCALE = 2.5

S5_LT_ROWS = 256
S5_SUB = 16
MOE_ROWS = 512
LANES = 128
VMEM_LIMIT = 56 * 1024 * 1024
SC_CORES = 2
SC_SUBCORES = 16
SC_WORKERS = SC_CORES * SC_SUBCORES
SC_WINDOW = 128

NEG_INF = float("-inf")
BIG_I32 = 1 << 30


def _dot(a, b):
    return jnp.dot(a, b, preferred_element_type=F32)


def _dot_nt(a, b):
    return lax.dot_general(a, b, (((1,), (1,)), ((), ())), preferred_element_type=F32)


def _split(a):
    hi = a.astype(BF16)
    lo = (a - hi.astype(F32)).astype(BF16)
    return hi, lo


def _lane_tile(x, n):
    return jnp.concatenate([x] * n, axis=1)


def _rms(x, g):
    return x * lax.rsqrt(jnp.mean(x * x, axis=-1, keepdims=True) + EPS) * g


def _silu(x):
    return x * jax.nn.sigmoid(x)


def _pack_rows(x):
    half = x.shape[1] // 2
    hi = lax.bitcast_convert_type(x[:, :half].astype(BF16).astype(F32), U32)
    lo = lax.bitcast_convert_type(x[:, half:].astype(BF16).astype(F32), U32)
    w = hi | (lo >> 16)
    return [w[:, j * LANES:(j + 1) * LANES] for j in range(half // LANES)]


def _unpack_rows(planes):
    his = [lax.bitcast_convert_type(p & jnp.uint32(0xFFFF0000), F32).astype(BF16) for p in planes]
    los = [lax.bitcast_convert_type(p << 16, F32).astype(BF16) for p in planes]
    return jnp.concatenate(his + los, axis=1)


def _params(sem):
    return pltpu.CompilerParams(dimension_semantics=sem, vmem_limit_bytes=VMEM_LIMIT)


def _const_spec(shape):
    nd = len(shape)
    return pl.BlockSpec(shape, lambda *_: (0,) * nd)


def _ada_kernel(c_ref, whi_ref, wlo_ref, b_ref, o_ref):
    c = c_ref[...]
    s_hi, s_lo = _split(_silu(c))
    w_hi = whi_ref[...]
    o_ref[...] = _dot(s_hi, w_hi) + _dot(s_hi, wlo_ref[...]) + _dot(s_lo, w_hi) + b_ref[...]


def _ada(c, w_ada, b_ada):
    rows, d = c.shape
    n = w_ada.shape[1]
    tn = 512
    w_hi, w_lo = _split(w_ada)
    return pl.pallas_call(
        _ada_kernel,
        grid=(n // tn,),
        in_specs=[_const_spec((rows, d)),
                  pl.BlockSpec((d, tn), lambda j: (0, j)),
                  pl.BlockSpec((d, tn), lambda j: (0, j)),
                  pl.BlockSpec((1, tn), lambda j: (0, j))],
        out_specs=pl.BlockSpec((rows, tn), lambda j: (0, j)),
        out_shape=jax.ShapeDtypeStruct((rows, n), F32),
        compiler_params=_params(("arbitrary",)),
        name="ada",
    )(c, w_hi, w_lo, b_ada.reshape(1, n))


def _pre_kernel(x_ref, sh_ref, sc_ref, g_ref, win_ref, gq_ref, gkv_ref, wqn_ref, wqr_ref, wqt_ref,
                wuk_ref, cos_ref, sin_ref, ckv_ref, kr_ref, u_ref, kcat_ref, v_ref, q_ref, *, scale):
    x = x_ref[0]
    h = _rms(x, g_ref[...]) * (1.0 + sc_ref[0]) + sh_ref[0]
    z = _dot(h.astype(BF16), win_ref[...])
    cq = _rms(z[:, :Q_LORA], gq_ref[...])
    ckv = _rms(z[:, Q_LORA:Q_LORA + KV_LORA], gkv_ref[...])
    o_s5 = Q_LORA + KV_LORA
    s5w = u_ref.shape[-1]
    u_ref[0] = z[:, o_s5:o_s5 + s5w]
    o_r = o_s5 + s5w
    cos = cos_ref[...]
    sin = sin_ref[...]
    kr = z[:, o_r:o_r + ROPE_DIM] * cos[:, :ROPE_DIM] + z[:, o_r + ROPE_DIM:o_r + 2 * ROPE_DIM] * sin[:, :ROPE_DIM]
    ckv_ref[0] = ckv
    kr_ref[0] = kr
    ckvb = ckv.astype(BF16)
    krb = kr.astype(BF16)
    v_ref[0] = ckvb
    kn = _dot(ckvb, wuk_ref[...])
    cqb = cq.astype(BF16)
    qn = _dot(cqb, wqn_ref[...]) * scale
    qr = (_dot(cqb, wqr_ref[...]) * cos + _dot(cqb, wqt_ref[...]) * sin) * scale
    for hd in range(MLA_HEADS):
        kcat_ref[0, hd, :, :NOPE_DIM] = kn[:, hd * NOPE_DIM:(hd + 1) * NOPE_DIM].astype(BF16)
        kcat_ref[0, hd, :, NOPE_DIM:] = krb
        q_ref[0, hd, :, :NOPE_DIM] = qn[:, hd * NOPE_DIM:(hd + 1) * NOPE_DIM].astype(BF16)
        q_ref[0, hd, :, NOPE_DIM:] = qr[:, hd * ROPE_DIM:(hd + 1) * ROPE_DIM].astype(BF16)


def _pre(x, shift, scale_mod, cos_t, sin_t, wts, tm):
    B, L, D = x.shape
    nt = L // tm
    per_row = shift.shape[1] != 1
    mod_spec = (pl.BlockSpec((1, tm, D), lambda b, i: (b, i, 0)) if per_row
                else pl.BlockSpec((1, 1, D), lambda b, i: (b, 0, 0)))
    s5w = wts["s5w"]
    hr = MLA_HEADS * ROPE_DIM
    kern = functools.partial(_pre_kernel, scale=QK_HEAD ** -0.5 * math.log2(math.e))
    consts = [wts["g_mix"], wts["w_in"], wts["g_q"], wts["g_kv"], wts["wq_nope"], wts["wq_rope"],
              wts["wq_rot"], wts["w_uk"]]
    in_specs = [pl.BlockSpec((1, tm, D), lambda b, i: (b, i, 0)), mod_spec, mod_spec]
    in_specs += [_const_spec(c.shape) for c in consts]
    in_specs += [pl.BlockSpec((tm, hr), lambda b, i: (i, 0)), pl.BlockSpec((tm, hr), lambda b, i: (i, 0))]
    out_shape = (jax.ShapeDtypeStruct((B, L, KV_LORA), F32),
                 jax.ShapeDtypeStruct((B, L, ROPE_DIM), F32),
                 jax.ShapeDtypeStruct((B, L, s5w), F32),
                 jax.ShapeDtypeStruct((B, MLA_HEADS, L, QK_HEAD), BF16),
                 jax.ShapeDtypeStruct((B, L, KV_LORA), BF16),
                 jax.ShapeDtypeStruct((B, MLA_HEADS, L, QK_HEAD), BF16))
    out_specs = (pl.BlockSpec((1, tm, KV_LORA), lambda b, i: (b, i, 0)),
                 pl.BlockSpec((1, tm, ROPE_DIM), lambda b, i: (b, i, 0)),
                 pl.BlockSpec((1, tm, s5w), lambda b, i: (b, i, 0)),
                 pl.BlockSpec((1, MLA_HEADS, tm, QK_HEAD), lambda b, i: (b, 0, i, 0)),
                 pl.BlockSpec((1, tm, KV_LORA), lambda b, i: (b, i, 0)),
                 pl.BlockSpec((1, MLA_HEADS, tm, QK_HEAD), lambda b, i: (b, 0, i, 0)))
    return pl.pallas_call(
        kern, grid=(B, nt), in_specs=in_specs, out_specs=out_specs, out_shape=out_shape,
        compiler_params=_params(("arbitrary", "arbitrary")), name="pre",
    )(x, shift, scale_mod, *consts, cos_t, sin_t)


def _attn_kernel(q_ref, k_ref, v_ref, o_ref, m_scr, l_scr, acc_scr, *, t):
    i = pl.program_id(1)
    m_scr[...] = jnp.full(m_scr.shape, NEG_INF, F32)
    l_scr[...] = jnp.zeros(l_scr.shape, F32)
    acc_scr[...] = jnp.zeros(acc_scr.shape, F32)
    visible = (lax.broadcasted_iota(I32, (t, t), 1) // CHUNK) <= (lax.broadcasted_iota(I32, (t, t), 0) // CHUNK)

    def step(j0, masked):
        v = v_ref[0, pl.ds(j0, t), :]
        for hd in range(MLA_HEADS):
            rs = slice(hd * t, (hd + 1) * t)
            s = _dot_nt(q_ref[0, hd], k_ref[0, hd, pl.ds(j0, t), :])
            if masked:
                s = jnp.where(visible, s, NEG_INF)
            m_prev = m_scr[rs]
            m_next = jnp.maximum(m_prev, jnp.max(s, axis=1, keepdims=True))
            alpha = jnp.exp2(m_prev - m_next)
            p = jnp.exp2(s - _lane_tile(m_next, t // LANES))
            l_scr[rs] = alpha * l_scr[rs] + jnp.sum(p, axis=1, keepdims=True)
            m_scr[rs] = m_next
            acc_scr[rs] = acc_scr[rs] * _lane_tile(alpha, KV_LORA // LANES) + _dot(p.astype(BF16), v)

    def body(j, carry):
        step(pl.multiple_of(j * t, t), False)
        return carry

    lax.fori_loop(0, i, body, 0)
    step(pl.multiple_of(i * t, t), True)
    for hd in range(MLA_HEADS):
        rs = slice(hd * t, (hd + 1) * t)
        inv = 1.0 / l_scr[rs]
        o_ref[0, hd] = (acc_scr[rs] * _lane_tile(inv, KV_LORA // LANES)).astype(BF16)


def _attn_prompt(q, kcat, v, t):
    B, H, L, _ = q.shape
    assert L % t == 0 and t % CHUNK == 0
    rows = H * t
    kern = functools.partial(_attn_kernel, t=t)
    resident = pl.Buffered(1)
    return pl.pallas_call(
        kern, grid=(B, L // t),
        in_specs=[pl.BlockSpec((1, H, t, QK_HEAD), lambda b, i: (b, 0, i, 0)),
                  pl.BlockSpec((1, H, L, QK_HEAD), lambda b, i: (b, 0, 0, 0), pipeline_mode=resident),
                  pl.BlockSpec((1, L, KV_LORA), lambda b, i: (b, 0, 0), pipeline_mode=resident)],
        out_specs=pl.BlockSpec((1, H, t, KV_LORA), lambda b, i: (b, 0, i, 0)),
        out_shape=jax.ShapeDtypeStruct((B, H, L, KV_LORA), BF16),
        scratch_shapes=[pltpu.VMEM((rows, LANES), F32), pltpu.VMEM((rows, LANES), F32),
                        pltpu.VMEM((rows, KV_LORA), F32)],
        compiler_params=_params(("arbitrary", "arbitrary")), name="attn_prompt",
    )(q, kcat, v)


def _attn_sample_kernel(q_ref, kn_ref, vn_ref, pc_ref, pr_ref, wuk_ref, o_ref, *, past, lq):
    pc = pc_ref[0].astype(BF16)
    pr = pr_ref[0].astype(BF16)
    kp = _dot(pc, wuk_ref[...]).astype(BF16)
    vn = vn_ref[0]
    qchunk_p = (past + lax.broadcasted_iota(I32, (lq, past), 0)) // CHUNK
    vis_p = lax.broadcasted_iota(I32, (lq, past), 1) // CHUNK <= qchunk_p
    qchunk_n = (past + lax.broadcasted_iota(I32, (lq, lq), 0)) // CHUNK
    vis_n = (past + lax.broadcasted_iota(I32, (lq, lq), 1)) // CHUNK <= qchunk_n
    for hd in range(MLA_HEADS):
        q = q_ref[0, hd]
        s_p = (_dot_nt(q[:, :NOPE_DIM], kp[:, hd * NOPE_DIM:(hd + 1) * NOPE_DIM])
               + _dot_nt(q[:, NOPE_DIM:], pr))
        s_n = _dot_nt(q, kn_ref[0, hd])
        s_p = jnp.where(vis_p, s_p, NEG_INF)
        s_n = jnp.where(vis_n, s_n, NEG_INF)
        m = jnp.maximum(jnp.max(s_p, axis=1, keepdims=True), jnp.max(s_n, axis=1, keepdims=True))
        p_p = jnp.exp2(s_p - m)
        p_n = jnp.exp2(s_n - m)
        l = jnp.sum(p_p, axis=1, keepdims=True) + jnp.sum(p_n, axis=1, keepdims=True)
        o = _dot(p_p.astype(BF16), pc) + _dot(p_n.astype(BF16), vn)
        o_ref[0, hd] = (o / l).astype(BF16)


def _attn_sample(q, kcat, v, past_ckv, past_kr, w_uk):
    B, past, _ = past_ckv.shape
    H = MLA_HEADS
    lq = q.shape[2] // B
    kern = functools.partial(_attn_sample_kernel, past=past, lq=lq)
    return pl.pallas_call(
        kern, grid=(B,),
        in_specs=[pl.BlockSpec((1, H, lq, QK_HEAD), lambda b: (0, 0, b, 0)),
                  pl.BlockSpec((1, H, lq, QK_HEAD), lambda b: (0, 0, b, 0)),
                  pl.BlockSpec((1, lq, KV_LORA), lambda b: (0, b, 0)),
                  pl.BlockSpec((1, past, KV_LORA), lambda b: (b, 0, 0)),
                  pl.BlockSpec((1, past, ROPE_DIM), lambda b: (b, 0, 0)),
                  _const_spec(w_uk.shape)],
        out_specs=pl.BlockSpec((1, H, lq, KV_LORA), lambda b: (0, 0, b, 0)),
        out_shape=jax.ShapeDtypeStruct((1, H, B * lq, KV_LORA), BF16),
        compiler_params=_params(("arbitrary",)), name="attn_sample",
    )(q, kcat, v, past_ckv, past_kr, w_uk)


def _s5_kernel(u_ref, h0_ref, bre_ref, bim_ref, brel_ref, biml_ref, lt_ref, pinr_ref, pini_ref,
               pwr_ref, pwi_ref, a_ref, cre_ref, cim_ref, d_ref, wglu_ref, bglu_ref, gout_ref,
               o_ref, hl_ref, st_scr, cum_scr, hs_scr, *, tm, precise):
    i = pl.program_id(1)
    ns = st_scr.shape[1]
    half = ns // 2
    wh = u_ref.shape[-1] // 2

    @pl.when(i == 0)
    def _():
        st_scr[...] = h0_ref[0]

    u = u_ref[0]
    lt = lt_ref[...]
    ys = []
    for hf in range(2):
        uh = u[:, hf * wh:(hf + 1) * wh]
        if precise:
            u_hi, u_lo = _split(uh)
            bu_re = _dot(u_hi, bre_ref[hf]) + _dot(u_lo, bre_ref[hf]) + _dot(u_hi, brel_ref[hf])
            bu_im = _dot(u_hi, bim_ref[hf]) + _dot(u_lo, bim_ref[hf]) + _dot(u_hi, biml_ref[hf])
        else:
            u_hi = uh.astype(BF16)
            bu_re = _dot(u_hi, bre_ref[hf])
            bu_im = _dot(u_hi, bim_ref[hf])
        sl = slice(hf * half, (hf + 1) * half)
        pr = pinr_ref[:, sl][None]
        pi = pini_ref[:, sl][None]
        bu_re = bu_re.reshape(tm // S5_SUB, S5_SUB, half)
        bu_im = bu_im.reshape(tm // S5_SUB, S5_SUB, half)
        v_re = (pr * bu_re - pi * bu_im).reshape(tm, half)
        v_im = (pr * bu_im + pi * bu_re).reshape(tm, half)
        ltr = lt.shape[0]
        for part, v in ((0, v_re), (1, v_im)):
            for r0 in range(0, tm, ltr):
                vb = v[r0:r0 + ltr]
                if precise:
                    v_hi, v_lo = _split(vb)
                    c = _dot(lt, v_hi) + _dot(lt, v_lo)
                else:
                    c = _dot(lt, vb.astype(BF16))
                cum_scr[r0:r0 + ltr, part * ns + hf * half:part * ns + (hf + 1) * half] = c

        re_cols = slice(hf * half, (hf + 1) * half)
        im_cols = slice(ns + hf * half, ns + (hf + 1) * half)
        a_re = a_ref[0:1, sl]
        a_im = a_ref[1:2, sl]
        pw_re = pwr_ref[:, sl]
        pw_im = pwi_ref[:, sl]
        s_re = st_scr[0:1, sl]
        s_im = st_scr[1:2, sl]
        for c in range(tm // S5_SUB):
            rows = slice(c * S5_SUB, (c + 1) * S5_SUB)
            t_re = cum_scr[rows, re_cols] + (a_re * s_re - a_im * s_im)
            t_im = cum_scr[rows, im_cols] + (a_re * s_im + a_im * s_re)
            h_re = pw_re * t_re - pw_im * t_im
            h_im = pw_re * t_im + pw_im * t_re
            hs_scr[rows, re_cols] = h_re.astype(BF16)
            hs_scr[rows, im_cols] = h_im.astype(BF16)
            s_re = h_re[S5_SUB - 1:S5_SUB, :]
            s_im = h_im[S5_SUB - 1:S5_SUB, :]
        st_scr[0:1, sl] = s_re
        st_scr[1:2, sl] = s_im
        hl_ref[0, 0:1, sl] = s_re
        hl_ref[0, 1:2, sl] = s_im
        ys.append(_dot(hs_scr[:, re_cols], cre_ref[hf]) + _dot(hs_scr[:, im_cols], cim_ref[hf]))
    y = jnp.concatenate(ys, axis=1) + d_ref[...] * u
    zg = jax.nn.gelu(y)
    gl = _dot(zg.astype(BF16), wglu_ref[...]) + bglu_ref[...]
    o = zg * jax.nn.sigmoid(gl)
    o_ref[0] = _rms(o, gout_ref[...]).astype(BF16)


def _s5(u, h0, tabs, wts, tm, precise):
    B, L, W = u.shape
    ns = h0.shape[-1]
    consts = [tabs["b_re"], tabs["b_im"], tabs["b_re_lo"], tabs["b_im_lo"], tabs["lt"], tabs["pin_re"],
              tabs["pin_im"], tabs["pw_re"], tabs["pw_im"], tabs["a"], tabs["c_re"], tabs["c_im"],
              wts["s5_d"], wts["w_glu"], wts["b_glu"], wts["g_out_s5"]]
    kern = functools.partial(_s5_kernel, tm=tm, precise=precise)
    return pl.pallas_call(
        kern, grid=(B, L // tm),
        in_specs=[pl.BlockSpec((1, tm, W), lambda b, i: (b, i, 0)),
                  pl.BlockSpec((1, 2, ns), lambda b, i: (b, 0, 0))] + [_const_spec(c.shape) for c in consts],
        out_specs=(pl.BlockSpec((1, tm, W), lambda b, i: (b, i, 0)),
                   pl.BlockSpec((1, 2, ns), lambda b, i: (b, 0, 0))),
        out_shape=(jax.ShapeDtypeStruct((B, L, W), BF16), jax.ShapeDtypeStruct((B, 2, ns), F32)),
        scratch_shapes=[pltpu.VMEM((2, ns), F32), pltpu.VMEM((tm, 2 * ns), F32), pltpu.VMEM((tm, 2 * ns), BF16)],
        compiler_params=_params(("arbitrary", "arbitrary")), name="s5",
    )(u, h0, *consts)


def _s5_tables(a_re_p, a_im_p, b_re_p, b_im_p, c_re_p, c_im_p, log_dt, tm):
    G, N = a_re_p.shape
    CH = b_re_p.shape[-1]
    dt = jnp.exp(log_dt.astype(F32))[:, None]
    lr = a_re_p.astype(F32) * dt
    li = a_im_p.astype(F32) * dt
    er = jnp.exp(lr)
    ab_re, ab_im = er * jnp.cos(li), er * jnp.sin(li)
    lam2 = a_re_p.astype(F32) ** 2 + a_im_p.astype(F32) ** 2
    nr, ni = ab_re - 1.0, ab_im
    f_re = (nr * a_re_p + ni * a_im_p) / lam2
    f_im = (ni * a_re_p - nr * a_im_p) / lam2
    bb_re = f_re[..., None] * b_re_p - f_im[..., None] * b_im_p
    bb_im = f_re[..., None] * b_im_p + f_im[..., None] * b_re_p
    gh = G // 2
    eye = jnp.eye(gh, dtype=F32)

    def blk_b(bb):
        t = bb.reshape(2, gh, N, CH)
        m = jnp.einsum("hgnc,gk->hgckn", t, eye)
        return m.reshape(2, gh * CH, gh * N)

    def blk_c(cc):
        t = cc.reshape(2, gh, CH, N)
        m = jnp.einsum("hgcn,gk->hgnkc", t, eye)
        return m.reshape(2, gh * N, gh * CH)

    b_re_m, b_im_m = blk_b(bb_re), blk_b(bb_im)
    b_re_hi, b_re_lo = _split(b_re_m)
    b_im_hi, b_im_lo = _split(b_im_m)
    s = jnp.arange(S5_SUB, dtype=F32)[:, None, None]

    def powers(sign):
        e = jnp.exp(sign * lr[None] * s)
        return ((e * jnp.cos(sign * li[None] * s)).reshape(S5_SUB, G * N),
                (e * jnp.sin(sign * li[None] * s)).reshape(S5_SUB, G * N))

    pin_re, pin_im = powers(-1.0)
    pw_re, pw_im = powers(1.0)
    r = np.arange(min(tm, S5_LT_ROWS))
    lt = ((r[:, None] // S5_SUB == r[None, :] // S5_SUB) & (r[None, :] <= r[:, None])).astype(np.float32)
    return {
        "b_re": b_re_hi, "b_im": b_im_hi, "b_re_lo": b_re_lo, "b_im_lo": b_im_lo,
        "lt": jnp.asarray(lt, BF16),
        "pin_re": pin_re, "pin_im": pin_im,
        "pw_re": pw_re, "pw_im": pw_im,
        "a": jnp.stack([ab_re.reshape(G * N), ab_im.reshape(G * N)]),
        "c_re": blk_c(c_re_p.astype(F32)).astype(BF16), "c_im": (-blk_c(c_im_p.astype(F32))).astype(BF16),
    }


def _route(scores, sel):
    E, tm = scores.shape
    ge = E // N_EXPERT_GROUPS
    io_g = lax.broadcasted_iota(I32, (ge, tm), 0)
    gs_rows = []
    for g in range(N_EXPERT_GROUPS):
        sg = sel[g * ge:(g + 1) * ge, :]
        m1 = jnp.max(sg, axis=0, keepdims=True)
        i1 = jnp.min(jnp.where(sg == m1, io_g, BIG_I32), axis=0, keepdims=True)
        m2 = jnp.max(jnp.where(io_g == i1, NEG_INF, sg), axis=0, keepdims=True)
        gs_rows.append(m1 + m2)
    gs = jnp.concatenate(gs_rows, axis=0)
    gio = lax.broadcasted_iota(I32, gs.shape, 0)
    gsel = jnp.zeros(gs.shape, F32)
    for _ in range(TOPK_GROUPS):
        mx = jnp.max(gs, axis=0, keepdims=True)
        ix = jnp.min(jnp.where(gs == mx, gio, BIG_I32), axis=0, keepdims=True)
        hit = gio == ix
        gsel = jnp.where(hit, 1.0, gsel)
        gs = jnp.where(hit, NEG_INF, gs)
    emask = jnp.concatenate([jnp.broadcast_to(gsel[g:g + 1, :], (ge, tm)) for g in range(N_EXPERT_GROUPS)], axis=0)
    cand = jnp.where(emask > 0.0, sel, NEG_INF)
    eio = lax.broadcasted_iota(I32, (E, tm), 0)
    idxs, gates = [], []
    for _ in range(TOP_K):
        mx = jnp.max(cand, axis=0, keepdims=True)
        ix = jnp.min(jnp.where(cand == mx, eio, BIG_I32), axis=0, keepdims=True)
        hit = eio == ix
        gates.append(jnp.sum(jnp.where(hit, scores, 0.0), axis=0, keepdims=True))
        idxs.append(ix)
        cand = jnp.where(hit, NEG_INF, cand)
    eidx = jnp.concatenate(idxs, axis=0)
    gate = jnp.concatenate(gates, axis=0)
    gate = gate / jnp.sum(gate, axis=0, keepdims=True) * ROUTED_SCALE
    return eidx, gate


def _post_kernel(x_ref, ol_ref, os_ref, gtm_ref, shf_ref, scf_ref, gtf_ref, wuv_ref, goa_ref, wout_ref,
                 gffn_ref, wrh_ref, wrl_ref, br_ref, wsg_ref, wsu_ref, wsd_ref, tri_ref, cnt0_ref,
                 h2_ref, base_ref, eidx_ref, gate_ref, rank_ref, cnt_ref, cnt_scr):
    @pl.when(jnp.logical_and(pl.program_id(0) == 0, pl.program_id(1) == 0))
    def _():
        cnt_scr[...] = cnt0_ref[...]

    x = x_ref[0]
    oa = jnp.concatenate([_dot(ol_ref[0, hd], wuv_ref[hd]) for hd in range(MLA_HEADS)], axis=1)
    oan = _rms(oa, goa_ref[...]).astype(BF16)
    wa = oan.shape[1]
    mix = _dot(oan, wout_ref[:wa, :]) + _dot(os_ref[0], wout_ref[wa:, :])
    x1 = x + gtm_ref[0] * mix
    h2 = _rms(x1, gffn_ref[...]) * (1.0 + scf_ref[0]) + shf_ref[0]
    h2_hi, h2_lo = _split(h2)
    for j, plane in enumerate(_pack_rows(h2)):
        h2_ref[j] = plane
    sh = _dot((_silu(_dot(h2_hi, wsg_ref[...])) * _dot(h2_hi, wsu_ref[...])).astype(BF16), wsd_ref[...])
    base_ref[0] = x1 + gtf_ref[0] * sh
    wr_hi = wrh_ref[...]
    logits = _dot_nt(wr_hi, h2_hi) + _dot_nt(wrl_ref[...], h2_hi) + _dot_nt(wr_hi, h2_lo)
    scores = jax.nn.sigmoid(logits)
    eidx, gate = _route(scores, scores + br_ref[...])
    eidx_ref[...] = eidx
    gate_ref[...] = gate

    eio = lax.broadcasted_iota(I32, scores.shape, 0)
    onehot = jnp.zeros(scores.shape, F32)
    for k in range(TOP_K):
        onehot = jnp.where(eio == eidx[k:k + 1, :], 1.0, onehot)
    before = _dot(onehot.astype(BF16), tri_ref[...]) + cnt_scr[...]
    ranks = [jnp.sum(jnp.where(eio == eidx[k:k + 1, :], before, 0.0), axis=0, keepdims=True) for k in range(TOP_K)]
    rank_ref[...] = jnp.concatenate(ranks, axis=0).astype(I32)
    counts = cnt_scr[...] + jnp.sum(onehot, axis=1, keepdims=True)
    cnt_scr[...] = counts
    cnt_ref[...] = counts


def _post(x, o_lat, o_s5, mods, wts, tm, counts0):
    nb, L, D = x.shape
    nt = L // tm
    per_row = mods[0].shape[1] != 1
    mod_spec = (pl.BlockSpec((1, tm, D), lambda b, i: (b, i, 0)) if per_row
                else pl.BlockSpec((1, 1, D), lambda b, i: (b, 0, 0)))
    W = o_s5.shape[-1]
    r = np.arange(tm)
    tri = jnp.asarray((r[:, None] < r[None, :]).astype(np.float32), BF16)
    consts = [wts["w_uv"], wts["g_out_attn"], wts["w_out"], wts["g_ffn"], wts["wr_hi"], wts["wr_lo"],
              wts["b_router"], wts["w_sg"], wts["w_su"], wts["w_sd"], tri, counts0]
    in_specs = [pl.BlockSpec((1, tm, D), lambda b, i: (b, i, 0)),
                pl.BlockSpec((1, MLA_HEADS, tm, KV_LORA), lambda b, i: (b, 0, i, 0)),
                pl.BlockSpec((1, tm, W), lambda b, i: (b, i, 0)),
                mod_spec, mod_spec, mod_spec, mod_spec] + [_const_spec(c.shape) for c in consts]
    planes = D // (2 * LANES)
    out_shape = (jax.ShapeDtypeStruct((planes, nb * L, LANES), U32), jax.ShapeDtypeStruct((nb, L, D), F32),
                 jax.ShapeDtypeStruct((TOP_K, nb * L), I32), jax.ShapeDtypeStruct((TOP_K, nb * L), F32),
                 jax.ShapeDtypeStruct((TOP_K, nb * L), I32), jax.ShapeDtypeStruct((N_EXPERTS, 1), F32))
    out_specs = (pl.BlockSpec((planes, tm, LANES), lambda b, i: (0, b * nt + i, 0)),
                 pl.BlockSpec((1, tm, D), lambda b, i: (b, i, 0)),
                 pl.BlockSpec((TOP_K, tm), lambda b, i: (0, b * nt + i)),
                 pl.BlockSpec((TOP_K, tm), lambda b, i: (0, b * nt + i)),
                 pl.BlockSpec((TOP_K, tm), lambda b, i: (0, b * nt + i)),
                 _const_spec((N_EXPERTS, 1)))
    return pl.pallas_call(
        _post_kernel, grid=(nb, nt), in_specs=in_specs, out_specs=out_specs, out_shape=out_shape,
        scratch_shapes=[pltpu.VMEM((N_EXPERTS, 1), F32)],
        compiler_params=_params(("arbitrary", "arbitrary")), name="post",
    )(x, o_lat, o_s5, *mods, *consts)


def _pos_kernel(eidx_ref, rank_ref, pstart_ref, idx_ref, *, rows_pad):
    eidx = eidx_ref[...]
    tt = eidx.shape[1]
    eio = lax.broadcasted_iota(I32, (N_EXPERTS, tt), 0)
    pstart = pstart_ref[...]
    starts = [jnp.sum(jnp.where(eio == eidx[k:k + 1, :], pstart, 0.0), axis=0, keepdims=True)
              for k in range(TOP_K)]
    pos = jnp.concatenate(starts, axis=0).astype(I32) + rank_ref[...]
    planes = idx_ref.shape[1] // TOP_K
    for w in range(idx_ref.shape[0]):
        for j in range(planes):
            idx_ref[w, j * TOP_K:(j + 1) * TOP_K, :] = pos[:, w * SC_WINDOW:(w + 1) * SC_WINDOW] + j * rows_pad


def _pair_rows(eidx, rank, pstart, planes, rows_pad, tt):
    K, T = eidx.shape
    wpt = tt // SC_WINDOW
    return pl.pallas_call(
        functools.partial(_pos_kernel, rows_pad=rows_pad), grid=(T // tt,),
        in_specs=[pl.BlockSpec((K, tt), lambda i: (0, i)), pl.BlockSpec((K, tt), lambda i: (0, i)),
                  _const_spec((N_EXPERTS, 1))],
        out_specs=pl.BlockSpec((wpt, planes * K, SC_WINDOW), lambda i: (i, 0, 0)),
        out_shape=jax.ShapeDtypeStruct((T // SC_WINDOW, planes * K, SC_WINDOW), I32),
        compiler_params=_params(("arbitrary",)), name="pair_rows",
    )(eidx, rank, pstart)


def _expert_rows(x_ref, o_ref, wg, wu, wd, nvalid, rows):
    planes = x_ref.shape[0]
    x = _unpack_rows([x_ref[j, :rows, :] for j in range(planes)])
    live = lax.broadcasted_iota(I32, x.shape, 0) < nvalid
    x = jnp.where(live, x, jnp.zeros_like(x))
    g = _dot(x, wg.astype(BF16))
    u = _dot(x, wu.astype(BF16))
    y = _dot((_silu(g) * u).astype(BF16), wd.astype(BF16))
    for j, plane in enumerate(_pack_rows(y)):
        o_ref[j, :rows, :] = plane


def _expert_kernel(blk_e_ref, nvalid_ref, nused_ref, first_ref, next_ref, slot_ref,
                   x_ref, wg_hbm, wu_hbm, wd_hbm, o_ref, wg_buf, wu_buf, wd_buf, sem):
    b = pl.program_id(0)
    used = b < nused_ref[0]
    e = blk_e_ref[b]
    slot = slot_ref[e]
    nvalid = nvalid_ref[b]
    half = MOE_ROWS // 2

    def fetch(expert, s):
        return (pltpu.make_async_copy(wg_hbm.at[expert], wg_buf.at[s], sem.at[s, 0]),
                pltpu.make_async_copy(wu_hbm.at[expert], wu_buf.at[s], sem.at[s, 1]),
                pltpu.make_async_copy(wd_hbm.at[expert], wd_buf.at[s], sem.at[s, 2]))

    @pl.when(jnp.logical_and(used, first_ref[b] == 1))
    def _():
        @pl.when(b == 0)
        def _():
            for cp in fetch(e, slot):
                cp.start()

        nxt = next_ref[e]

        @pl.when(nxt >= 0)
        def _():
            for cp in fetch(nxt, 1 - slot):
                cp.start()

        for cp in fetch(e, slot):
            cp.wait()

    @pl.when(jnp.logical_and(used, nvalid > half))
    def _():
        _expert_rows(x_ref, o_ref, wg_buf[slot], wu_buf[slot], wd_buf[slot], nvalid, MOE_ROWS)

    @pl.when(jnp.logical_and(used, nvalid <= half))
    def _():
        _expert_rows(x_ref, o_ref, wg_buf[slot], wu_buf[slot], wd_buf[slot], nvalid, half)
        o_ref[:, half:, :] = jnp.zeros((o_ref.shape[0], MOE_ROWS - half, LANES), o_ref.dtype)


def _experts(xs, plan, wg, wu, wd):
    planes, rows, _ = xs.shape
    nb = rows // MOE_ROWS
    _, D, F = wg.shape

    def xmap(b, blk_e_ref, nvalid_ref, nused_ref, *_):
        return (0, jnp.minimum(b, nused_ref[0] - 1), 0)

    hbm = pl.BlockSpec(memory_space=pl.ANY)
    grid_spec = pltpu.PrefetchScalarGridSpec(
        num_scalar_prefetch=len(plan), grid=(nb,),
        in_specs=[pl.BlockSpec((planes, MOE_ROWS, LANES), xmap), hbm, hbm, hbm],
        out_specs=pl.BlockSpec((planes, MOE_ROWS, LANES), xmap),
        scratch_shapes=[pltpu.VMEM((2, D, F), wg.dtype), pltpu.VMEM((2, D, F), wu.dtype),
                        pltpu.VMEM((2, F, D), wd.dtype), pltpu.SemaphoreType.DMA((2, 3))])
    return pl.pallas_call(
        _expert_kernel, grid_spec=grid_spec, out_shape=jax.ShapeDtypeStruct(xs.shape, U32),
        compiler_params=_params(("arbitrary",)), name="experts",
    )(*plan, xs, wg, wu, wd)


def _final_kernel(base_ref, ys_ref, gate_ref, gtf_ref, gfin_ref, o_ref):
    nwin, planes = ys_ref.shape[:2]
    gate = gate_ref[...]
    routed = jnp.zeros(base_ref.shape[1:], F32)
    for k in range(TOP_K):
        rows = jnp.concatenate([_unpack_rows([ys_ref[w, j, k] for j in range(planes)]) for w in range(nwin)], axis=0)
        routed = routed + gate[:, k:k + 1] * rows.astype(F32)
    y = base_ref[0] + gtf_ref[0] * routed
    o_ref[0] = _rms(y, gfin_ref[...])


def _final(base, ysg, gate_t, gt_f, g_final, tm, row0):
    B, L, D = base.shape
    planes = ysg.shape[1]
    wpt = tm // SC_WINDOW
    nt = L // tm
    off = row0 // tm
    per_row = gt_f.shape[1] != 1
    mod_spec = (pl.BlockSpec((1, tm, D), lambda b, i: (b, i, 0)) if per_row
                else pl.BlockSpec((1, 1, D), lambda b, i: (b, 0, 0)))
    return pl.pallas_call(
        _final_kernel, grid=(B, nt),
        in_specs=[pl.BlockSpec((1, tm, D), lambda b, i: (b, i, 0)),
                  pl.BlockSpec((wpt, planes, TOP_K, SC_WINDOW, LANES), lambda b, i: (off + b * nt + i, 0, 0, 0, 0)),
                  pl.BlockSpec((tm, TOP_K), lambda b, i: (off + b * nt + i, 0)),
                  mod_spec, _const_spec((1, D))],
        out_specs=pl.BlockSpec((1, tm, D), lambda b, i: (b, i, 0)),
        out_shape=jax.ShapeDtypeStruct((B, L, D), F32),
        compiler_params=_params(("arbitrary", "arbitrary")), name="final",
    )(base, ysg, gate_t, gt_f, g_final)


def _sc_mesh():
    return plsc.VectorSubcoreMesh(core_axis_name="c", subcore_axis_name="s")


def _sc_worker():
    return lax.axis_index("s") * SC_CORES + lax.axis_index("c")


def _sc_dispatch(srcs, idx, out_rows):
    planes = srcs[0].shape[0]
    wins = [s.shape[1] // SC_WINDOW for s in srcs]
    n_win = sum(wins)
    n_iter = -(-n_win // SC_WORKERS)
    flat = [s.reshape(planes * s.shape[1], LANES) for s in srcs]

    @functools.partial(
        pl.kernel, mesh=_sc_mesh(), out_type=jax.ShapeDtypeStruct((out_rows, LANES), srcs[0].dtype),
        scratch_types=[pltpu.VMEM((planes * TOP_K, SC_WINDOW), I32),
                       pltpu.VMEM((planes, SC_WINDOW, LANES), srcs[0].dtype),
                       pltpu.SemaphoreType.DMA((planes,)), pltpu.SemaphoreType.DMA],
        name="sc_dispatch")
    def k(*refs):
        src_hbms, (idx_hbm, out_hbm, idx_v, rows_v, load_sem, scat_sem) = refs[:len(srcs)], refs[len(srcs):]
        wid = _sc_worker()

        def window(src_hbm, n_tok, win, local_win):
            t0 = local_win * SC_WINDOW
            loads = [pltpu.make_async_copy(
                src_hbm.at[pl.ds(pl.multiple_of(j * n_tok + t0, SC_WINDOW), SC_WINDOW)],
                rows_v.at[j], load_sem.at[j]) for j in range(planes)]
            for ld in loads:
                ld.start()
            pltpu.sync_copy(idx_hbm.at[win], idx_v)
            scatters = []
            for j in range(planes):
                loads[j].wait()
                for kk in range(TOP_K):
                    cp = pltpu.make_async_copy(rows_v.at[j], out_hbm.at[idx_v.at[j * TOP_K + kk]], scat_sem)
                    cp.start()
                    scatters.append(cp)
            for cp in scatters:
                cp.wait()

        @pl.loop(0, n_iter)
        def _(i):
            win = i * SC_WORKERS + wid
            first = 0
            for src_hbm, src, nw in zip(src_hbms, srcs, wins):
                @pl.when(jnp.logical_and(win >= first, win < first + nw))
                def _(src_hbm=src_hbm, n_tok=src.shape[1], first=first):
                    window(src_hbm, n_tok, win, win - first)
                first += nw

    return k(*flat, idx)


def _sc_gather(table, idx):
    n_chunks, win = idx.shape
    n_iter = n_chunks // SC_WORKERS
    assert n_chunks % SC_WORKERS == 0 and n_iter % 2 == 0

    @functools.partial(
        pl.kernel, mesh=_sc_mesh(), out_type=jax.ShapeDtypeStruct((n_chunks * win, LANES), table.dtype),
        scratch_types=[pltpu.VMEM((n_iter, win), I32), pltpu.VMEM((2, win, LANES), table.dtype),
                       pltpu.SemaphoreType.DMA((2,)), pltpu.SemaphoreType.DMA((2,))],
        name="sc_gather")
    def k(table_hbm, idx_hbm, out_hbm, idx_v, rows_v, gather_sem, write_sem):
        wid = _sc_worker()
        c0 = wid * n_iter
        pltpu.sync_copy(idx_hbm.at[wid], idx_v)

        def gather(c, b):
            return pltpu.make_async_copy(table_hbm.at[idx_v.at[c]], rows_v.at[b], gather_sem.at[b])

        def write(c, b):
            return pltpu.make_async_copy(rows_v.at[b], out_hbm.at[pl.ds(pl.multiple_of((c0 + c) * win, win), win)],
                                         write_sem.at[b])

        gather(0, 0).start()

        @pl.loop(0, n_iter, step=2)
        def _(i):
            for b in range(2):
                c = i + b
                other = 1 - b

                @pl.when(c >= 1)
                def _():
                    write(c - 1, other).wait()

                @pl.when(c + 1 < n_iter)
                def _():
                    gather(c + 1, other).start()

                gather(c, b).wait()
                write(c, b).start()

        write(n_iter - 1, (n_iter - 1) % 2).wait()

    return k(table, idx.reshape(SC_WORKERS, n_iter, win))


def _moe_plan(eidx, rank, counts, planes):
    T = eidx.shape[1]
    counts = counts.reshape(N_EXPERTS).astype(I32)
    nblk = (counts + MOE_ROWS - 1) // MOE_ROWS
    blk_end = jnp.cumsum(nblk)
    blk_start = blk_end - nblk
    nb_max = (T * TOP_K) // MOE_ROWS + N_EXPERTS
    nused = blk_end[-1]
    bidx = jnp.arange(nb_max, dtype=I32)
    last = jnp.minimum(bidx, nused - 1)
    blk_e = jnp.minimum(jnp.sum((blk_end[None, :] <= last[:, None]).astype(I32), axis=1), N_EXPERTS - 1)
    mine = blk_e[:, None] == jnp.arange(N_EXPERTS, dtype=I32)[None, :]
    cnt_b = jnp.sum(jnp.where(mine, counts[None, :], 0), axis=1)
    start_b = jnp.sum(jnp.where(mine, blk_start[None, :], 0), axis=1)
    nvalid = jnp.clip(cnt_b - (bidx - start_b) * MOE_ROWS, 0, MOE_ROWS).astype(I32)
    first = jnp.logical_and(bidx < nused, jnp.concatenate([jnp.ones((1,), bool), blk_e[1:] != blk_e[:-1]]))
    has = counts > 0
    eids = jnp.arange(N_EXPERTS, dtype=I32)
    nxt_or_self = lax.cummin(jnp.where(has, eids, N_EXPERTS), axis=0, reverse=True)
    next_used = jnp.concatenate([nxt_or_self[1:], jnp.full((1,), N_EXPERTS, I32)])
    next_used = jnp.where(next_used < N_EXPERTS, next_used, -1).astype(I32)
    slot = ((jnp.cumsum(has.astype(I32)) - 1) % 2).astype(I32)
    rows_pad = nb_max * MOE_ROWS
    pstart = (blk_start * MOE_ROWS).astype(F32).reshape(N_EXPERTS, 1)
    idx = _pair_rows(eidx, rank, pstart, planes, rows_pad, 256)
    plan = (blk_e.astype(I32), nvalid, nused.reshape(1).astype(I32), first.astype(I32), next_used, slot)
    return idx, plan, rows_pad


def _moe_group(h2ps, eidx, rank, counts, w_gate, w_up, w_down):
    planes = h2ps[0].shape[0]
    T = eidx.shape[1]
    assert (T * TOP_K) % MOE_ROWS == 0 and T % 256 == 0
    idx, plan, rows_pad = _moe_plan(eidx, rank, counts, planes)
    n_win = T // SC_WINDOW
    xs = _sc_dispatch(h2ps, idx, planes * rows_pad)
    ys = _experts(xs.reshape(planes, rows_pad, LANES), plan, w_gate, w_up, w_down)
    ysg = _sc_gather(ys.reshape(planes * rows_pad, LANES), idx.reshape(n_win * planes * TOP_K, SC_WINDOW))
    return ysg.reshape(n_win, planes, TOP_K, SC_WINDOW, LANES)


def _rope_tables(pos):
    half = ROPE_DIM // 2
    inv = ROPE_THETA ** (-jnp.arange(half, dtype=F32) / half)
    ang = pos.astype(F32)[:, None] * inv[None, :]
    cos = jnp.tile(jnp.cos(ang), (1, 2 * MLA_HEADS))
    sin = jnp.tile(jnp.sin(ang), (1, 2 * MLA_HEADS))
    return cos, sin


def _rot_cols(w):
    half = ROPE_DIM // 2
    return jnp.concatenate([-w[..., half:], w[..., :half]], axis=-1)


def _layer_weights(w_in, g_mix, g_q, g_kv, w_uq, w_uk, w_uv, s5_D, s5_w_glu, s5_b_glu, g_out_attn, g_out_s5,
                   w_out, g_ffn, w_router, b_router, w_sh_gate, w_sh_up, w_sh_down):
    D = w_in.shape[0]
    o1, o2, o3 = Q_LORA, Q_LORA + KV_LORA, Q_LORA + KV_LORA + ROPE_DIM
    s5w = w_in.shape[1] - o3
    w_rope = w_in[:, o2:o3]
    w_in_ext = jnp.concatenate([w_in[:, :o2], w_in[:, o3:], w_rope, _rot_cols(w_rope)], axis=1).astype(BF16)
    wq = w_uq.reshape(Q_LORA, MLA_HEADS, NOPE_DIM + ROPE_DIM)
    wq_rope = wq[:, :, NOPE_DIM:]
    wr_hi, wr_lo = _split(w_router.T)
    return {
        "s5w": s5w,
        "g_mix": g_mix.reshape(1, D), "w_in": w_in_ext,
        "g_q": g_q.reshape(1, Q_LORA), "g_kv": g_kv.reshape(1, KV_LORA),
        "wq_nope": wq[:, :, :NOPE_DIM].reshape(Q_LORA, MLA_HEADS * NOPE_DIM).astype(BF16),
        "wq_rope": wq_rope.reshape(Q_LORA, MLA_HEADS * ROPE_DIM).astype(BF16),
        "wq_rot": _rot_cols(wq_rope).reshape(Q_LORA, MLA_HEADS * ROPE_DIM).astype(BF16),
        "w_uk": w_uk.reshape(KV_LORA, MLA_HEADS * NOPE_DIM).astype(BF16),
        "w_uv": jnp.transpose(w_uv, (1, 0, 2)).astype(BF16),
        "s5_d": s5_D.reshape(1, s5w), "w_glu": s5_w_glu.astype(BF16), "b_glu": s5_b_glu.reshape(1, s5w),
        "g_out_attn": g_out_attn.reshape(1, -1), "g_out_s5": g_out_s5.reshape(1, s5w),
        "w_out": w_out.astype(BF16), "g_ffn": g_ffn.reshape(1, D),
        "wr_hi": wr_hi, "wr_lo": wr_lo, "b_router": b_router.reshape(N_EXPERTS, 1),
        "w_sg": w_sh_gate.astype(BF16), "w_su": w_sh_up.astype(BF16), "w_sd": w_sh_down.astype(BF16),
    }


def _state_in(s_re, s_im):
    B = s_re.shape[0]
    return jnp.stack([s_re.reshape(B, -1), s_im.reshape(B, -1)], axis=1).astype(F32)


def kernel(x_prompt, x_sample, c_prompt, c_sample, cache_ckv, cache_krope, state_s5_re, state_s5_im, w_ada, b_ada, g_mix, g_ffn, w_in, g_q, w_uq, g_kv, w_uk, w_uv, s5_A_re, s5_A_im, s5_B_re, s5_B_im, s5_C_re, s5_C_im, s5_D, s5_log_dt, s5_w_glu, s5_b_glu, g_out_attn, g_out_s5, w_out, w_router, b_router, w_exp_gate, w_exp_up, w_exp_down, w_sh_gate, w_sh_up, w_sh_down, g_final):
    Bp, Lp, D = x_prompt.shape
    Bs, Ls, _ = x_sample.shape
    depth = w_ada.shape[0]
    assert depth == 1, "single-layer step"
    past = cache_ckv.shape[2]
    G, N = s5_A_re.shape[1:]
    Ts = Bs * Ls
    Tp = Bp * Lp
    l = 0

    wts = _layer_weights(w_in[l], g_mix[l], g_q[l], g_kv[l], w_uq[l], w_uk[l], w_uv[l], s5_D[l], s5_w_glu[l],
                         s5_b_glu[l], g_out_attn[l], g_out_s5[l], w_out[l], g_ffn[l], w_router[l], b_router[l],
                         w_sh_gate[l], w_sh_up[l], w_sh_down[l])

    mod = _ada(jnp.concatenate([c_prompt, c_sample], axis=0), w_ada[l], b_ada[l])
    mod_p = [m.reshape(Bp, 1, D) for m in jnp.split(mod[:Bp], 6, axis=-1)]
    mod_s = [jnp.broadcast_to(m[:, None, :], (Bs, Ls, D)).reshape(1, Ts, D)
             for m in jnp.split(mod[Bp:], 6, axis=-1)]

    tm_p = 256
    tm_pre = 1024
    tm_post = 1024
    tm_s5 = 512
    cos_p, sin_p = _rope_tables(jnp.arange(Lp))
    cos_s, sin_s = _rope_tables(jnp.tile(past + jnp.arange(Ls), Bs))

    ckv_p, kr_p, u_p, kcat_p, v_p, q_p = _pre(x_prompt, mod_p[0], mod_p[1], cos_p, sin_p, wts, tm_pre)
    olat_p = _attn_prompt(q_p, kcat_p, v_p, 512)
    tabs_p = _s5_tables(s5_A_re[l], s5_A_im[l], s5_B_re[l], s5_B_im[l], s5_C_re[l], s5_C_im[l], s5_log_dt[l], tm_s5)
    os5_p, hl_p = _s5(u_p, jnp.zeros((Bp, 2, G * N), F32), tabs_p, wts, tm_s5, precise=False)

    xs_rows = x_sample.reshape(1, Ts, D)
    ckv_s, kr_s, u_s, kcat_s, v_s, q_s = _pre(xs_rows, mod_s[0], mod_s[1], cos_s, sin_s, wts, Ts)
    olat_s = _attn_sample(q_s, kcat_s, v_s, cache_ckv[l], cache_krope[l], wts["w_uk"])
    tabs_s = _s5_tables(s5_A_re[l], s5_A_im[l], s5_B_re[l], s5_B_im[l], s5_C_re[l], s5_C_im[l], s5_log_dt[l], Ls)
    os5_s, hl_s = _s5(u_s.reshape(Bs, Ls, -1), _state_in(state_s5_re[l], state_s5_im[l]), tabs_s, wts, Ls,
                      precise=True)
    h2_p, base_p, eidx_p, gate_p, rank_p, counts_p = _post(
        x_prompt, olat_p, os5_p, [mod_p[2], mod_p[3], mod_p[4], mod_p[5]], wts, tm_post,
        jnp.zeros((N_EXPERTS, 1), F32))
    h2_s, base_s, eidx_s, gate_s, rank_s, counts = _post(
        xs_rows, olat_s, os5_s.reshape(1, Ts, -1), [mod_s[2], mod_s[3], mod_s[4], mod_s[5]], wts, Ts, counts_p)
    ysg = _moe_group([h2_p, h2_s], jnp.concatenate([eidx_p, eidx_s], axis=1),
                     jnp.concatenate([rank_p, rank_s], axis=1), counts, w_exp_gate[l], w_exp_up[l], w_exp_down[l])
    gate_t = jnp.concatenate([gate_p, gate_s], axis=1).T

    gfin = g_final.reshape(1, D)
    y_p = _final(base_p, ysg, gate_t, mod_p[5], gfin, tm_p, 0)
    y_s = _final(base_s, ysg, gate_t, mod_s[5], gfin, Ts, Tp).reshape(Bs, Ls, D)

    def state_out(hl, B):
        return hl[:, 0].reshape(1, B, G, N), hl[:, 1].reshape(1, B, G, N)

    sre_p, sim_p = state_out(hl_p, Bp)
    sre_s, sim_s = state_out(hl_s, Bs)
    return (y_p, y_s, ckv_p[None], kr_p[None], sre_p, sim_p,
            ckv_s.reshape(1, Bs, Ls, KV_LORA), kr_s.reshape(1, Bs, Ls, ROPE_DIM), sre_s, sim_s)
```

```python
import functools
import math

import numpy as np
import jax
import jax.numpy as jnp
from jax import lax
from jax.experimental import pallas as pl
from jax.experimental.pallas import tpu as pltpu
from jax.experimental.pallas import tpu_sc as plsc

F32 = jnp.float32
BF16 = jnp.bfloat16
I32 = jnp.int32
U32 = jnp.uint32

EPS = 1e-6
CHUNK = 64
MLA_HEADS = 4
NOPE_DIM = 128
ROPE_DIM = 64
V_DIM = 128
Q_LORA = 256
KV_LORA = 256
QK_HEAD = NOPE_DIM + ROPE_DIM
ROPE_THETA = 10000.0
S5_GROUP_CH = 16
S5_STATE = 64
N_EXPERTS = 256
TOP_K = 8
N_EXPERT_GROUPS = 8
TOPK_GROUPS = 4
ROUTED_SCALE = 2.5

S5_LT_ROWS = 256
S5_SUB = 16
MOE_ROWS = 512
LANES = 128
VMEM_LIMIT = 56 * 1024 * 1024
SC_CORES = 2
SC_SUBCORES = 16
SC_WORKERS = SC_CORES * SC_SUBCORES
SC_WINDOW = 128

NEG_INF = float("-inf")
BIG_I32 = 1 << 30


def _dot(a, b):
    return jnp.dot(a, b, preferred_element_type=F32)


def _dot_nt(a, b):
    return lax.dot_general(a, b, (((1,), (1,)), ((), ())), preferred_element_type=F32)


def _split(a):
    hi = a.astype(BF16)
    lo = (a - hi.astype(F32)).astype(BF16)
    return hi, lo


def _lane_tile(x, n):
    return jnp.concatenate([x] * n, axis=1)


def _rms(x, g):
    return x * lax.rsqrt(jnp.mean(x * x, axis=-1, keepdims=True) + EPS) * g


def _silu(x):
    return x * jax.nn.sigmoid(x)


def _pack_rows(x):
    half = x.shape[1] // 2
    hi = lax.bitcast_convert_type(x[:, :half].astype(BF16).astype(F32), U32)
    lo = lax.bitcast_convert_type(x[:, half:].astype(BF16).astype(F32), U32)
    w = hi | (lo >> 16)
    return [w[:, j * LANES:(j + 1) * LANES] for j in range(half // LANES)]


def _unpack_rows(planes):
    his = [lax.bitcast_convert_type(p & jnp.uint32(0xFFFF0000), F32).astype(BF16) for p in planes]
    los = [lax.bitcast_convert_type(p << 16, F32).astype(BF16) for p in planes]
    return jnp.concatenate(his + los, axis=1)


def _params(sem):
    return pltpu.CompilerParams(dimension_semantics=sem, vmem_limit_bytes=VMEM_LIMIT)


def _const_spec(shape):
    nd = len(shape)
    return pl.BlockSpec(shape, lambda *_: (0,) * nd)


def _ada_kernel(c_ref, whi_ref, wlo_ref, b_ref, o_ref):
    c = c_ref[...]
    s_hi, s_lo = _split(_silu(c))
    w_hi = whi_ref[...]
    o_ref[...] = _dot(s_hi, w_hi) + _dot(s_hi, wlo_ref[...]) + _dot(s_lo, w_hi) + b_ref[...]


def _ada(c, w_ada, b_ada):
    rows, d = c.shape
    n = w_ada.shape[1]
    tn = 512
    w_hi, w_lo = _split(w_ada)
    return pl.pallas_call(
        _ada_kernel,
        grid=(n // tn,),
        in_specs=[_const_spec((rows, d)),
                  pl.BlockSpec((d, tn), lambda j: (0, j)),
                  pl.BlockSpec((d, tn), lambda j: (0, j)),
                  pl.BlockSpec((1, tn), lambda j: (0, j))],
        out_specs=pl.BlockSpec((rows, tn), lambda j: (0, j)),
        out_shape=jax.ShapeDtypeStruct((rows, n), F32),
        compiler_params=_params(("arbitrary",)),
        name="ada",
    )(c, w_hi, w_lo, b_ada.reshape(1, n))


def _pre_kernel(x_ref, sh_ref, sc_ref, g_ref, win_ref, gq_ref, gkv_ref, wqn_ref, wqr_ref, wqt_ref,
                wuk_ref, cos_ref, sin_ref, ckv_ref, kr_ref, u_ref, kcat_ref, v_ref, q_ref, *, scale):
    x = x_ref[0]
    h = _rms(x, g_ref[...]) * (1.0 + sc_ref[0]) + sh_ref[0]
    z = _dot(h.astype(BF16), win_ref[...])
    cq = _rms(z[:, :Q_LORA], gq_ref[...])
    ckv = _rms(z[:, Q_LORA:Q_LORA + KV_LORA], gkv_ref[...])
    o_s5 = Q_LORA + KV_LORA
    s5w = u_ref.shape[-1]
    u_ref[0] = z[:, o_s5:o_s5 + s5w]
    o_r = o_s5 + s5w
    cos = cos_ref[...]
    sin = sin_ref[...]
    kr = z[:, o_r:o_r + ROPE_DIM] * cos[:, :ROPE_DIM] + z[:, o_r + ROPE_DIM:o_r + 2 * ROPE_DIM] * sin[:, :ROPE_DIM]
    ckv_ref[0] = ckv
    kr_ref[0] = kr
    ckvb = ckv.astype(BF16)
    krb = kr.astype(BF16)
    v_ref[0] = ckvb
    kn = _dot(ckvb, wuk_ref[...])
    cqb = cq.astype(BF16)
    qn = _dot(cqb, wqn_ref[...]) * scale
    qr = (_dot(cqb, wqr_ref[...]) * cos + _dot(cqb, wqt_ref[...]) * sin) * scale
    for hd in range(MLA_HEADS):
        kcat_ref[0, hd, :, :NOPE_DIM] = kn[:, hd * NOPE_DIM:(hd + 1) * NOPE_DIM].astype(BF16)
        kcat_ref[0, hd, :, NOPE_DIM:] = krb
        q_ref[0, hd, :, :NOPE_DIM] = qn[:, hd * NOPE_DIM:(hd + 1) * NOPE_DIM].astype(BF16)
        q_ref[0, hd, :, NOPE_DIM:] = qr[:, hd * ROPE_DIM:(hd + 1) * ROPE_DIM].astype(BF16)


def _pre(x, shift, scale_mod, cos_t, sin_t, wts, tm):
    B, L, D = x.shape
    nt = L // tm
    per_row = shift.shape[1] != 1
    mod_spec = (pl.BlockSpec((1, tm, D), lambda b, i: (b, i, 0)) if per_row
                else pl.BlockSpec((1, 1, D), lambda b, i: (b, 0, 0)))
    s5w = wts["s5w"]
    hr = MLA_HEADS * ROPE_DIM
    kern = functools.partial(_pre_kernel, scale=QK_HEAD ** -0.5 * math.log2(math.e))
    consts = [wts["g_mix"], wts["w_in"], wts["g_q"], wts["g_kv"], wts["wq_nope"], wts["wq_rope"],
              wts["wq_rot"], wts["w_uk"]]
    in_specs = [pl.BlockSpec((1, tm, D), lambda b, i: (b, i, 0)), mod_spec, mod_spec]
    in_specs += [_const_spec(c.shape) for c in consts]
    in_specs += [pl.BlockSpec((tm, hr), lambda b, i: (i, 0)), pl.BlockSpec((tm, hr), lambda b, i: (i, 0))]
    out_shape = (jax.ShapeDtypeStruct((B, L, KV_LORA), F32),
                 jax.ShapeDtypeStruct((B, L, ROPE_DIM), F32),
                 jax.ShapeDtypeStruct((B, L, s5w), F32),
                 jax.ShapeDtypeStruct((B, MLA_HEADS, L, QK_HEAD), BF16),
                 jax.ShapeDtypeStruct((B, L, KV_LORA), BF16),
                 jax.ShapeDtypeStruct((B, MLA_HEADS, L, QK_HEAD), BF16))
    out_specs = (pl.BlockSpec((1, tm, KV_LORA), lambda b, i: (b, i, 0)),
                 pl.BlockSpec((1, tm, ROPE_DIM), lambda b, i: (b, i, 0)),
                 pl.BlockSpec((1, tm, s5w), lambda b, i: (b, i, 0)),
                 pl.BlockSpec((1, MLA_HEADS, tm, QK_HEAD), lambda b, i: (b, 0, i, 0)),
                 pl.BlockSpec((1, tm, KV_LORA), lambda b, i: (b, i, 0)),
                 pl.BlockSpec((1, MLA_HEADS, tm, QK_HEAD), lambda b, i: (b, 0, i, 0)))
    return pl.pallas_call(
        kern, grid=(B, nt), in_specs=in_specs, out_specs=out_specs, out_shape=out_shape,
        compiler_params=_params(("arbitrary", "arbitrary")), name="pre",
    )(x, shift, scale_mod, *consts, cos_t, sin_t)


def _attn_kernel(q_ref, k_ref, v_ref, o_ref, m_scr, l_scr, acc_scr, *, t):
    i = pl.program_id(1)
    m_scr[...] = jnp.full(m_scr.shape, NEG_INF, F32)
    l_scr[...] = jnp.zeros(l_scr.shape, F32)
    acc_scr[...] = jnp.zeros(acc_scr.shape, F32)
    visible = (lax.broadcasted_iota(I32, (t, t), 1) // CHUNK) <= (lax.broadcasted_iota(I32, (t, t), 0) // CHUNK)

    def step(j0, masked):
        v = v_ref[0, pl.ds(j0, t), :]
        for hd in range(MLA_HEADS):
            rs = slice(hd * t, (hd + 1) * t)
            s = _dot_nt(q_ref[0, hd], k_ref[0, hd, pl.ds(j0, t), :])
            if masked:
                s = jnp.where(visible, s, NEG_INF)
            m_prev = m_scr[rs]
            m_next = jnp.maximum(m_prev, jnp.max(s, axis=1, keepdims=True))
            alpha = jnp.exp2(m_prev - m_next)
            p = jnp.exp2(s - _lane_tile(m_next, t // LANES))
            l_scr[rs] = alpha * l_scr[rs] + jnp.sum(p, axis=1, keepdims=True)
            m_scr[rs] = m_next
            acc_scr[rs] = acc_scr[rs] * _lane_tile(alpha, KV_LORA // LANES) + _dot(p.astype(BF16), v)

    def body(j, carry):
        step(pl.multiple_of(j * t, t), False)
        return carry

    lax.fori_loop(0, i, body, 0)
    step(pl.multiple_of(i * t, t), True)
    for hd in range(MLA_HEADS):
        rs = slice(hd * t, (hd + 1) * t)
        inv = 1.0 / l_scr[rs]
        o_ref[0, hd] = (acc_scr[rs] * _lane_tile(inv, KV_LORA // LANES)).astype(BF16)


def _attn_prompt(q, kcat, v, t):
    B, H, L, _ = q.shape
    assert L % t == 0 and t % CHUNK == 0
    rows = H * t
    kern = functools.partial(_attn_kernel, t=t)
    resident = pl.Buffered(1)
    return pl.pallas_call(
        kern, grid=(B, L // t),
        in_specs=[pl.BlockSpec((1, H, t, QK_HEAD), lambda b, i: (b, 0, i, 0)),
                  pl.BlockSpec((1, H, L, QK_HEAD), lambda b, i: (b, 0, 0, 0), pipeline_mode=resident),
                  pl.BlockSpec((1, L, KV_LORA), lambda b, i: (b, 0, 0), pipeline_mode=resident)],
        out_specs=pl.BlockSpec((1, H, t, KV_LORA), lambda b, i: (b, 0, i, 0)),
        out_shape=jax.ShapeDtypeStruct((B, H, L, KV_LORA), BF16),
        scratch_shapes=[pltpu.VMEM((rows, LANES), F32), pltpu.VMEM((rows, LANES), F32),
                        pltpu.VMEM((rows, KV_LORA), F32)],
        compiler_params=_params(("arbitrary", "arbitrary")), name="attn_prompt",
    )(q, kcat, v)


def _attn_sample_kernel(q_ref, kn_ref, vn_ref, pc_ref, pr_ref, wuk_ref, o_ref, *, past, lq):
    pc = pc_ref[0].astype(BF16)
    pr = pr_ref[0].astype(BF16)
    kp = _dot(pc, wuk_ref[...]).astype(BF16)
    vn = vn_ref[0]
    qchunk_p = (past + lax.broadcasted_iota(I32, (lq, past), 0)) // CHUNK
    vis_p = lax.broadcasted_iota(I32, (lq, past), 1) // CHUNK <= qchunk_p
    qchunk_n = (past + lax.broadcasted_iota(I32, (lq, lq), 0)) // CHUNK
    vis_n = (past + lax.broadcasted_iota(I32, (lq, lq), 1)) // CHUNK <= qchunk_n
    for hd in range(MLA_HEADS):
        q = q_ref[0, hd]
        s_p = (_dot_nt(q[:, :NOPE_DIM], kp[:, hd * NOPE_DIM:(hd + 1) * NOPE_DIM])
               + _dot_nt(q[:, NOPE_DIM:], pr))
        s_n = _dot_nt(q, kn_ref[0, hd])
        s_p = jnp.where(vis_p, s_p, NEG_INF)
        s_n = jnp.where(vis_n, s_n, NEG_INF)
        m = jnp.maximum(jnp.max(s_p, axis=1, keepdims=True), jnp.max(s_n, axis=1, keepdims=True))
        p_p = jnp.exp2(s_p - m)
        p_n = jnp.exp2(s_n - m)
        l = jnp.sum(p_p, axis=1, keepdims=True) + jnp.sum(p_n, axis=1, keepdims=True)
        o = _dot(p_p.astype(BF16), pc) + _dot(p_n.astype(BF16), vn)
        o_ref[0, hd] = (o / l).astype(BF16)


def _attn_sample(q, kcat, v, past_ckv, past_kr, w_uk):
    B, past, _ = past_ckv.shape
    H = MLA_HEADS
    lq = q.shape[2] // B
    kern = functools.partial(_attn_sample_kernel, past=past, lq=lq)
    return pl.pallas_call(
        kern, grid=(B,),
        in_specs=[pl.BlockSpec((1, H, lq, QK_HEAD), lambda b: (0, 0, b, 0)),
                  pl.BlockSpec((1, H, lq, QK_HEAD), lambda b: (0, 0, b, 0)),
                  pl.BlockSpec((1, lq, KV_LORA), lambda b: (0, b, 0)),
                  pl.BlockSpec((1, past, KV_LORA), lambda b: (b, 0, 0)),
                  pl.BlockSpec((1, past, ROPE_DIM), lambda b: (b, 0, 0)),
                  _const_spec(w_uk.shape)],
        out_specs=pl.BlockSpec((1, H, lq, KV_LORA), lambda b: (0, 0, b, 0)),
        out_shape=jax.ShapeDtypeStruct((1, H, B * lq, KV_LORA), BF16),
        compiler_params=_params(("arbitrary",)), name="attn_sample",
    )(q, kcat, v, past_ckv, past_kr, w_uk)


def _s5_kernel(u_ref, h0_ref, bre_ref, bim_ref, brel_ref, biml_ref, lt_ref, pinr_ref, pini_ref,
               pwr_ref, pwi_ref, a_ref, cre_ref, cim_ref, d_ref, wglu_ref, bglu_ref, gout_ref,
               o_ref, hl_ref, st_scr, cum_scr, hs_scr, *, tm, precise):
    i = pl.program_id(1)
    ns = st_scr.shape[1]
    half = ns // 2
    wh = u_ref.shape[-1] // 2

    @pl.when(i == 0)
    def _():
        st_scr[...] = h0_ref[0]

    u = u_ref[0]
    lt = lt_ref[...]
    ys = []
    for hf in range(2):
        uh = u[:, hf * wh:(hf + 1) * wh]
        if precise:
            u_hi, u_lo = _split(uh)
            bu_re = _dot(u_hi, bre_ref[hf]) + _dot(u_lo, bre_ref[hf]) + _dot(u_hi, brel_ref[hf])
            bu_im = _dot(u_hi, bim_ref[hf]) + _dot(u_lo, bim_ref[hf]) + _dot(u_hi, biml_ref[hf])
        else:
            u_hi = uh.astype(BF16)
            bu_re = _dot(u_hi, bre_ref[hf])
            bu_im = _dot(u_hi, bim_ref[hf])
        sl = slice(hf * half, (hf + 1) * half)
        pr = pinr_ref[:, sl][None]
        pi = pini_ref[:, sl][None]
        bu_re = bu_re.reshape(tm // S5_SUB, S5_SUB, half)
        bu_im = bu_im.reshape(tm // S5_SUB, S5_SUB, half)
        v_re = (pr * bu_re - pi * bu_im).reshape(tm, half)
        v_im = (pr * bu_im + pi * bu_re).reshape(tm, half)
        ltr = lt.shape[0]
        for part, v in ((0, v_re), (1, v_im)):
            for r0 in range(0, tm, ltr):
                vb = v[r0:r0 + ltr]
                if precise:
                    v_hi, v_lo = _split(vb)
                    c = _dot(lt, v_hi) + _dot(lt, v_lo)
                else:
                    c = _dot(lt, vb.astype(BF16))
                cum_scr[r0:r0 + ltr, part * ns + hf * half:part * ns + (hf + 1) * half] = c

        re_cols = slice(hf * half, (hf + 1) * half)
        im_cols = slice(ns + hf * half, ns + (hf + 1) * half)
        a_re = a_ref[0:1, sl]
        a_im = a_ref[1:2, sl]
        pw_re = pwr_ref[:, sl]
        pw_im = pwi_ref[:, sl]
        s_re = st_scr[0:1, sl]
        s_im = st_scr[1:2, sl]
        for c in range(tm // S5_SUB):
            rows = slice(c * S5_SUB, (c + 1) * S5_SUB)
            t_re = cum_scr[rows, re_cols] + (a_re * s_re - a_im * s_im)
            t_im = cum_scr[rows, im_cols] + (a_re * s_im + a_im * s_re)
            h_re = pw_re * t_re - pw_im * t_im
            h_im = pw_re * t_im + pw_im * t_re
            hs_scr[rows, re_cols] = h_re.astype(BF16)
            hs_scr[rows, im_cols] = h_im.astype(BF16)
            s_re = h_re[S5_SUB - 1:S5_SUB, :]
            s_im = h_im[S5_SUB - 1:S5_SUB, :]
        st_scr[0:1, sl] = s_re
        st_scr[1:2, sl] = s_im
        hl_ref[0, 0:1, sl] = s_re
        hl_ref[0, 1:2, sl] = s_im
        ys.append(_dot(hs_scr[:, re_cols], cre_ref[hf]) + _dot(hs_scr[:, im_cols], cim_ref[hf]))
    y = jnp.concatenate(ys, axis=1) + d_ref[...] * u
    zg = jax.nn.gelu(y)
    gl = _dot(zg.astype(BF16), wglu_ref[...]) + bglu_ref[...]
    o = zg * jax.nn.sigmoid(gl)
    o_ref[0] = _rms(o, gout_ref[...]).astype(BF16)


def _s5(u, h0, tabs, wts, tm, precise):
    B, L, W = u.shape
    ns = h0.shape[-1]
    consts = [tabs["b_re"], tabs["b_im"], tabs["b_re_lo"], tabs["b_im_lo"], tabs["lt"], tabs["pin_re"],
              tabs["pin_im"], tabs["pw_re"], tabs["pw_im"], tabs["a"], tabs["c_re"], tabs["c_im"],
              wts["s5_d"], wts["w_glu"], wts["b_glu"], wts["g_out_s5"]]
    kern = functools.partial(_s5_kernel, tm=tm, precise=precise)
    return pl.pallas_call(
        kern, grid=(B, L // tm),
        in_specs=[pl.BlockSpec((1, tm, W), lambda b, i: (b, i, 0)),
                  pl.BlockSpec((1, 2, ns), lambda b, i: (b, 0, 0))] + [_const_spec(c.shape) for c in consts],
        out_specs=(pl.BlockSpec((1, tm, W), lambda b, i: (b, i, 0)),
                   pl.BlockSpec((1, 2, ns), lambda b, i: (b, 0, 0))),
        out_shape=(jax.ShapeDtypeStruct((B, L, W), BF16), jax.ShapeDtypeStruct((B, 2, ns), F32)),
        scratch_shapes=[pltpu.VMEM((2, ns), F32), pltpu.VMEM((tm, 2 * ns), F32), pltpu.VMEM((tm, 2 * ns), BF16)],
        compiler_params=_params(("arbitrary", "arbitrary")), name="s5",
    )(u, h0, *consts)


def _s5_tables(a_re_p, a_im_p, b_re_p, b_im_p, c_re_p, c_im_p, log_dt, tm):
    G, N = a_re_p.shape
    CH = b_re_p.shape[-1]
    dt = jnp.exp(log_dt.astype(F32))[:, None]
    lr = a_re_p.astype(F32) * dt
    li = a_im_p.astype(F32) * dt
    er = jnp.exp(lr)
    ab_re, ab_im = er * jnp.cos(li), er * jnp.sin(li)
    lam2 = a_re_p.astype(F32) ** 2 + a_im_p.astype(F32) ** 2
    nr, ni = ab_re - 1.0, ab_im
    f_re = (nr * a_re_p + ni * a_im_p) / lam2
    f_im = (ni * a_re_p - nr * a_im_p) / lam2
    bb_re = f_re[..., None] * b_re_p - f_im[..., None] * b_im_p
    bb_im = f_re[..., None] * b_im_p + f_im[..., None] * b_re_p
    gh = G // 2
    eye = jnp.eye(gh, dtype=F32)

    def blk_b(bb):
        t = bb.reshape(2, gh, N, CH)
        m = jnp.einsum("hgnc,gk->hgckn", t, eye)
        return m.reshape(2, gh * CH, gh * N)

    def blk_c(cc):
        t = cc.reshape(2, gh, CH, N)
        m = jnp.einsum("hgcn,gk->hgnkc", t, eye)
        return m.reshape(2, gh * N, gh * CH)

    b_re_m, b_im_m = blk_b(bb_re), blk_b(bb_im)
    b_re_hi, b_re_lo = _split(b_re_m)
    b_im_hi, b_im_lo = _split(b_im_m)
    s = jnp.arange(S5_SUB, dtype=F32)[:, None, None]

    def powers(sign):
        e = jnp.exp(sign * lr[None] * s)
        return ((e * jnp.cos(sign * li[None] * s)).reshape(S5_SUB, G * N),
                (e * jnp.sin(sign * li[None] * s)).reshape(S5_SUB, G * N))

    pin_re, pin_im = powers(-1.0)
    pw_re, pw_im = powers(1.0)
    r = np.arange(min(tm, S5_LT_ROWS))
    lt = ((r[:, None] // S5_SUB == r[None, :] // S5_SUB) & (r[None, :] <= r[:, None])).astype(np.float32)
    return {
        "b_re": b_re_hi, "b_im": b_im_hi, "b_re_lo": b_re_lo, "b_im_lo": b_im_lo,
        "lt": jnp.asarray(lt, BF16),
        "pin_re": pin_re, "pin_im": pin_im,
        "pw_re": pw_re, "pw_im": pw_im,
        "a": jnp.stack([ab_re.reshape(G * N), ab_im.reshape(G * N)]),
        "c_re": blk_c(c_re_p.astype(F32)).astype(BF16), "c_im": (-blk_c(c_im_p.astype(F32))).astype(BF16),
    }


def _route(scores, sel):
    E, tm = scores.shape
    ge = E // N_EXPERT_GROUPS
    io_g = lax.broadcasted_iota(I32, (ge, tm), 0)
    gs_rows = []
    for g in range(N_EXPERT_GROUPS):
        sg = sel[g * ge:(g + 1) * ge, :]
        m1 = jnp.max(sg, axis=0, keepdims=True)
        i1 = jnp.min(jnp.where(sg == m1, io_g, BIG_I32), axis=0, keepdims=True)
        m2 = jnp.max(jnp.where(io_g == i1, NEG_INF, sg), axis=0, keepdims=True)
        gs_rows.append(m1 + m2)
    gs = jnp.concatenate(gs_rows, axis=0)
    gio = lax.broadcasted_iota(I32, gs.shape, 0)
    gsel = jnp.zeros(gs.shape, F32)
    for _ in range(TOPK_GROUPS):
        mx = jnp.max(gs, axis=0, keepdims=True)
        ix = jnp.min(jnp.where(gs == mx, gio, BIG_I32), axis=0, keepdims=True)
        hit = gio == ix
        gsel = jnp.where(hit, 1.0, gsel)
        gs = jnp.where(hit, NEG_INF, gs)
    emask = jnp.concatenate([jnp.broadcast_to(gsel[g:g + 1, :], (ge, tm)) for g in range(N_EXPERT_GROUPS)], axis=0)
    cand = jnp.where(emask > 0.0, sel, NEG_INF)
    eio = lax.broadcasted_iota(I32, (E, tm), 0)
    idxs, gates = [], []
    for _ in range(TOP_K):
        mx = jnp.max(cand, axis=0, keepdims=True)
        ix = jnp.min(jnp.where(cand == mx, eio, BIG_I32), axis=0, keepdims=True)
        hit = eio == ix
        gates.append(jnp.sum(jnp.where(hit, scores, 0.0), axis=0, keepdims=True))
        idxs.append(ix)
        cand = jnp.where(hit, NEG_INF, cand)
    eidx = jnp.concatenate(idxs, axis=0)
    gate = jnp.concatenate(gates, axis=0)
    gate = gate / jnp.sum(gate, axis=0, keepdims=True) * ROUTED_SCALE
    return eidx, gate


def _post_kernel(x_ref, ol_ref, os_ref, gtm_ref, shf_ref, scf_ref, gtf_ref, wuv_ref, goa_ref, wout_ref,
                 gffn_ref, wrh_ref, wrl_ref, br_ref, wsg_ref, wsu_ref, wsd_ref, tri_ref, cnt0_ref,
                 h2_ref, base_ref, eidx_ref, gate_ref, rank_ref, cnt_ref, cnt_scr):
    @pl.when(jnp.logical_and(pl.program_id(0) == 0, pl.program_id(1) == 0))
    def _():
        cnt_scr[...] = cnt0_ref[...]

    x = x_ref[0]
    oa = jnp.concatenate([_dot(ol_ref[0, hd], wuv_ref[hd]) for hd in range(MLA_HEADS)], axis=1)
    oan = _rms(oa, goa_ref[...]).astype(BF16)
    wa = oan.shape[1]
    mix = _dot(oan, wout_ref[:wa, :]) + _dot(os_ref[0], wout_ref[wa:, :])
    x1 = x + gtm_ref[0] * mix
    h2 = _rms(x1, gffn_ref[...]) * (1.0 + scf_ref[0]) + shf_ref[0]
    h2_hi, h2_lo = _split(h2)
    for j, plane in enumerate(_pack_rows(h2)):
        h2_ref[j] = plane
    sh = _dot((_silu(_dot(h2_hi, wsg_ref[...])) * _dot(h2_hi, wsu_ref[...])).astype(BF16), wsd_ref[...])
    base_ref[0] = x1 + gtf_ref[0] * sh
    wr_hi = wrh_ref[...]
    logits = _dot_nt(wr_hi, h2_hi) + _dot_nt(wrl_ref[...], h2_hi) + _dot_nt(wr_hi, h2_lo)
    scores = jax.nn.sigmoid(logits)
    eidx, gate = _route(scores, scores + br_ref[...])
    eidx_ref[...] = eidx
    gate_ref[...] = gate

    eio = lax.broadcasted_iota(I32, scores.shape, 0)
    onehot = jnp.zeros(scores.shape, F32)
    for k in range(TOP_K):
        onehot = jnp.where(eio == eidx[k:k + 1, :], 1.0, onehot)
    before = _dot(onehot.astype(BF16), tri_ref[...]) + cnt_scr[...]
    ranks = [jnp.sum(jnp.where(eio == eidx[k:k + 1, :], before, 0.0), axis=0, keepdims=True) for k in range(TOP_K)]
    rank_ref[...] = jnp.concatenate(ranks, axis=0).astype(I32)
    counts = cnt_scr[...] + jnp.sum(onehot, axis=1, keepdims=True)
    cnt_scr[...] = counts
    cnt_ref[...] = counts


def _post(x, o_lat, o_s5, mods, wts, tm, counts0):
    nb, L, D = x.shape
    nt = L // tm
    per_row = mods[0].shape[1] != 1
    mod_spec = (pl.BlockSpec((1, tm, D), lambda b, i: (b, i, 0)) if per_row
                else pl.BlockSpec((1, 1, D), lambda b, i: (b, 0, 0)))
    W = o_s5.shape[-1]
    r = np.arange(tm)
    tri = jnp.asarray((r[:, None] < r[None, :]).astype(np.float32), BF16)
    consts = [wts["w_uv"], wts["g_out_attn"], wts["w_out"], wts["g_ffn"], wts["wr_hi"], wts["wr_lo"],
              wts["b_router"], wts["w_sg"], wts["w_su"], wts["w_sd"], tri, counts0]
    in_specs = [pl.BlockSpec((1, tm, D), lambda b, i: (b, i, 0)),
                pl.BlockSpec((1, MLA_HEADS, tm, KV_LORA), lambda b, i: (b, 0, i, 0)),
                pl.BlockSpec((1, tm, W), lambda b, i: (b, i, 0)),
                mod_spec, mod_spec, mod_spec, mod_spec] + [_const_spec(c.shape) for c in consts]
    planes = D // (2 * LANES)
    out_shape = (jax.ShapeDtypeStruct((planes, nb * L, LANES), U32), jax.ShapeDtypeStruct((nb, L, D), F32),
                 jax.ShapeDtypeStruct((TOP_K, nb * L), I32), jax.ShapeDtypeStruct((TOP_K, nb * L), F32),
                 jax.ShapeDtypeStruct((TOP_K, nb * L), I32), jax.ShapeDtypeStruct((N_EXPERTS, 1), F32))
    out_specs = (pl.BlockSpec((planes, tm, LANES), lambda b, i: (0, b * nt + i, 0)),
                 pl.BlockSpec((1, tm, D), lambda b, i: (b, i, 0)),
                 pl.BlockSpec((TOP_K, tm), lambda b, i: (0, b * nt + i)),
                 pl.BlockSpec((TOP_K, tm), lambda b, i: (0, b * nt + i)),
                 pl.BlockSpec((TOP_K, tm), lambda b, i: (0, b * nt + i)),
                 _const_spec((N_EXPERTS, 1)))
    return pl.pallas_call(
        _post_kernel, grid=(nb, nt), in_specs=in_specs, out_specs=out_specs, out_shape=out_shape,
        scratch_shapes=[pltpu.VMEM((N_EXPERTS, 1), F32)],
        compiler_params=_params(("arbitrary", "arbitrary")), name="post",
    )(x, o_lat, o_s5, *mods, *consts)


def _pos_kernel(eidx_ref, rank_ref, pstart_ref, idx_ref, *, rows_pad):
    eidx = eidx_ref[...]
    tt = eidx.shape[1]
    eio = lax.broadcasted_iota(I32, (N_EXPERTS, tt), 0)
    pstart = pstart_ref[...]
    starts = [jnp.sum(jnp.where(eio == eidx[k:k + 1, :], pstart, 0.0), axis=0, keepdims=True)
              for k in range(TOP_K)]
    pos = jnp.concatenate(starts, axis=0).astype(I32) + rank_ref[...]
    planes = idx_ref.shape[1] // TOP_K
    for w in range(idx_ref.shape[0]):
        for j in range(planes):
            idx_ref[w, j * TOP_K:(j + 1) * TOP_K, :] = pos[:, w * SC_WINDOW:(w + 1) * SC_WINDOW] + j * rows_pad


def _pair_rows(eidx, rank, pstart, planes, rows_pad, tt):
    K, T = eidx.shape
    wpt = tt // SC_WINDOW
    return pl.pallas_call(
        functools.partial(_pos_kernel, rows_pad=rows_pad), grid=(pl.cdiv(T, tt),),
        in_specs=[pl.BlockSpec((K, tt), lambda i: (0, i)), pl.BlockSpec((K, tt), lambda i: (0, i)),
                  _const_spec((N_EXPERTS, 1))],
        out_specs=pl.BlockSpec((wpt, planes * K, SC_WINDOW), lambda i: (i, 0, 0)),
        out_shape=jax.ShapeDtypeStruct((T // SC_WINDOW, planes * K, SC_WINDOW), I32),
        compiler_params=_params(("arbitrary",)), name="pair_rows",
    )(eidx, rank, pstart)


def _expert_rows(x_ref, o_ref, wg, wu, wd, nvalid, rows):
    planes = x_ref.shape[0]
    x = _unpack_rows([x_ref[j, :rows, :] for j in range(planes)])
    live = lax.broadcasted_iota(I32, x.shape, 0) < nvalid
    x = jnp.where(live, x, jnp.zeros_like(x))
    g = _dot(x, wg.astype(BF16))
    u = _dot(x, wu.astype(BF16))
    y = _dot((_silu(g) * u).astype(BF16), wd.astype(BF16))
    for j, plane in enumerate(_pack_rows(y)):
        o_ref[j, :rows, :] = plane


def _expert_kernel(blk_e_ref, nvalid_ref, nused_ref, first_ref, next_ref, slot_ref,
                   x_ref, wg_hbm, wu_hbm, wd_hbm, o_ref, wg_buf, wu_buf, wd_buf, sem):
    b = pl.program_id(0)
    used = b < nused_ref[0]
    e = blk_e_ref[b]
    slot = slot_ref[e]
    nvalid = nvalid_ref[b]
    half = MOE_ROWS // 2

    def fetch(expert, s):
        return (pltpu.make_async_copy(wg_hbm.at[expert], wg_buf.at[s], sem.at[s, 0]),
                pltpu.make_async_copy(wu_hbm.at[expert], wu_buf.at[s], sem.at[s, 1]),
                pltpu.make_async_copy(wd_hbm.at[expert], wd_buf.at[s], sem.at[s, 2]))

    @pl.when(jnp.logical_and(used, first_ref[b] == 1))
    def _():
        @pl.when(b == 0)
        def _():
            for cp in fetch(e, slot):
                cp.start()

        nxt = next_ref[e]

        @pl.when(nxt >= 0)
        def _():
            for cp in fetch(nxt, 1 - slot):
                cp.start()

        for cp in fetch(e, slot):
            cp.wait()

    @pl.when(jnp.logical_and(used, nvalid > half))
    def _():
        _expert_rows(x_ref, o_ref, wg_buf[slot], wu_buf[slot], wd_buf[slot], nvalid, MOE_ROWS)

    @pl.when(jnp.logical_and(used, nvalid <= half))
    def _():
        _expert_rows(x_ref, o_ref, wg_buf[slot], wu_buf[slot], wd_buf[slot], nvalid, half)
        o_ref[:, half:, :] = jnp.zeros((o_ref.shape[0], MOE_ROWS - half, LANES), o_ref.dtype)


def _experts(xs, plan, wg, wu, wd):
    planes, rows, _ = xs.shape
    nb = rows // MOE_ROWS
    _, D, F = wg.shape

    def xmap(b, blk_e_ref, nvalid_ref, nused_ref, *_):
        return (0, jnp.minimum(b, nused_ref[0] - 1), 0)

    hbm = pl.BlockSpec(memory_space=pl.ANY)
    grid_spec = pltpu.PrefetchScalarGridSpec(
        num_scalar_prefetch=len(plan), grid=(nb,),
        in_specs=[pl.BlockSpec((planes, MOE_ROWS, LANES), xmap), hbm, hbm, hbm],
        out_specs=pl.BlockSpec((planes, MOE_ROWS, LANES), xmap),
        scratch_shapes=[pltpu.VMEM((2, D, F), wg.dtype), pltpu.VMEM((2, D, F), wu.dtype),
                        pltpu.VMEM((2, F, D), wd.dtype), pltpu.SemaphoreType.DMA((2, 3))])
    return pl.pallas_call(
        _expert_kernel, grid_spec=grid_spec, out_shape=jax.ShapeDtypeStruct(xs.shape, U32),
        compiler_params=_params(("arbitrary",)), name="experts",
    )(*plan, xs, wg, wu, wd)


def _final_kernel(base_ref, ys_ref, gate_ref, gtf_ref, gfin_ref, o_ref):
    nwin, planes = ys_ref.shape[:2]
    gate = gate_ref[...]
    routed = jnp.zeros(base_ref.shape[1:], F32)
    for k in range(TOP_K):
        rows = jnp.concatenate([_unpack_rows([ys_ref[w, j, k] for j in range(planes)]) for w in range(nwin)], axis=0)
        routed = routed + gate[:, k:k + 1] * rows.astype(F32)
    y = base_ref[0] + gtf_ref[0] * routed
    o_ref[0] = _rms(y, gfin_ref[...])


def _final(base, ysg, gate_t, gt_f, g_final, tm, row0):
    B, L, D = base.shape
    planes = ysg.shape[1]
    wpt = tm // SC_WINDOW
    nt = L // tm
    off = row0 // tm
    per_row = gt_f.shape[1] != 1
    mod_spec = (pl.BlockSpec((1, tm, D), lambda b, i: (b, i, 0)) if per_row
                else pl.BlockSpec((1, 1, D), lambda b, i: (b, 0, 0)))
    return pl.pallas_call(
        _final_kernel, grid=(B, nt),
        in_specs=[pl.BlockSpec((1, tm, D), lambda b, i: (b, i, 0)),
                  pl.BlockSpec((wpt, planes, TOP_K, SC_WINDOW, LANES), lambda b, i: (off + b * nt + i, 0, 0, 0, 0)),
                  pl.BlockSpec((tm, TOP_K), lambda b, i: (off + b * nt + i, 0)),
                  mod_spec, _const_spec((1, D))],
        out_specs=pl.BlockSpec((1, tm, D), lambda b, i: (b, i, 0)),
        out_shape=jax.ShapeDtypeStruct((B, L, D), F32),
        compiler_params=_params(("arbitrary", "arbitrary")), name="final",
    )(base, ysg, gate_t, gt_f, g_final)


def _sc_mesh():
    return plsc.VectorSubcoreMesh(core_axis_name="c", subcore_axis_name="s")


def _sc_worker():
    return lax.axis_index("s") * SC_CORES + lax.axis_index("c")


def _sc_dispatch(srcs, idx, out_rows):
    planes = srcs[0].shape[0]
    wins = [s.shape[1] // SC_WINDOW for s in srcs]
    n_win = sum(wins)
    n_iter = -(-n_win // SC_WORKERS)
    flat = [s.reshape(planes * s.shape[1], LANES) for s in srcs]

    @functools.partial(
        pl.kernel, mesh=_sc_mesh(), out_type=jax.ShapeDtypeStruct((out_rows, LANES), srcs[0].dtype),
        scratch_types=[pltpu.VMEM((planes * TOP_K, SC_WINDOW), I32),
                       pltpu.VMEM((planes, SC_WINDOW, LANES), srcs[0].dtype),
                       pltpu.SemaphoreType.DMA((planes,)), pltpu.SemaphoreType.DMA],
        name="sc_dispatch")
    def k(*refs):
        src_hbms, (idx_hbm, out_hbm, idx_v, rows_v, load_sem, scat_sem) = refs[:len(srcs)], refs[len(srcs):]
        wid = _sc_worker()

        def window(src_hbm, n_tok, win, local_win):
            t0 = local_win * SC_WINDOW
            loads = [pltpu.make_async_copy(
                src_hbm.at[pl.ds(pl.multiple_of(j * n_tok + t0, SC_WINDOW), SC_WINDOW)],
                rows_v.at[j], load_sem.at[j]) for j in range(planes)]
            for ld in loads:
                ld.start()
            pltpu.sync_copy(idx_hbm.at[win], idx_v)
            scatters = []
            for j in range(planes):
                loads[j].wait()
                for kk in range(TOP_K):
                    cp = pltpu.make_async_copy(rows_v.at[j], out_hbm.at[idx_v.at[j * TOP_K + kk]], scat_sem)
                    cp.start()
                    scatters.append(cp)
            for cp in scatters:
                cp.wait()

        @pl.loop(0, n_iter)
        def _(i):
            win = i * SC_WORKERS + wid
            first = 0
            for src_hbm, src, nw in zip(src_hbms, srcs, wins):
                @pl.when(jnp.logical_and(win >= first, win < first + nw))
                def _(src_hbm=src_hbm, n_tok=src.shape[1], first=first):
                    window(src_hbm, n_tok, win, win - first)
                first += nw

    return k(*flat, idx)


def _sc_gather(table, idx):
    n_chunks, win = idx.shape
    n_iter = n_chunks // SC_WORKERS
    assert n_chunks % SC_WORKERS == 0 and n_iter % 2 == 0

    @functools.partial(
        pl.kernel, mesh=_sc_mesh(), out_type=jax.ShapeDtypeStruct((n_chunks * win, LANES), table.dtype),
        scratch_types=[pltpu.VMEM((n_iter, win), I32), pltpu.VMEM((2, win, LANES), table.dtype),
                       pltpu.SemaphoreType.DMA((2,)), pltpu.SemaphoreType.DMA((2,))],
        name="sc_gather")
    def k(table_hbm, idx_hbm, out_hbm, idx_v, rows_v, gather_sem, write_sem):
        wid = _sc_worker()
        c0 = wid * n_iter
        pltpu.sync_copy(idx_hbm.at[wid], idx_v)

        def gather(c, b):
            return pltpu.make_async_copy(table_hbm.at[idx_v.at[c]], rows_v.at[b], gather_sem.at[b])

        def write(c, b):
            return pltpu.make_async_copy(rows_v.at[b], out_hbm.at[pl.ds(pl.multiple_of((c0 + c) * win, win), win)],
                                         write_sem.at[b])

        gather(0, 0).start()

        @pl.loop(0, n_iter, step=2)
        def _(i):
            for b in range(2):
                c = i + b
                other = 1 - b

                @pl.when(c >= 1)
                def _():
                    write(c - 1, other).wait()

                @pl.when(c + 1 < n_iter)
                def _():
                    gather(c + 1, other).start()

                gather(c, b).wait()
                write(c, b).start()

        write(n_iter - 1, (n_iter - 1) % 2).wait()

    return k(table, idx.reshape(SC_WORKERS, n_iter, win))


def _moe_plan(eidx, rank, counts, planes):
    T = eidx.shape[1]
    counts = counts.reshape(N_EXPERTS).astype(I32)
    nblk = (counts + MOE_ROWS - 1) // MOE_ROWS
    blk_end = jnp.cumsum(nblk)
    blk_start = blk_end - nblk
    nb_max = (T * TOP_K) // MOE_ROWS + N_EXPERTS
    nused = blk_end[-1]
    bidx = jnp.arange(nb_max, dtype=I32)
    last = jnp.minimum(bidx, nused - 1)
    blk_e = jnp.minimum(jnp.sum((blk_end[None, :] <= last[:, None]).astype(I32), axis=1), N_EXPERTS - 1)
    mine = blk_e[:, None] == jnp.arange(N_EXPERTS, dtype=I32)[None, :]
    cnt_b = jnp.sum(jnp.where(mine, counts[None, :], 0), axis=1)
    start_b = jnp.sum(jnp.where(mine, blk_start[None, :], 0), axis=1)
    nvalid = jnp.clip(cnt_b - (bidx - start_b) * MOE_ROWS, 0, MOE_ROWS).astype(I32)
    first = jnp.logical_and(bidx < nused, jnp.concatenate([jnp.ones((1,), bool), blk_e[1:] != blk_e[:-1]]))
    has = counts > 0
    eids = jnp.arange(N_EXPERTS, dtype=I32)
    nxt_or_self = lax.cummin(jnp.where(has, eids, N_EXPERTS), axis=0, reverse=True)
    next_used = jnp.concatenate([nxt_or_self[1:], jnp.full((1,), N_EXPERTS, I32)])
    next_used = jnp.where(next_used < N_EXPERTS, next_used, -1).astype(I32)
    slot = ((jnp.cumsum(has.astype(I32)) - 1) % 2).astype(I32)
    rows_pad = nb_max * MOE_ROWS
    pstart = (blk_start * MOE_ROWS).astype(F32).reshape(N_EXPERTS, 1)
    idx = _pair_rows(eidx, rank, pstart, planes, rows_pad, 1024)
    plan = (blk_e.astype(I32), nvalid, nused.reshape(1).astype(I32), first.astype(I32), next_used, slot)
    return idx, plan, rows_pad


def _moe_group(h2ps, eidx, rank, counts, w_gate, w_up, w_down):
    planes = h2ps[0].shape[0]
    T = eidx.shape[1]
    assert (T * TOP_K) % MOE_ROWS == 0 and T % 256 == 0
    idx, plan, rows_pad = _moe_plan(eidx, rank, counts, planes)
    n_win = T // SC_WINDOW
    xs = _sc_dispatch(h2ps, idx, planes * rows_pad)
    ys = _experts(xs.reshape(planes, rows_pad, LANES), plan, w_gate, w_up, w_down)
    ysg = _sc_gather(ys.reshape(planes * rows_pad, LANES), idx.reshape(n_win * planes * TOP_K, SC_WINDOW))
    return ysg.reshape(n_win, planes, TOP_K, SC_WINDOW, LANES)


def _rope_tables(pos):
    half = ROPE_DIM // 2
    inv = ROPE_THETA ** (-jnp.arange(half, dtype=F32) / half)
    ang = pos.astype(F32)[:, None] * inv[None, :]
    cos = jnp.tile(jnp.cos(ang), (1, 2 * MLA_HEADS))
    sin = jnp.tile(jnp.sin(ang), (1, 2 * MLA_HEADS))
    return cos, sin


def _rot_cols(w):
    half = ROPE_DIM // 2
    return jnp.concatenate([-w[..., half:], w[..., :half]], axis=-1)


def _layer_weights(w_in, g_mix, g_q, g_kv, w_uq, w_uk, w_uv, s5_D, s5_w_glu, s5_b_glu, g_out_attn, g_out_s5,
                   w_out, g_ffn, w_router, b_router, w_sh_gate, w_sh_up, w_sh_down):
    D = w_in.shape[0]
    o1, o2, o3 = Q_LORA, Q_LORA + KV_LORA, Q_LORA + KV_LORA + ROPE_DIM
    s5w = w_in.shape[1] - o3
    w_rope = w_in[:, o2:o3]
    w_in_ext = jnp.concatenate([w_in[:, :o2], w_in[:, o3:], w_rope, _rot_cols(w_rope)], axis=1).astype(BF16)
    wq = w_uq.reshape(Q_LORA, MLA_HEADS, NOPE_DIM + ROPE_DIM)
    wq_rope = wq[:, :, NOPE_DIM:]
    wr_hi, wr_lo = _split(w_router.T)
    return {
        "s5w": s5w,
        "g_mix": g_mix.reshape(1, D), "w_in": w_in_ext,
        "g_q": g_q.reshape(1, Q_LORA), "g_kv": g_kv.reshape(1, KV_LORA),
        "wq_nope": wq[:, :, :NOPE_DIM].reshape(Q_LORA, MLA_HEADS * NOPE_DIM).astype(BF16),
        "wq_rope": wq_rope.reshape(Q_LORA, MLA_HEADS * ROPE_DIM).astype(BF16),
        "wq_rot": _rot_cols(wq_rope).reshape(Q_LORA, MLA_HEADS * ROPE_DIM).astype(BF16),
        "w_uk": w_uk.reshape(KV_LORA, MLA_HEADS * NOPE_DIM).astype(BF16),
        "w_uv": jnp.transpose(w_uv, (1, 0, 2)).astype(BF16),
        "s5_d": s5_D.reshape(1, s5w), "w_glu": s5_w_glu.astype(BF16), "b_glu": s5_b_glu.reshape(1, s5w),
        "g_out_attn": g_out_attn.reshape(1, -1), "g_out_s5": g_out_s5.reshape(1, s5w),
        "w_out": w_out.astype(BF16), "g_ffn": g_ffn.reshape(1, D),
        "wr_hi": wr_hi, "wr_lo": wr_lo, "b_router": b_router.reshape(N_EXPERTS, 1),
        "w_sg": w_sh_gate.astype(BF16), "w_su": w_sh_up.astype(BF16), "w_sd": w_sh_down.astype(BF16),
    }


def _state_in(s_re, s_im):
    B = s_re.shape[0]
    return jnp.stack([s_re.reshape(B, -1), s_im.reshape(B, -1)], axis=1).astype(F32)


def kernel(x_prompt, x_sample, c_prompt, c_sample, cache_ckv, cache_krope, state_s5_re, state_s5_im, w_ada, b_ada, g_mix, g_ffn, w_in, g_q, w_uq, g_kv, w_uk, w_uv, s5_A_re, s5_A_im, s5_B_re, s5_B_im, s5_C_re, s5_C_im, s5_D, s5_log_dt, s5_w_glu, s5_b_glu, g_out_attn, g_out_s5, w_out, w_router, b_router, w_exp_gate, w_exp_up, w_exp_down, w_sh_gate, w_sh_up, w_sh_down, g_final):
    Bp, Lp, D = x_prompt.shape
    Bs, Ls, _ = x_sample.shape
    depth = w_ada.shape[0]
    assert depth == 1, "single-layer step"
    past = cache_ckv.shape[2]
    G, N = s5_A_re.shape[1:]
    Ts = Bs * Ls
    Tp = Bp * Lp
    l = 0

    wts = _layer_weights(w_in[l], g_mix[l], g_q[l], g_kv[l], w_uq[l], w_uk[l], w_uv[l], s5_D[l], s5_w_glu[l],
                         s5_b_glu[l], g_out_attn[l], g_out_s5[l], w_out[l], g_ffn[l], w_router[l], b_router[l],
                         w_sh_gate[l], w_sh_up[l], w_sh_down[l])

    mod = _ada(jnp.concatenate([c_prompt, c_sample], axis=0), w_ada[l], b_ada[l])
    mod_p = [m.reshape(Bp, 1, D) for m in jnp.split(mod[:Bp], 6, axis=-1)]
    mod_s = [jnp.broadcast_to(m[:, None, :], (Bs, Ls, D)).reshape(1, Ts, D)
             for m in jnp.split(mod[Bp:], 6, axis=-1)]

    tm_p = 256
    tm_pre = 1024
    tm_post = 1024
    tm_s5 = 512
    cos_p, sin_p = _rope_tables(jnp.arange(Lp))
    cos_s, sin_s = _rope_tables(jnp.tile(past + jnp.arange(Ls), Bs))

    ckv_p, kr_p, u_p, kcat_p, v_p, q_p = _pre(x_prompt, mod_p[0], mod_p[1], cos_p, sin_p, wts, tm_pre)
    olat_p = _attn_prompt(q_p, kcat_p, v_p, 512)
    tabs_p = _s5_tables(s5_A_re[l], s5_A_im[l], s5_B_re[l], s5_B_im[l], s5_C_re[l], s5_C_im[l], s5_log_dt[l], tm_s5)
    os5_p, hl_p = _s5(u_p, jnp.zeros((Bp, 2, G * N), F32), tabs_p, wts, tm_s5, precise=False)

    xs_rows = x_sample.reshape(1, Ts, D)
    ckv_s, kr_s, u_s, kcat_s, v_s, q_s = _pre(xs_rows, mod_s[0], mod_s[1], cos_s, sin_s, wts, Ts)
    olat_s = _attn_sample(q_s, kcat_s, v_s, cache_ckv[l], cache_krope[l], wts["w_uk"])
    tabs_s = _s5_tables(s5_A_re[l], s5_A_im[l], s5_B_re[l], s5_B_im[l], s5_C_re[l], s5_C_im[l], s5_log_dt[l], Ls)
    os5_s, hl_s = _s5(u_s.reshape(Bs, Ls, -1), _state_in(state_s5_re[l], state_s5_im[l]), tabs_s, wts, Ls,
                      precise=True)
    h2_p, base_p, eidx_p, gate_p, rank_p, counts_p = _post(
        x_prompt, olat_p, os5_p, [mod_p[2], mod_p[3], mod_p[4], mod_p[5]], wts, tm_post,
        jnp.zeros((N_EXPERTS, 1), F32))
    h2_s, base_s, eidx_s, gate_s, rank_s, counts = _post(
        xs_rows, olat_s, os5_s.reshape(1, Ts, -1), [mod_s[2], mod_s[3], mod_s[4], mod_s[5]], wts, Ts, counts_p)
    ysg = _moe_group([h2_p, h2_s], jnp.concatenate([eidx_p, eidx_s], axis=1),
                     jnp.concatenate([rank_p, rank_s], axis=1), counts, w_exp_gate[l], w_exp_up[l], w_exp_down[l])
    gate_t = jnp.concatenate([gate_p, gate_s], axis=1).T

    gfin = g_final.reshape(1, D)
    y_p = _final(base_p, ysg, gate_t, mod_p[5], gfin, 2 * tm_p, 0)
    y_s = _final(base_s, ysg, gate_t, mod_s[5], gfin, Ts, Tp).reshape(Bs, Ls, D)

    def state_out(hl, B):
        return hl[:, 0].reshape(1, B, G, N), hl[:, 1].reshape(1, B, G, N)

    sre_p, sim_p = state_out(hl_p, Bp)
    sre_s, sim_s = state_out(hl_s, Bs)
    return (y_p, y_s, ckv_p[None], kr_p[None], sre_p, sim_p,
            ckv_s.reshape(1, Bs, Ls, KV_LORA), kr_s.reshape(1, Bs, Ls, ROPE_DIM), sre_s, sim_s)
```
